```python
import math
import jax, jax.numpy as jnp
from jax import lax
import numpy as np

D_MODEL = 1024
BATCH = 8
SEQ = 8192
DEPTH = 2
DEC_BATCH = 32
DEC_SEQ = 32
PAST_LEN = 2048

CHUNK = 64
N_A_LAYERS = DEPTH // 2
N_B_LAYERS = DEPTH - N_A_LAYERS
A_HEADS = 8
A_DK = D_MODEL // (2 * A_HEADS)
A_DV = 2 * A_DK
Q_BLOCK = 128
B_HEADS = 16
B_DH = D_MODEL // B_HEADS
LEFT_CHUNKS = 8
B_WINDOW = LEFT_CHUNKS * CHUNK
B_BAND = (LEFT_CHUNKS + 1) * CHUNK
REL_CLIP = 256
N_GROUPS = 4
EXPERTS_PER_GROUP = 4
N_EXPERTS = N_GROUPS * EXPERTS_PER_GROUP
EXPERT_FF = 512
TOP_K_INNER = 2
ROPE_THETA = 10000.0
EPS = 1e-6

kernel_name = "yoco_diffattn_chunkband_hmoe_step"


def rmsnorm(x, g):
    xf = x.astype(jnp.float32)
    y = xf * lax.rsqrt(jnp.mean(xf * xf, axis=-1, keepdims=True) + EPS)
    return (y * g.astype(jnp.float32)).astype(x.dtype)


def rope(x, pos):
    half = x.shape[-1] // 2
    inv = jnp.power(ROPE_THETA, -jnp.arange(half, dtype=jnp.float32) / half)
    ang = pos.astype(jnp.float32)[:, None] * inv[None, :]
    cos = jnp.cos(ang)[:, None, :]
    sin = jnp.sin(ang)[:, None, :]
    xf = x.astype(jnp.float32)
    x1, x2 = xf[..., :half], xf[..., half:]
    return jnp.concatenate([x1 * cos - x2 * sin, x2 * cos + x1 * sin], axis=-1).astype(x.dtype)


def diff_project(h, pos, w_q, w_k, w_v):
    b, s, _ = h.shape
    q = rope((h @ w_q).reshape(b, s, 2 * A_HEADS, A_DK), pos)
    k = rope((h @ w_k).reshape(b, s, 2 * A_HEADS, A_DK), pos)
    v = (h @ w_v).reshape(b, s, A_HEADS, A_DV)
    return q, k, v


def diff_lambda(lq1, lk1, lq2, lk2, lam_init):
    f = lambda a: a.astype(jnp.float32)
    return jnp.exp(jnp.sum(f(lq1) * f(lk1))) - jnp.exp(jnp.sum(f(lq2) * f(lk2))) + lam_init


def diff_core(q, k, v, lam, mask):
    s = jnp.einsum('bqhmd,bkhmd->bhmqk', q, k).astype(jnp.float32) * (A_DK ** -0.5)
    if mask is not None:
        s = jnp.where(mask, s, -jnp.inf)
    p = jax.nn.softmax(s, axis=-1)
    w = p[:, :, 0] - lam * p[:, :, 1]
    return jnp.einsum('bhqk,bkhe->bqhe', w.astype(v.dtype), v)


def diff_attn_prompt(q, k, v, lam):
    b, s = q.shape[:2]
    nqb = s // Q_BLOCK
    kh = k.reshape(b, s, A_HEADS, 2, A_DK)
    kchunk = jnp.arange(s) // CHUNK
    qb = q.reshape(b, nqb, Q_BLOCK, A_HEADS, 2, A_DK).transpose(1, 0, 2, 3, 4, 5)

    def block(args):
        qi, i = args
        qchunk = (i * Q_BLOCK + jnp.arange(Q_BLOCK)) // CHUNK
        mask = kchunk[None, :] <= qchunk[:, None]
        return diff_core(qi, kh, v, lam, mask)

    o = lax.map(block, (qb, jnp.arange(nqb)))
    return o.transpose(1, 0, 2, 3, 4).reshape(b, s, A_HEADS, A_DV)


def diff_attn_sample(q, k_all, v_all, lam):
    b, t = q.shape[:2]
    n = k_all.shape[1]
    return diff_core(q.reshape(b, t, A_HEADS, 2, A_DK), k_all.reshape(b, n, A_HEADS, 2, A_DK), v_all, lam, None)


def diff_out(o, g_sub, w_o, lam_init):
    b, s = o.shape[:2]
    o = rmsnorm(o, g_sub) * (1.0 - lam_init)
    return o.reshape(b, s, A_HEADS * A_DV) @ w_o


def shared_kv(x, g_kv, w_kv):
    b, s, _ = x.shape
    kv = rmsnorm(x, g_kv) @ w_kv
    k, v = jnp.split(kv, 2, axis=-1)
    return k.reshape(b, s, B_HEADS, B_DH), v.reshape(b, s, B_HEADS, B_DH)


def rel_bias_lookup(rel_bias, rel):
    return jnp.take(rel_bias, jnp.clip(rel, -REL_CLIP, REL_CLIP) + REL_CLIP, axis=1).astype(jnp.float32)


def band_attn_prompt(q, k, v, rel_bias):
    b, s = q.shape[:2]
    nc = s // CHUNK
    pad = ((0, 0), (B_WINDOW, 0), (0, 0), (0, 0))
    kp = jnp.pad(k, pad)
    vp = jnp.pad(v, pad)
    idx = (jnp.arange(nc) * CHUNK)[:, None] + jnp.arange(B_BAND)[None, :]
    valid = idx >= B_WINDOW
    rel = jnp.arange(CHUNK)[:, None] + B_WINDOW - jnp.arange(B_BAND)[None, :]
    bias = rel_bias_lookup(rel_bias, rel)
    qc = q.reshape(b, nc, CHUNK, B_HEADS, B_DH)
    scale = B_DH ** -0.5

    def one(args):
        qi, ki, vi = args
        kb = ki[idx]
        vb = vi[idx]
        sc = jnp.einsum('cqhd,ckhd->chqk', qi, kb).astype(jnp.float32) * scale + bias[None]
        sc = jnp.where(valid[:, None, None, :], sc, -jnp.inf)
        p = jax.nn.softmax(sc, axis=-1)
        return jnp.einsum('chqk,ckhd->cqhd', p.astype(vb.dtype), vb)

    o = lax.map(one, (qc, kp, vp))
    return o.reshape(b, s, B_HEADS * B_DH)


def band_attn_sample(q, k_all, v_all, rel_bias, wb):
    b, t = q.shape[:2]
    rel = jnp.arange(t)[:, None] - (jnp.arange(wb + t) - wb)[None, :]
    bias = rel_bias_lookup(rel_bias, rel)
    sc = jnp.einsum('bqhd,bkhd->bhqk', q, k_all).astype(jnp.float32) * (B_DH ** -0.5) + bias[None]
    p = jax.nn.softmax(sc, axis=-1)
    return jnp.einsum('bhqk,bkhd->bqhd', p.astype(v_all.dtype), v_all).reshape(b, t, B_HEADS * B_DH)


def hier_moe(h, w_rg, b_rg, w_re, b_re, w_gate, w_up, w_down):
    shp = h.shape
    t = h.reshape(-1, D_MODEL)
    gl = (t @ w_rg).astype(jnp.float32) + b_rg.astype(jnp.float32)
    pg = jax.nn.softmax(gl, axis=-1)
    gval, gidx = lax.top_k(pg, 1)
    el_all = jnp.einsum('nd,gde->nge', t, w_re).astype(jnp.float32) + b_re.astype(jnp.float32)
    el = jnp.take_along_axis(el_all, gidx[:, :, None], axis=1)[:, 0]
    ev, eidx = lax.top_k(el, TOP_K_INNER)
    ew = jax.nn.softmax(ev, axis=-1) * gval
    flat = gidx * EXPERTS_PER_GROUP + eidx
    gates = jnp.sum(jax.nn.one_hot(flat, N_EXPERTS, dtype=jnp.float32) * ew[..., None], axis=1)
    out = jnp.zeros_like(t)
    for e in range(N_EXPERTS):
        he = jax.nn.silu(t @ w_gate[e]) * (t @ w_up[e])
        out = out + gates[:, e:e + 1].astype(t.dtype) * (he @ w_down[e])
    return out.reshape(shp)


def setup_inputs(seed: int = 0) -> dict:
    key = jax.random.key(seed)
    ks = jax.random.split(key, 32)
    f32 = jnp.float32
    nrm = lambda k, shp, sc: jax.random.normal(k, shp, f32) * sc
    wb = min(B_WINDOW, PAST_LEN)
    d = D_MODEL
    return {
        "x_prompt": nrm(ks[0], (BATCH, SEQ, d), 1.0),
        "x_sample": nrm(ks[1], (DEC_BATCH, DEC_SEQ, d), 1.0),
        "cache_k_a": nrm(ks[2], (N_A_LAYERS, DEC_BATCH, PAST_LEN, 2 * A_HEADS, A_DK), 1.0),
        "cache_v_a": nrm(ks[3], (N_A_LAYERS, DEC_BATCH, PAST_LEN, A_HEADS, A_DV), 1.0),
        "cache_k_b": nrm(ks[4], (DEC_BATCH, wb, B_HEADS, B_DH), 1.0),
        "cache_v_b": nrm(ks[5], (DEC_BATCH, wb, B_HEADS, B_DH), 1.0),
        "g_attn": 1.0 + nrm(ks[6], (DEPTH, d), 0.01),
        "g_ffn": 1.0 + nrm(ks[7], (DEPTH, d), 0.01),
        "w_q_a": nrm(ks[8], (N_A_LAYERS, d, 2 * A_HEADS * A_DK), d ** -0.5),
        "w_k_a": nrm(ks[9], (N_A_LAYERS, d, 2 * A_HEADS * A_DK), d ** -0.5),
        "w_v_a": nrm(ks[10], (N_A_LAYERS, d, A_HEADS * A_DV), d ** -0.5),
        "lam_q1": nrm(ks[11], (N_A_LAYERS, A_DK), 0.1),
        "lam_k1": nrm(ks[12], (N_A_LAYERS, A_DK), 0.1),
        "lam_q2": nrm(ks[13], (N_A_LAYERS, A_DK), 0.1),
        "lam_k2": nrm(ks[14], (N_A_LAYERS, A_DK), 0.1),
        "g_sub_a": 1.0 + nrm(ks[15], (N_A_LAYERS, A_DV), 0.01),
        "w_o_a": nrm(ks[16], (N_A_LAYERS, A_HEADS * A_DV, d), (A_HEADS * A_DV) ** -0.5),
        "g_kv": 1.0 + nrm(ks[17], (d,), 0.01),
        "w_kv_b": nrm(ks[18], (d, 2 * B_HEADS * B_DH), d ** -0.5),
        "w_q_b": nrm(ks[19], (N_B_LAYERS, d, B_HEADS * B_DH), d ** -0.5),
        "rel_bias_b": nrm(ks[20], (N_B_LAYERS, B_HEADS, 2 * REL_CLIP + 1), 0.2),
        "w_o_b": nrm(ks[21], (N_B_LAYERS, B_HEADS * B_DH, d), (B_HEADS * B_DH) ** -0.5),
        "w_router_g": nrm(ks[22], (DEPTH, d, N_GROUPS), d ** -0.5),
        "b_router_g": nrm(ks[23], (DEPTH, N_GROUPS), 0.01),
        "w_router_e": nrm(ks[24], (DEPTH, N_GROUPS, d, EXPERTS_PER_GROUP), d ** -0.5),
        "b_router_e": nrm(ks[25], (DEPTH, N_GROUPS, EXPERTS_PER_GROUP), 0.01),
        "w_gate": nrm(ks[26], (DEPTH, N_EXPERTS, d, EXPERT_FF), d ** -0.5),
        "w_up": nrm(ks[27], (DEPTH, N_EXPERTS, d, EXPERT_FF), d ** -0.5),
        "w_down": nrm(ks[28], (DEPTH, N_EXPERTS, EXPERT_FF, d), EXPERT_FF ** -0.5),
        "g_final": 1.0 + nrm(ks[29], (d,), 0.01),
    }


def reference(x_prompt, x_sample, cache_k_a, cache_v_a, cache_k_b, cache_v_b,
              g_attn, g_ffn, w_q_a, w_k_a, w_v_a, lam_q1, lam_k1, lam_q2, lam_k2, g_sub_a, w_o_a,
              g_kv, w_kv_b, w_q_b, rel_bias_b, w_o_b,
              w_router_g, b_router_g, w_router_e, b_router_e, w_gate, w_up, w_down, g_final):
    bp, sp = x_prompt.shape[:2]
    bs, ts = x_sample.shape[:2]
    past = cache_k_a.shape[2]
    wb = cache_k_b.shape[1]
    pos_p = jnp.arange(sp, dtype=jnp.int32)
    pos_s = past + jnp.arange(ts, dtype=jnp.int32)
    xp, xs = x_prompt, x_sample
    ka_p, va_p, ka_s, va_s = [], [], [], []
    kb_p = vb_p = kb_s = vb_s = None
    for l in range(DEPTH):
        if l < N_A_LAYERS:
            i = l
            lam_init = 0.8 - 0.6 * math.exp(-0.3 * l)
            lam = diff_lambda(lam_q1[i], lam_k1[i], lam_q2[i], lam_k2[i], lam_init)
            qp, kp, vp = diff_project(rmsnorm(xp, g_attn[l]), pos_p, w_q_a[i], w_k_a[i], w_v_a[i])
            xp = xp + diff_out(diff_attn_prompt(qp, kp, vp, lam), g_sub_a[i], w_o_a[i], lam_init)
            qs, ks_, vs_ = diff_project(rmsnorm(xs, g_attn[l]), pos_s, w_q_a[i], w_k_a[i], w_v_a[i])
            k_all = jnp.concatenate([cache_k_a[i], ks_], axis=1)
            v_all = jnp.concatenate([cache_v_a[i], vs_], axis=1)
            xs = xs + diff_out(diff_attn_sample(qs, k_all, v_all, lam), g_sub_a[i], w_o_a[i], lam_init)
            ka_p.append(kp); va_p.append(vp); ka_s.append(ks_); va_s.append(vs_)
        else:
            if l == N_A_LAYERS:
                kb_p, vb_p = shared_kv(xp, g_kv, w_kv_b)
                kn, vn = shared_kv(xs, g_kv, w_kv_b)
                kb_s = jnp.concatenate([cache_k_b, kn], axis=1)
                vb_s = jnp.concatenate([cache_v_b, vn], axis=1)
            j = l - N_A_LAYERS
            qp = (rmsnorm(xp, g_attn[l]) @ w_q_b[j]).reshape(bp, sp, B_HEADS, B_DH)
            xp = xp + band_attn_prompt(qp, kb_p, vb_p, rel_bias_b[j]) @ w_o_b[j]
            qs = (rmsnorm(xs, g_attn[l]) @ w_q_b[j]).reshape(bs, ts, B_HEADS, B_DH)
            xs = xs + band_attn_sample(qs, kb_s, vb_s, rel_bias_b[j], wb) @ w_o_b[j]
        xp = xp + hier_moe(rmsnorm(xp, g_ffn[l]), w_router_g[l], b_router_g[l], w_router_e[l], b_router_e[l],
                           w_gate[l], w_up[l], w_down[l])
        xs = xs + hier_moe(rmsnorm(xs, g_ffn[l]), w_router_g[l], b_router_g[l], w_router_e[l], b_router_e[l],
                           w_gate[l], w_up[l], w_down[l])
    y_prompt = rmsnorm(xp, g_final)
    y_sample = rmsnorm(xs, g_final)
    wp = min(B_WINDOW, sp)
    k_a_prompt = jnp.stack(ka_p, axis=0)
    v_a_prompt = jnp.stack(va_p, axis=0)
    k_a_sample = jnp.stack(ka_s, axis=0)
    v_a_sample = jnp.stack(va_s, axis=0)
    k_b_prompt = kb_p[:, sp - wp:]
    v_b_prompt = vb_p[:, sp - wp:]
    k_b_sample = kb_s[:, ts:]
    v_b_sample = vb_s[:, ts:]
    return (y_prompt, y_sample, k_a_prompt, v_a_prompt, k_b_prompt, v_b_prompt,
            k_a_sample, v_a_sample, k_b_sample, v_b_sample)
```

```python
import functools
import math

import jax
import jax.numpy as jnp
from jax import lax
from jax.experimental import pallas as pl
from jax.experimental.pallas import tpu as pltpu

F32 = jnp.float32
BF16 = jnp.bfloat16

EPS = 1e-6
CHUNK = 64
D_MODEL = 1024
A_HEADS = 8
A_DK = 64
A_DV = 128
B_HEADS = 16
B_DH = 64
B_WINDOW = 512
REL_CLIP = 256
N_GROUPS = 4
EXPERTS_PER_GROUP = 4
N_EXPERTS = 16
EXPERT_FF = 512
ROPE_THETA = 10000.0

LANES = 128
NEG = -1e30
VMEM_LIMIT = 56 * 1024 * 1024

ROW_TILE = 512
MOE_TILE = 1024
ATT_TILE = 512
BAND_TQ = 256
BAND_KW = BAND_TQ + B_WINDOW


def _params(*sem):
    return pltpu.CompilerParams(dimension_semantics=sem, vmem_limit_bytes=VMEM_LIMIT)


def _rms_scale(x):
    return x * lax.rsqrt(jnp.mean(x * x, axis=-1, keepdims=True) + EPS)


def _dot(a, b):
    return jnp.dot(a, b, preferred_element_type=F32)


def _dot_nt(a, b):
    return lax.dot_general(a, b, (((1,), (1,)), ((), ())), preferred_element_type=F32)


def _chunk_of(pos):
    return jnp.right_shift(pos, CHUNK.bit_length() - 1)


def _stack_halves(q):
    lane = lax.broadcasted_iota(jnp.int32, q.shape, 1)
    zero = jnp.zeros_like(q)
    return jnp.concatenate([jnp.where(lane < 64, q, zero), jnp.where(lane >= 64, q, zero)], axis=0)


def _rope_table_kernel(inv_ref, cos_ref, sin_ref, *, period, offset, rows):
    i = pl.program_id(0)
    row = lax.broadcasted_iota(jnp.int32, (rows, LANES), 0) + i * rows
    lane = lax.broadcasted_iota(jnp.int32, (rows, LANES), 1)
    pos = (offset + lax.rem(row, period)).astype(F32)
    ang = pos * inv_ref[...]
    s = jnp.sin(ang)
    cos_ref[...] = jnp.cos(ang)
    sin_ref[...] = jnp.where(lax.rem(lane, 64) < 32, -s, s)


def _rope_table(n_rows, period, offset, rows):
    half = A_DK // 2
    inv = jnp.power(ROPE_THETA, -jnp.arange(half, dtype=F32) / half)
    inv = jnp.tile(inv, LANES // half)[None, :]
    return pl.pallas_call(
        functools.partial(_rope_table_kernel, period=period, offset=offset, rows=rows),
        grid=(n_rows // rows,),
        in_specs=[pl.BlockSpec((1, LANES), lambda i: (0, 0))],
        out_specs=[pl.BlockSpec((rows, LANES), lambda i: (i, 0))] * 2,
        out_shape=[jax.ShapeDtypeStruct((n_rows, LANES), F32)] * 2,
        compiler_params=_params("arbitrary"),
        name="rope_table",
    )(inv)


def _qkv_a_kernel(x_ref, g_ref, wq_ref, wk_ref, wv_ref, cos_ref, sin_ref,
                  q_ref, kf_ref, kb_ref, vf_ref, vb_ref, *, head_major):
    h = (_rms_scale(x_ref[0]) * g_ref[...]).astype(BF16)
    cos = cos_ref[...]
    sin = sin_ref[...]
    lane = lax.broadcasted_iota(jnp.int32, cos.shape, 1)
    first = lax.rem(lane, 64) < 32

    def put(ref, c, val):
        if head_major:
            ref[0, c] = val
        else:
            ref[0, :, c * LANES:(c + 1) * LANES] = val

    yq = _dot(h, wq_ref[...])
    yk = _dot(h, wk_ref[...])
    yv = _dot(h, wv_ref[...])
    for c in range(D_MODEL // LANES):
        sl = slice(c * LANES, (c + 1) * LANES)
        for y, is_q in ((yq, True), (yk, False)):
            yc = y[:, sl]
            rot = jnp.where(first, pltpu.roll(yc, LANES - 32, 1), pltpu.roll(yc, 32, 1))
            r = yc * cos + rot * sin
            if is_q:
                put(q_ref, c, (r * (A_DK ** -0.5)).astype(BF16))
            else:
                kf_ref[0, :, sl] = r
                put(kb_ref, c, r.astype(BF16))
        vc = yv[:, sl]
        vf_ref[0, :, sl] = vc
        put(vb_ref, c, vc.astype(BF16))


def _qkv_a(x3, g, wq, wk, wv, cos, sin, head_major):
    b, s, d = x3.shape
    tm = min(ROW_TILE, s)
    n_tab = cos.shape[0] // tm
    nh = d // LANES
    if head_major:
        hspec = pl.BlockSpec((1, nh, tm, LANES), lambda bi, i: (bi, 0, i, 0))
        hshape = jax.ShapeDtypeStruct((b, nh, s, LANES), BF16)
    else:
        hspec = pl.BlockSpec((1, tm, d), lambda bi, i: (bi, i, 0))
        hshape = jax.ShapeDtypeStruct((b, s, d), BF16)
    tok = pl.BlockSpec((1, tm, d), lambda bi, i: (bi, i, 0))
    wspec = pl.BlockSpec((d, d), lambda bi, i: (0, 0))
    tab = pl.BlockSpec((tm, LANES), lambda bi, i: (i % n_tab, 0))
    fshape = jax.ShapeDtypeStruct((b, s, d), F32)
    return pl.pallas_call(
        functools.partial(_qkv_a_kernel, head_major=head_major),
        grid=(b, s // tm),
        in_specs=[tok, pl.BlockSpec((1, d), lambda bi, i: (0, 0)), wspec, wspec, wspec, tab, tab],
        out_specs=[hspec, tok, hspec, tok, hspec],
        out_shape=[hshape, fshape, hshape, fshape, hshape],
        compiler_params=_params("parallel", "parallel"),
        name="qkv_a",
    )(x3, g, wq, wk, wv, cos, sin)


def _diff_lambda(lam_ref, lam_init):
    lv = lam_ref[...]
    s1 = jnp.sum(lv[0:1] * lv[1:2], axis=-1, keepdims=True)
    s2 = jnp.sum(lv[2:3] * lv[3:4], axis=-1, keepdims=True)
    return jnp.exp(s1) - jnp.exp(s2) + lam_init


def _diff_finish(acc, l, lam, gsub, lam_init, t):
    o = acc * (1.0 / l)
    d = o[:t] - lam * o[t:]
    return (_rms_scale(d) * gsub * (1.0 - lam_init)).astype(BF16)


def _softmax_step(qs, k, v, m, l, acc, mask):
    s = _dot_nt(qs, k)
    if mask is not None:
        s = jnp.where(mask, s, NEG)
    m_new = jnp.maximum(m, jnp.max(s, axis=-1, keepdims=True))
    p = jnp.exp(s - m_new)
    alpha = jnp.exp(m - m_new)
    l = alpha * l + jnp.sum(p, axis=-1, keepdims=True)
    acc = alpha * acc + _dot(p.astype(BF16), v)
    return m_new, l, acc


def _diff_prompt_kernel(lam_ref, gsub_ref, q_ref, k_ref, v_ref, o_ref, *, t, lam_init):
    qi = pl.program_id(2)
    qs = _stack_halves(q_ref[0, 0])

    def kv(i):
        off = pl.multiple_of(i * t, t)
        return k_ref[0, 0, pl.ds(off, t), :], v_ref[0, 0, pl.ds(off, t), :]

    def body(i, carry):
        k, v = kv(i)
        return _softmax_step(qs, k, v, *carry, None)

    init = (jnp.full((2 * t, 1), NEG, F32), jnp.zeros((2 * t, 1), F32), jnp.zeros((2 * t, A_DV), F32))
    m, l, acc = lax.fori_loop(0, qi, body, init)
    r = lax.broadcasted_iota(jnp.int32, (2 * t, t), 0)
    c = lax.broadcasted_iota(jnp.int32, (2 * t, t), 1)
    mask = _chunk_of(c) <= _chunk_of(jnp.bitwise_and(r, t - 1))
    k, v = kv(qi)
    m, l, acc = _softmax_step(qs, k, v, m, l, acc, mask)
    o_ref[0] = _diff_finish(acc, l, _diff_lambda(lam_ref, lam_init), gsub_ref[...], lam_init, t)


def _diff_prompt(lamv, gsub, q, k, v, lam_init):
    b, nh, s, _ = q.shape
    t = min(ATT_TILE, s)
    full = pl.BlockSpec((1, 1, s, LANES), lambda bi, hi, qi: (bi, hi, 0, 0))
    return pl.pallas_call(
        functools.partial(_diff_prompt_kernel, t=t, lam_init=lam_init),
        grid=(b, nh, s // t),
        in_specs=[pl.BlockSpec(lamv.shape, lambda bi, hi, qi: (0, 0)),
                  pl.BlockSpec((1, A_DV), lambda bi, hi, qi: (0, 0)),
                  pl.BlockSpec((1, 1, t, LANES), lambda bi, hi, qi: (bi, hi, qi, 0)), full, full],
        out_specs=pl.BlockSpec((1, t, LANES), lambda bi, hi, qi: (bi, qi, hi)),
        out_shape=jax.ShapeDtypeStruct((b, s, nh * LANES), BF16),
        compiler_params=_params("parallel", "parallel", "arbitrary"),
        name="diff_attn_prompt",
    )(lamv, gsub, q, k, v)


def _pad_rows(x, rows):
    return jnp.concatenate([x, jnp.zeros((rows - x.shape[0], x.shape[1]), x.dtype)], axis=0)


def _diff_sample_kernel(lam_ref, gsub_ref, q_ref, kc_ref, kn_ref, vc_ref, vn_ref, o_ref, *, lam_init):
    t = q_ref.shape[1]
    qs = _stack_halves(q_ref[0])
    kc = kc_ref[0].astype(BF16)
    vc = vc_ref[0].astype(BF16)
    kn = _pad_rows(kn_ref[0], LANES)
    vn = _pad_rows(vn_ref[0], LANES)
    sc = _dot_nt(qs, kc)
    sn = _dot_nt(qs, kn)
    col = lax.broadcasted_iota(jnp.int32, sn.shape, 1)
    sn = jnp.where(col < t, sn, NEG)
    m = jnp.maximum(jnp.max(sc, axis=-1, keepdims=True), jnp.max(sn, axis=-1, keepdims=True))
    pc = jnp.exp(sc - m)
    pn = jnp.exp(sn - m)
    l = jnp.sum(pc, axis=-1, keepdims=True) + jnp.sum(pn, axis=-1, keepdims=True)
    acc = _dot(pc.astype(BF16), vc) + _dot(pn.astype(BF16), vn)
    o_ref[0] = _diff_finish(acc, l, _diff_lambda(lam_ref, lam_init), gsub_ref[...], lam_init, t)


def _diff_sample(lamv, gsub, q, kc, kn, vc, vn, lam_init):
    b, t, d = q.shape
    past = kc.shape[1]
    new = pl.BlockSpec((1, t, LANES), lambda bi, hi: (bi, 0, hi))
    old = pl.BlockSpec((1, past, LANES), lambda bi, hi: (bi, 0, hi))
    return pl.pallas_call(
        functools.partial(_diff_sample_kernel, lam_init=lam_init),
        grid=(b, d // LANES),
        in_specs=[pl.BlockSpec(lamv.shape, lambda bi, hi: (0, 0)),
                  pl.BlockSpec((1, A_DV), lambda bi, hi: (0, 0)), new, old, new, old, new],
        out_specs=new,
        out_shape=jax.ShapeDtypeStruct((b, t, d), BF16),
        compiler_params=_params("parallel", "parallel"),
        name="diff_attn_sample",
    )(lamv, gsub, q, kc, kn, vc, vn)


def _out_proj_kernel(a_ref, x_ref, w_ref, o_ref):
    o_ref[...] = x_ref[...] + _dot(a_ref[...], w_ref[...])


def _out_proj(a, x, w):
    n, d = x.shape
    tm = min(ROW_TILE, n)
    tok = pl.BlockSpec((tm, d), lambda i: (i, 0))
    return pl.pallas_call(
        _out_proj_kernel,
        grid=(n // tm,),
        in_specs=[tok, tok, pl.BlockSpec((d, d), lambda i: (0, 0))],
        out_specs=tok,
        out_shape=jax.ShapeDtypeStruct((n, d), F32),
        compiler_params=_params("parallel"),
        name="out_proj",
    )(a, x, w)


def _moe_gates(hf, wr_ref, br_ref):
    lg = jnp.dot(hf, wr_ref[...], preferred_element_type=F32, precision=lax.Precision.HIGHEST) + br_ref[...]
    lane = lax.broadcasted_iota(jnp.int32, lg.shape, 1).astype(F32)
    big = float(LANES)

    def first_argmax(v):
        mx = jnp.max(v, axis=-1, keepdims=True)
        return mx, jnp.min(jnp.where(v == mx, lane, big), axis=-1, keepdims=True)

    is_g = lane < N_GROUPS
    gmax, gidx = first_argmax(jnp.where(is_g, lg, NEG))
    gval = 1.0 / jnp.sum(jnp.where(is_g, jnp.exp(lg - gmax), 0.0), axis=-1, keepdims=True)
    lo = N_GROUPS + EXPERTS_PER_GROUP * gidx
    el = jnp.where((lane >= lo) & (lane < lo + EXPERTS_PER_GROUP), lg, NEG)
    v1, i1 = first_argmax(el)
    v2, i2 = first_argmax(jnp.where(lane == i1, NEG, el))
    e2 = jnp.exp(v2 - v1)
    inv = gval / (1.0 + e2)
    return jnp.where(lane == i1, inv, 0.0) + jnp.where(lane == i2, inv * e2, 0.0)


def _moe_kernel(x_ref, g_ref, wr_ref, br_ref, wgu_ref, wd_ref, gf_ref, o_ref, h_ref, gate_ref, *, final_norm):
    e = pl.program_id(1)

    @pl.when(e == 0)
    def _():
        x = x_ref[...]
        hf = _rms_scale(x) * g_ref[...]
        h_ref[...] = hf.astype(BF16)
        gate_ref[...] = _moe_gates(hf, wr_ref, br_ref)
        o_ref[...] = x

    gu = _dot(h_ref[...], wgu_ref[0])
    a = gu[:, :EXPERT_FF]
    he = (a * (1.0 / (1.0 + jnp.exp(-a))) * gu[:, EXPERT_FF:]).astype(BF16)
    y = _dot(he, wd_ref[0])
    gates = gate_ref[...]
    lane = lax.broadcasted_iota(jnp.int32, gates.shape, 1)
    ge = jnp.sum(jnp.where(lane == e + N_GROUPS, gates, 0.0), axis=-1, keepdims=True)
    o_ref[...] += ge * y

    if final_norm:
        @pl.when(e == N_EXPERTS - 1)
        def _():
            o_ref[...] = _rms_scale(o_ref[...]) * gf_ref[...]


def _moe(x, g, wr, br, wgu, wd, gf, final_norm):
    n, d = x.shape
    tm = min(MOE_TILE, n)
    tok = pl.BlockSpec((tm, d), lambda i, e: (i, 0))
    vec = pl.BlockSpec((1, d), lambda i, e: (0, 0))
    return pl.pallas_call(
        functools.partial(_moe_kernel, final_norm=final_norm),
        grid=(n // tm, N_EXPERTS),
        in_specs=[tok, vec,
                  pl.BlockSpec((d, LANES), lambda i, e: (0, 0)),
                  pl.BlockSpec((1, LANES), lambda i, e: (0, 0)),
                  pl.BlockSpec((1, d, 2 * EXPERT_FF), lambda i, e: (e, 0, 0)),
                  pl.BlockSpec((1, EXPERT_FF, d), lambda i, e: (e, 0, 0)),
                  vec],
        out_specs=tok,
        out_shape=jax.ShapeDtypeStruct((n, d), F32),
        scratch_shapes=[pltpu.VMEM((tm, d), BF16), pltpu.VMEM((tm, LANES), F32)],
        compiler_params=_params("parallel", "arbitrary"),
        name="hier_moe",
    )(x, g, wr, br, wgu, wd, gf)


def _kvq_b_kernel(x_ref, gkv_ref, gq_ref, wkv_ref, wq_ref, k_ref, v_ref, q_ref, *, lead_pad):
    def compute():
        xh = _rms_scale(x_ref[0])
        kv = _dot((xh * gkv_ref[...]).astype(BF16), wkv_ref[...])
        k_ref[0] = kv[:, :D_MODEL].astype(k_ref.dtype)
        v_ref[0] = kv[:, D_MODEL:].astype(v_ref.dtype)
        q = _dot((xh * gq_ref[...]).astype(BF16), wq_ref[...])
        q_ref[0] = (q * (B_DH ** -0.5)).astype(BF16)

    if lead_pad:
        i = pl.program_id(1)

        @pl.when(i == 0)
        def _():
            k_ref[...] = jnp.zeros_like(k_ref)
            v_ref[...] = jnp.zeros_like(v_ref)

        pl.when(i > 0)(compute)
    else:
        compute()


def _kvq_b(x3, gkv, gq, wkv, wq, kv_dtype, lead_pad, row_block=None):
    b, s, d = x3.shape
    tm = min(ROW_TILE, s)
    if lead_pad:
        assert tm == B_WINDOW
        grid = (b, s // tm + 1)
        src = lambda bi, i: (bi, jnp.maximum(i - 1, 0), 0)
        dst = lambda bi, i: (bi, i, 0)
        s_kv, s_q = s + B_WINDOW, s
    elif row_block is not None:
        grid = (b, 1)
        src = lambda bi, i: (bi, row_block, 0)
        dst = lambda bi, i: (bi, 0, 0)
        s_kv = s_q = tm
    else:
        grid = (b, s // tm)
        src = dst = lambda bi, i: (bi, i, 0)
        s_kv = s_q = s
    vec = pl.BlockSpec((1, d), lambda bi, i: (0, 0))
    return pl.pallas_call(
        functools.partial(_kvq_b_kernel, lead_pad=lead_pad),
        grid=grid,
        in_specs=[pl.BlockSpec((1, tm, d), src), vec, vec,
                  pl.BlockSpec((d, 2 * d), lambda bi, i: (0, 0)),
                  pl.BlockSpec((d, d), lambda bi, i: (0, 0))],
        out_specs=[pl.BlockSpec((1, tm, d), dst), pl.BlockSpec((1, tm, d), dst),
                   pl.BlockSpec((1, tm, d), src if lead_pad else dst)],
        out_shape=[jax.ShapeDtypeStruct((b, s_kv, d), kv_dtype), jax.ShapeDtypeStruct((b, s_kv, d), kv_dtype),
                   jax.ShapeDtypeStruct((b, s_q, d), BF16)],
        compiler_params=_params("parallel", "arbitrary"),
        name="kvq_b",
    )(x3, gkv, gq, wkv, wq)


def _band_bias_kernel(u_ref, o_ref):
    x = jnp.broadcast_to(u_ref[0], (BAND_TQ, 2 * B_WINDOW))
    o_ref[0] = pltpu.roll(x, 0, 1, stride=1, stride_axis=0)[:, :BAND_KW]


def _band_bias(rel_bias):
    edge = jnp.broadcast_to(rel_bias[:, 2 * REL_CLIP:], (B_HEADS, REL_CLIP))
    u = jnp.concatenate([edge, rel_bias[:, :0:-1], edge], axis=1)[:, None, :]
    assert u.shape[-1] == 2 * B_WINDOW
    return pl.pallas_call(
        _band_bias_kernel,
        grid=(B_HEADS,),
        in_specs=[pl.BlockSpec((1, 1, 2 * B_WINDOW), lambda h: (h, 0, 0))],
        out_specs=pl.BlockSpec((1, BAND_TQ, BAND_KW), lambda h: (h, 0, 0)),
        out_shape=jax.ShapeDtypeStruct((B_HEADS, BAND_TQ, BAND_KW), F32),
        compiler_params=_params("parallel"),
        name="band_bias",
    )(u)


def _pick_halves(o, t):
    lane = lax.broadcasted_iota(jnp.int32, (t, LANES), 1)
    return jnp.where(lane < 64, o[:t], o[t:])


def _band_prompt_kernel(q_ref, k_ref, v_ref, bias_ref, o_ref):
    ti = pl.program_id(2)
    t = BAND_TQ
    qs = _stack_halves(q_ref[0])
    off = pl.multiple_of(ti * t, t)
    kw = k_ref[0, pl.ds(off, BAND_KW), :]
    vw = v_ref[0, pl.ds(off, BAND_KW), :]
    s = _dot_nt(qs, kw) + bias_ref[...].reshape(2 * t, BAND_KW)
    r = lax.broadcasted_iota(jnp.int32, s.shape, 0)
    c = lax.broadcasted_iota(jnp.int32, s.shape, 1)
    qc = _chunk_of(jnp.bitwise_and(r, t - 1))
    kc = _chunk_of(c)
    ok = (kc >= qc) & (kc <= qc + B_WINDOW // CHUNK) & (c + off >= B_WINDOW)
    s = jnp.where(ok, s, NEG)
    m = jnp.max(s, axis=-1, keepdims=True)
    p = jnp.exp(s - m)
    l = jnp.sum(p, axis=-1, keepdims=True)
    o = _dot(p.astype(BF16), vw) * (1.0 / l)
    o_ref[0] = _pick_halves(o, t).astype(BF16)


def _band_prompt(q, kpad, vpad, bias):
    b, s, d = q.shape
    t = BAND_TQ
    nhp = d // LANES
    win = pl.BlockSpec((1, s + B_WINDOW, LANES), lambda hp, bi, ti: (bi, 0, hp))
    tile = pl.BlockSpec((1, t, LANES), lambda hp, bi, ti: (bi, ti, hp))
    return pl.pallas_call(
        _band_prompt_kernel,
        grid=(nhp, b, s // t),
        in_specs=[tile, win, win, pl.BlockSpec((2, t, BAND_KW), lambda hp, bi, ti: (hp, 0, 0))],
        out_specs=tile,
        out_shape=jax.ShapeDtypeStruct((b, s, d), BF16),
        compiler_params=_params("parallel", "parallel", "arbitrary"),
        name="band_attn_prompt",
    )(q, kpad, vpad, bias)


def _band_sample_kernel(q_ref, kc_ref, kn_ref, vc_ref, vn_ref, bias_ref, o_ref):
    t = q_ref.shape[1]
    wb = kc_ref.shape[1]
    qs = _stack_halves(q_ref[0])
    kc = kc_ref[0].astype(BF16)
    vc = vc_ref[0].astype(BF16)
    kn = _pad_rows(kn_ref[0].astype(BF16), LANES)
    vn = _pad_rows(vn_ref[0].astype(BF16), LANES)
    bias = bias_ref[...].reshape(2 * t, BAND_KW)
    sc = _dot_nt(qs, kc) + bias[:, :wb]
    sn = _dot_nt(qs, kn) + bias[:, wb:wb + LANES]
    col = lax.broadcasted_iota(jnp.int32, sn.shape, 1)
    sn = jnp.where(col < t, sn, NEG)
    m = jnp.maximum(jnp.max(sc, axis=-1, keepdims=True), jnp.max(sn, axis=-1, keepdims=True))
    pc = jnp.exp(sc - m)
    pn = jnp.exp(sn - m)
    l = jnp.sum(pc, axis=-1, keepdims=True) + jnp.sum(pn, axis=-1, keepdims=True)
    o = (_dot(pc.astype(BF16), vc) + _dot(pn.astype(BF16), vn)) * (1.0 / l)
    o_ref[0] = _pick_halves(o, t).astype(BF16)


def _band_sample(q, kc, kn, vc, vn, bias):
    b, t, d = q.shape
    wb = kc.shape[1]
    assert wb == B_WINDOW
    new = pl.BlockSpec((1, t, LANES), lambda bi, hp: (bi, 0, hp))
    old = pl.BlockSpec((1, wb, LANES), lambda bi, hp: (bi, 0, hp))
    return pl.pallas_call(
        _band_sample_kernel,
        grid=(b, d // LANES),
        in_specs=[new, old, new, old, new, pl.BlockSpec((2, t, BAND_KW), lambda bi, hp: (hp, 0, 0))],
        out_specs=new,
        out_shape=jax.ShapeDtypeStruct((b, t, d), BF16),
        compiler_params=_params("parallel", "parallel"),
        name="band_attn_sample",
    )(q, kc, kn, vc, vn, bias)


def kernel(x_prompt, x_sample, cache_k_a, cache_v_a, cache_k_b, cache_v_b, g_attn, g_ffn, w_q_a, w_k_a, w_v_a, lam_q1, lam_k1, lam_q2, lam_k2, g_sub_a, w_o_a, g_kv, w_kv_b, w_q_b, rel_bias_b, w_o_b, w_router_g, b_router_g, w_router_e, b_router_e, w_gate, w_up, w_down, g_final):
    bp, sp, d = x_prompt.shape
    bs, ts, _ = x_sample.shape
    past = cache_k_a.shape[2]
    wb = cache_k_b.shape[1]
    n_p, n_s = bp * sp, bs * ts
    assert d == D_MODEL and sp % ROW_TILE == 0 and ROW_TILE % ts == 0 and ts <= LANES
    depth = g_attn.shape[0]
    n_a = w_q_a.shape[0]
    assert depth == 2 and n_a == 1 and w_q_b.shape[0] == 1

    row = lambda v: v.reshape(1, -1).astype(F32)
    bf = lambda w: w.astype(BF16)

    def moe_weights(l):
        wr = jnp.concatenate([w_router_g[l], w_router_e[l].transpose(1, 0, 2).reshape(d, N_EXPERTS)], axis=1)
        wr = jnp.pad(wr, ((0, 0), (0, LANES - wr.shape[1])))
        br = jnp.concatenate([b_router_g[l], b_router_e[l].reshape(-1)])
        br = jnp.pad(br, (0, LANES - br.shape[0]))[None, :]
        wgu = bf(jnp.concatenate([w_gate[l], w_up[l]], axis=-1))
        return row(g_ffn[l]), wr.astype(F32), br.astype(F32), wgu, bf(w_down[l])

    xp = x_prompt.reshape(n_p, d)
    xs = x_sample.reshape(n_s, d)

    lam_init = 0.8 - 0.6 * math.exp(-0.3 * 0)
    lamv = jnp.stack([lam_q1[0], lam_k1[0], lam_q2[0], lam_k2[0]]).astype(F32)
    gsub = row(g_sub_a[0])
    wq, wk, wv, wo = bf(w_q_a[0]), bf(w_k_a[0]), bf(w_v_a[0]), bf(w_o_a[0])
    g0 = row(g_attn[0])

    cos_p, sin_p = _rope_table(sp, sp, 0, min(ROW_TILE, sp))
    qp, kfp, kbp, vfp, vbp = _qkv_a(x_prompt, g0, wq, wk, wv, cos_p, sin_p, head_major=True)
    ap = _diff_prompt(lamv, gsub, qp, kbp, vbp, lam_init)
    xp = _out_proj(ap.reshape(n_p, d), xp, wo)

    tm_s = min(ROW_TILE, n_s)
    cos_s, sin_s = _rope_table(tm_s, ts, past, tm_s)
    qs, kfs, _, vfs, _ = _qkv_a(xs.reshape(1, n_s, d), g0, wq, wk, wv, cos_s, sin_s, head_major=False)
    kfs3, vfs3 = kfs.reshape(bs, ts, d), vfs.reshape(bs, ts, d)
    a_s = _diff_sample(lamv, gsub, qs.reshape(bs, ts, d), cache_k_a[0].reshape(bs, past, d), bf(kfs3),
                       cache_v_a[0].reshape(bs, past, d), bf(vfs3), lam_init)
    xs = _out_proj(a_s.reshape(n_s, d), xs, wo)

    mw = moe_weights(0)
    gfin = row(g_final)
    xp = _moe(xp, *mw, gfin, final_norm=False)
    xs = _moe(xs, *mw, gfin, final_norm=False)

    gkv, g1 = row(g_kv), row(g_attn[1])
    wkv, wqb, wob = bf(w_kv_b), bf(w_q_b[0]), bf(w_o_b[0])
    bias = _band_bias(rel_bias_b[0].astype(F32))

    xp3 = xp.reshape(bp, sp, d)
    kpad, vpad, qb = _kvq_b(xp3, gkv, g1, wkv, wqb, BF16, lead_pad=True)
    kbt, vbt, _ = _kvq_b(xp3, gkv, g1, wkv, wqb, F32, lead_pad=False, row_block=sp // ROW_TILE - 1)
    ab = _band_prompt(qb, kpad, vpad, bias)
    xp = _out_proj(ab.reshape(n_p, d), xp, wob)

    kns, vns, qbs = _kvq_b(xs.reshape(1, n_s, d), gkv, g1, wkv, wqb, F32, lead_pad=False)
    kns3, vns3 = kns.reshape(bs, ts, d), vns.reshape(bs, ts, d)
    abs_ = _band_sample(qbs.reshape(bs, ts, d), cache_k_b.reshape(bs, wb, d), kns3,
                        cache_v_b.reshape(bs, wb, d), vns3, bias)
    xs = _out_proj(abs_.reshape(n_s, d), xs, wob)

    mw = moe_weights(1)
    y_prompt = _moe(xp, *mw, gfin, final_norm=True).reshape(bp, sp, d)
    y_sample = _moe(xs, *mw, gfin, final_norm=True).reshape(bs, ts, d)

    wp = min(B_WINDOW, sp)
    assert wp == ROW_TILE
    k_a_prompt = kfp.reshape(1, bp, sp, 2 * A_HEADS, A_DK)
    v_a_prompt = vfp.reshape(1, bp, sp, A_HEADS, A_DV)
    k_a_sample = kfs.reshape(1, bs, ts, 2 * A_HEADS, A_DK)
    v_a_sample = vfs.reshape(1, bs, ts, A_HEADS, A_DV)
    k_b_prompt = kbt.reshape(bp, wp, B_HEADS, B_DH)
    v_b_prompt = vbt.reshape(bp, wp, B_HEADS, B_DH)
    k_b_sample = jnp.concatenate([cache_k_b.reshape(bs, wb, d)[:, ts:], kns3], axis=1).reshape(bs, wb, B_HEADS, B_DH)
    v_b_sample = jnp.concatenate([cache_v_b.reshape(bs, wb, d)[:, ts:], vns3], axis=1).reshape(bs, wb, B_HEADS, B_DH)
    return (y_prompt, y_sample, k_a_prompt, v_a_prompt, k_b_prompt, v_b_prompt,
            k_a_sample, v_a_sample, k_b_sample, v_b_sample)
```

```python
import functools
import math

import jax
import jax.numpy as jnp
from jax import lax
from jax.experimental import pallas as pl
from jax.experimental.pallas import tpu as pltpu

F32 = jnp.float32
BF16 = jnp.bfloat16

EPS = 1e-6
CHUNK = 64
D_MODEL = 1024
A_HEADS = 8
A_DK = 64
A_DV = 128
B_HEADS = 16
B_DH = 64
B_WINDOW = 512
REL_CLIP = 256
N_GROUPS = 4
EXPERTS_PER_GROUP = 4
N_EXPERTS = 16
EXPERT_FF = 512
ROPE_THETA = 10000.0

LANES = 128
NEG = -1e30
LOG2E = math.log2(math.e)
VMEM_LIMIT = 56 * 1024 * 1024

ROW_TILE = 512
MOE_TILE = 1024
HEADS_PER_STEP = 2
ONES_ROWS = 16
BAND_TQ = 256
BAND_KW = BAND_TQ + B_WINDOW


def _params(*sem):
    return pltpu.CompilerParams(dimension_semantics=sem, vmem_limit_bytes=VMEM_LIMIT)


def _rms_scale(x):
    return x * lax.rsqrt(jnp.mean(x * x, axis=-1, keepdims=True) + EPS)


def _dot(a, b):
    return jnp.dot(a, b, preferred_element_type=F32)


def _dot_nt(a, b):
    return lax.dot_general(a, b, (((1,), (1,)), ((), ())), preferred_element_type=F32)


def _chunk_of(pos):
    return jnp.right_shift(pos, CHUNK.bit_length() - 1)


def _stack_halves(q):
    lane = lax.broadcasted_iota(jnp.int32, q.shape, 1)
    zero = jnp.zeros_like(q)
    return jnp.concatenate([jnp.where(lane < 64, q, zero), jnp.where(lane >= 64, q, zero)], axis=0)


def _rope_table_kernel(inv_ref, cos_ref, sin_ref, *, period, offset, rows):
    i = pl.program_id(0)
    row = lax.broadcasted_iota(jnp.int32, (rows, LANES), 0) + i * rows
    lane = lax.broadcasted_iota(jnp.int32, (rows, LANES), 1)
    pos = (offset + lax.rem(row, period)).astype(F32)
    ang = pos * inv_ref[...]
    s = jnp.sin(ang)
    cos_ref[...] = jnp.cos(ang)
    sin_ref[...] = jnp.where(lax.rem(lane, 64) < 32, -s, s)


def _rope_table(n_rows, period, offset, rows):
    half = A_DK // 2
    inv = jnp.power(ROPE_THETA, -jnp.arange(half, dtype=F32) / half)
    inv = jnp.tile(inv, LANES // half)[None, :]
    return pl.pallas_call(
        functools.partial(_rope_table_kernel, period=period, offset=offset, rows=rows),
        grid=(n_rows // rows,),
        in_specs=[pl.BlockSpec((1, LANES), lambda i: (0, 0))],
        out_specs=[pl.BlockSpec((rows, LANES), lambda i: (i, 0))] * 2,
        out_shape=[jax.ShapeDtypeStruct((n_rows, LANES), F32)] * 2,
        compiler_params=_params("arbitrary"),
        name="rope_table",
    )(inv)


def _qkv_a_kernel(x_ref, g_ref, wq_ref, wk_ref, wv_ref, cos_ref, sin_ref,
                  q_ref, kf_ref, kb_ref, vf_ref, vb_ref, *, transposed, q_scale):
    h = (_rms_scale(x_ref[0]) * g_ref[...]).astype(BF16)
    cos = cos_ref[...]
    sin = sin_ref[...]
    lane = lax.broadcasted_iota(jnp.int32, cos.shape, 1)
    first = lax.rem(lane, 64) < 32

    yq = _dot(h, wq_ref[...])
    yk = _dot(h, wk_ref[...])
    yv = _dot(h, wv_ref[...])
    for c in range(D_MODEL // LANES):
        sl = slice(c * LANES, (c + 1) * LANES)

        def rope(y):
            yc = y[:, sl]
            rot = jnp.where(first, pltpu.roll(yc, LANES - 32, 1), pltpu.roll(yc, 32, 1))
            return yc * cos + rot * sin

        rq = rope(yq) * q_scale
        rk = rope(yk)
        vc = yv[:, sl]
        kf_ref[0, :, sl] = rk
        vf_ref[0, :, sl] = vc
        if transposed:
            q_ref[0, c] = rq.T.astype(BF16)
            kb_ref[0, c] = rk.astype(BF16)
            vb_ref[0, c, 0, :A_DV] = vc.T.astype(BF16)
            vb_ref[0, c, 0, A_DV:] = jnp.ones((ONES_ROWS, vc.shape[0]), BF16)
        else:
            q_ref[0, :, sl] = rq.astype(BF16)
            kb_ref[0, :, sl] = rk.astype(BF16)
            vb_ref[0, :, sl] = vc.astype(BF16)


def _qkv_a(x3, g, wq, wk, wv, cos, sin, transposed, q_scale):
    b, s, d = x3.shape
    tm = min(ROW_TILE, s)
    n_tab = cos.shape[0] // tm
    nh = d // LANES
    tok = pl.BlockSpec((1, tm, d), lambda bi, i: (bi, i, 0))
    fshape = jax.ShapeDtypeStruct((b, s, d), F32)
    if transposed:
        hspecs = [pl.BlockSpec((1, nh, LANES, tm), lambda bi, i: (bi, 0, 0, i)),
                  pl.BlockSpec((1, nh, tm, LANES), lambda bi, i: (bi, 0, i, 0)),
                  pl.BlockSpec((1, nh, 1, A_DV + ONES_ROWS, tm), lambda bi, i: (bi, 0, i, 0, 0))]
        hshapes = [jax.ShapeDtypeStruct((b, nh, LANES, s), BF16),
                   jax.ShapeDtypeStruct((b, nh, s, LANES), BF16),
                   jax.ShapeDtypeStruct((b, nh, s // tm, A_DV + ONES_ROWS, tm), BF16)]
    else:
        hspecs = [tok] * 3
        hshapes = [jax.ShapeDtypeStruct((b, s, d), BF16)] * 3
    wspec = pl.BlockSpec((d, d), lambda bi, i: (0, 0))
    tab = pl.BlockSpec((tm, LANES), lambda bi, i: (i % n_tab, 0))
    return pl.pallas_call(
        functools.partial(_qkv_a_kernel, transposed=transposed, q_scale=q_scale),
        grid=(b, s // tm),
        in_specs=[tok, pl.BlockSpec((1, d), lambda bi, i: (0, 0)), wspec, wspec, wspec, tab, tab],
        out_specs=[hspecs[0], tok, hspecs[1], tok, hspecs[2]],
        out_shape=[hshapes[0], fshape, hshapes[1], fshape, hshapes[2]],
        compiler_params=_params("parallel", "parallel"),
        name="qkv_a",
    )(x3, g, wq, wk, wv, cos, sin)


def _diff_lambda(lam_ref, lam_init):
    lv = lam_ref[...]
    s1 = jnp.sum(lv[0:1] * lv[1:2], axis=-1, keepdims=True)
    s2 = jnp.sum(lv[2:3] * lv[3:4], axis=-1, keepdims=True)
    return jnp.exp(s1) - jnp.exp(s2) + lam_init


def _diff_finish(o1, o2, lam, gsub, lam_init):
    d = o1 - lam * o2
    return (_rms_scale(d) * gsub * (1.0 - lam_init)).astype(BF16)


def _diff_prompt_kernel(lam_ref, gsub_ref, qt_ref, k_ref, vt_ref, o_ref, *, t, lam_init):
    qi = pl.program_id(2)
    nh = qt_ref.shape[1]
    row = lax.broadcasted_iota(jnp.int32, (A_DV, t), 0)
    zero = jnp.zeros((A_DV, t), BF16)
    qst = [jnp.concatenate([jnp.where(row < A_DK, qt_ref[0, h], zero), jnp.where(row >= A_DK, qt_ref[0, h], zero)],
                           axis=1) for h in range(nh)]

    def step(i, carry, mask):
        out = []
        for h in range(nh):
            m, acc = carry[h]
            k = k_ref[0, h, pl.ds(pl.multiple_of(i * t, t), t), :]
            s = _dot(k, qst[h])
            if mask is not None:
                s = jnp.where(mask, s, NEG)
            m_new = jnp.maximum(m, jnp.max(s, axis=0, keepdims=True))
            p = jnp.exp2((s - m_new).astype(BF16))
            acc = jnp.exp2(m - m_new) * acc + _dot(vt_ref[0, h, i], p)
            out.append((m_new, acc))
        return tuple(out)

    init = tuple((jnp.full((1, 2 * t), NEG, F32), jnp.zeros((A_DV + ONES_ROWS, 2 * t), F32)) for _ in range(nh))
    carry = lax.fori_loop(0, qi, lambda i, c: step(i, c, None), init)
    kk = lax.broadcasted_iota(jnp.int32, (t, 2 * t), 0)
    qq = lax.broadcasted_iota(jnp.int32, (t, 2 * t), 1)
    mask = _chunk_of(kk) <= _chunk_of(jnp.bitwise_and(qq, t - 1))
    lam = _diff_lambda(lam_ref, lam_init)
    for h, (m, acc) in enumerate(step(qi, carry, mask)):
        o = (acc[:A_DV] * (1.0 / acc[A_DV:A_DV + 1])).T
        o_ref[0, :, h * A_DV:(h + 1) * A_DV] = _diff_finish(o[:t], o[t:], lam, gsub_ref[...], lam_init)


def _diff_prompt(lamv, gsub, qt, k, vt, lam_init):
    b, nh, _, s = qt.shape
    t = vt.shape[-1]
    hs = HEADS_PER_STEP
    return pl.pallas_call(
        functools.partial(_diff_prompt_kernel, t=t, lam_init=lam_init),
        grid=(b, nh // hs, s // t),
        in_specs=[pl.BlockSpec(lamv.shape, lambda bi, hi, qi: (0, 0)),
                  pl.BlockSpec((1, A_DV), lambda bi, hi, qi: (0, 0)),
                  pl.BlockSpec((1, hs, LANES, t), lambda bi, hi, qi: (bi, hi, 0, qi)),
                  pl.BlockSpec((1, hs, s, LANES), lambda bi, hi, qi: (bi, hi, 0, 0)),
                  pl.BlockSpec((1, hs, s // t, A_DV + ONES_ROWS, t), lambda bi, hi, qi: (bi, hi, 0, 0, 0))],
        out_specs=pl.BlockSpec((1, t, hs * LANES), lambda bi, hi, qi: (bi, qi, hi)),
        out_shape=jax.ShapeDtypeStruct((b, s, nh * LANES), BF16),
        compiler_params=_params("parallel", "parallel", "arbitrary"),
        name="diff_attn_prompt",
    )(lamv, gsub, qt, k, vt)


def _pad_rows(x, rows):
    return jnp.concatenate([x, jnp.zeros((rows - x.shape[0], x.shape[1]), x.dtype)], axis=0)


def _diff_sample_kernel(lam_ref, gsub_ref, q_ref, kc_ref, kn_ref, vc_ref, vn_ref, o_ref, *, lam_init):
    t = q_ref.shape[1]
    qs = _stack_halves(q_ref[0])
    kc = kc_ref[0].astype(BF16)
    vc = vc_ref[0].astype(BF16)
    kn = _pad_rows(kn_ref[0], LANES)
    vn = _pad_rows(vn_ref[0], LANES)
    sc = _dot_nt(qs, kc)
    sn = _dot_nt(qs, kn)
    col = lax.broadcasted_iota(jnp.int32, sn.shape, 1)
    sn = jnp.where(col < t, sn, NEG)
    m = jnp.maximum(jnp.max(sc, axis=-1, keepdims=True), jnp.max(sn, axis=-1, keepdims=True))
    pc = jnp.exp(sc - m)
    pn = jnp.exp(sn - m)
    l = jnp.sum(pc, axis=-1, keepdims=True) + jnp.sum(pn, axis=-1, keepdims=True)
    acc = _dot(pc.astype(BF16), vc) + _dot(pn.astype(BF16), vn)
    o = acc * (1.0 / l)
    o_ref[0] = _diff_finish(o[:t], o[t:], _diff_lambda(lam_ref, lam_init), gsub_ref[...], lam_init)


def _diff_sample(lamv, gsub, q, kc, kn, vc, vn, lam_init):
    b, t, d = q.shape
    past = kc.shape[1]
    new = pl.BlockSpec((1, t, LANES), lambda bi, hi: (bi, 0, hi))
    old = pl.BlockSpec((1, past, LANES), lambda bi, hi: (bi, 0, hi))
    return pl.pallas_call(
        functools.partial(_diff_sample_kernel, lam_init=lam_init),
        grid=(b, d // LANES),
        in_specs=[pl.BlockSpec(lamv.shape, lambda bi, hi: (0, 0)),
                  pl.BlockSpec((1, A_DV), lambda bi, hi: (0, 0)), new, old, new, old, new],
        out_specs=new,
        out_shape=jax.ShapeDtypeStruct((b, t, d), BF16),
        compiler_params=_params("parallel", "parallel"),
        name="diff_attn_sample",
    )(lamv, gsub, q, kc, kn, vc, vn)


def _out_proj_kernel(a_ref, x_ref, w_ref, o_ref):
    o_ref[...] = x_ref[...] + _dot(a_ref[...], w_ref[...])


def _out_proj(a, x, w):
    n, d = x.shape
    tm = min(ROW_TILE, n)
    tok = pl.BlockSpec((tm, d), lambda i: (i, 0))
    return pl.pallas_call(
        _out_proj_kernel,
        grid=(n // tm,),
        in_specs=[tok, tok, pl.BlockSpec((d, d), lambda i: (0, 0))],
        out_specs=tok,
        out_shape=jax.ShapeDtypeStruct((n, d), F32),
        compiler_params=_params("parallel"),
        name="out_proj",
    )(a, x, w)


def _moe_gates(hf, wr_ref, br_ref):
    lg = jnp.dot(hf, wr_ref[...], preferred_element_type=F32, precision=lax.Precision.HIGHEST) + br_ref[...]
    lane = lax.broadcasted_iota(jnp.int32, lg.shape, 1).astype(F32)
    big = float(LANES)

    def first_argmax(v):
        mx = jnp.max(v, axis=-1, keepdims=True)
        return mx, jnp.min(jnp.where(v == mx, lane, big), axis=-1, keepdims=True)

    is_g = lane < N_GROUPS
    gmax, gidx = first_argmax(jnp.where(is_g, lg, NEG))
    gval = 1.0 / jnp.sum(jnp.where(is_g, jnp.exp(lg - gmax), 0.0), axis=-1, keepdims=True)
    lo = N_GROUPS + EXPERTS_PER_GROUP * gidx
    el = jnp.where((lane >= lo) & (lane < lo + EXPERTS_PER_GROUP), lg, NEG)
    v1, i1 = first_argmax(el)
    v2, i2 = first_argmax(jnp.where(lane == i1, NEG, el))
    e2 = jnp.exp(v2 - v1)
    inv = gval / (1.0 + e2)
    return jnp.where(lane == i1, inv, 0.0) + jnp.where(lane == i2, inv * e2, 0.0)


def _moe_kernel(x_ref, g_ref, wr_ref, br_ref, wgu_ref, wd_ref, gf_ref, o_ref, h_ref, gate_ref, *, final_norm):
    e = pl.program_id(1)

    @pl.when(e == 0)
    def _():
        x = x_ref[...]
        hf = _rms_scale(x) * g_ref[...]
        h_ref[...] = hf.astype(BF16)
        gate_ref[...] = _moe_gates(hf, wr_ref, br_ref)
        o_ref[...] = x

    gu = _dot(h_ref[...], wgu_ref[0])
    a = gu[:, :EXPERT_FF]
    he = (a * (1.0 / (1.0 + jnp.exp(-a))) * gu[:, EXPERT_FF:]).astype(BF16)
    y = _dot(he, wd_ref[0])
    gates = gate_ref[...]
    lane = lax.broadcasted_iota(jnp.int32, gates.shape, 1)
    ge = jnp.sum(jnp.where(lane == e + N_GROUPS, gates, 0.0), axis=-1, keepdims=True)
    o_ref[...] += ge * y

    if final_norm:
        @pl.when(e == N_EXPERTS - 1)
        def _():
            o_ref[...] = _rms_scale(o_ref[...]) * gf_ref[...]


def _moe(x, g, wr, br, wgu, wd, gf, final_norm):
    n, d = x.shape
    tm = min(MOE_TILE, n)
    tok = pl.BlockSpec((tm, d), lambda i, e: (i, 0))
    vec = pl.BlockSpec((1, d), lambda i, e: (0, 0))
    return pl.pallas_call(
        functools.partial(_moe_kernel, final_norm=final_norm),
        grid=(n // tm, N_EXPERTS),
        in_specs=[tok, vec,
                  pl.BlockSpec((d, LANES), lambda i, e: (0, 0)),
                  pl.BlockSpec((1, LANES), lambda i, e: (0, 0)),
                  pl.BlockSpec((1, d, 2 * EXPERT_FF), lambda i, e: (e, 0, 0)),
                  pl.BlockSpec((1, EXPERT_FF, d), lambda i, e: (e, 0, 0)),
                  vec],
        out_specs=tok,
        out_shape=jax.ShapeDtypeStruct((n, d), F32),
        scratch_shapes=[pltpu.VMEM((tm, d), BF16), pltpu.VMEM((tm, LANES), F32)],
        compiler_params=_params("parallel", "arbitrary"),
        name="hier_moe",
    )(x, g, wr, br, wgu, wd, gf)


def _kvq_b_kernel(x_ref, gkv_ref, gq_ref, wkv_ref, wq_ref, k_ref, v_ref, q_ref, *, lead_pad):
    def compute():
        xh = _rms_scale(x_ref[0])
        kv = _dot((xh * gkv_ref[...]).astype(BF16), wkv_ref[...])
        k_ref[0] = kv[:, :D_MODEL].astype(k_ref.dtype)
        v_ref[0] = kv[:, D_MODEL:].astype(v_ref.dtype)
        q = _dot((xh * gq_ref[...]).astype(BF16), wq_ref[...])
        q_ref[0] = (q * (B_DH ** -0.5 * LOG2E)).astype(BF16)

    if lead_pad:
        i = pl.program_id(1)

        @pl.when(i == 0)
        def _():
            k_ref[...] = jnp.zeros_like(k_ref)
            v_ref[...] = jnp.zeros_like(v_ref)

        pl.when(i > 0)(compute)
    else:
        compute()


def _kvq_b(x3, gkv, gq, wkv, wq, kv_dtype, lead_pad, row_block=None):
    b, s, d = x3.shape
    tm = min(ROW_TILE, s)
    if lead_pad:
        assert tm == B_WINDOW
        grid = (b, s // tm + 1)
        src = lambda bi, i: (bi, jnp.maximum(i - 1, 0), 0)
        dst = lambda bi, i: (bi, i, 0)
        s_kv, s_q = s + B_WINDOW, s
    elif row_block is not None:
        grid = (b, 1)
        src = lambda bi, i: (bi, row_block, 0)
        dst = lambda bi, i: (bi, 0, 0)
        s_kv = s_q = tm
    else:
        grid = (b, s // tm)
        src = dst = lambda bi, i: (bi, i, 0)
        s_kv = s_q = s
    vec = pl.BlockSpec((1, d), lambda bi, i: (0, 0))
    return pl.pallas_call(
        functools.partial(_kvq_b_kernel, lead_pad=lead_pad),
        grid=grid,
        in_specs=[pl.BlockSpec((1, tm, d), src), vec, vec,
                  pl.BlockSpec((d, 2 * d), lambda bi, i: (0, 0)),
                  pl.BlockSpec((d, d), lambda bi, i: (0, 0))],
        out_specs=[pl.BlockSpec((1, tm, d), dst), pl.BlockSpec((1, tm, d), dst),
                   pl.BlockSpec((1, tm, d), src if lead_pad else dst)],
        out_shape=[jax.ShapeDtypeStruct((b, s_kv, d), kv_dtype), jax.ShapeDtypeStruct((b, s_kv, d), kv_dtype),
                   jax.ShapeDtypeStruct((b, s_q, d), BF16)],
        compiler_params=_params("parallel", "arbitrary"),
        name="kvq_b",
    )(x3, gkv, gq, wkv, wq)


def _band_bias_kernel(u_ref, o_ref):
    x = jnp.broadcast_to(u_ref[0], (BAND_TQ, 2 * B_WINDOW))
    bias = pltpu.roll(x, 0, 1, stride=1, stride_axis=0)[:, :BAND_KW]
    qc = _chunk_of(lax.broadcasted_iota(jnp.int32, bias.shape, 0))
    kc = _chunk_of(lax.broadcasted_iota(jnp.int32, bias.shape, 1))
    ok = (kc >= qc) & (kc <= qc + B_WINDOW // CHUNK)
    o_ref[0] = jnp.where(ok, bias * LOG2E, NEG)


def _band_bias(rel_bias):
    edge = jnp.broadcast_to(rel_bias[:, 2 * REL_CLIP:], (B_HEADS, REL_CLIP))
    u = jnp.concatenate([edge, rel_bias[:, :0:-1], edge], axis=1)[:, None, :]
    assert u.shape[-1] == 2 * B_WINDOW
    return pl.pallas_call(
        _band_bias_kernel,
        grid=(B_HEADS,),
        in_specs=[pl.BlockSpec((1, 1, 2 * B_WINDOW), lambda h: (h, 0, 0))],
        out_specs=pl.BlockSpec((1, BAND_TQ, BAND_KW), lambda h: (h, 0, 0)),
        out_shape=jax.ShapeDtypeStruct((B_HEADS, BAND_TQ, BAND_KW), F32),
        compiler_params=_params("parallel"),
        name="band_bias",
    )(u)


def _pick_halves(o, t):
    lane = lax.broadcasted_iota(jnp.int32, (t, LANES), 1)
    return jnp.where(lane < 64, o[:t], o[t:])


def _band_prompt_kernel(q_ref, k_ref, v_ref, bias_ref, o_ref):
    ti = pl.program_id(2)
    t = BAND_TQ
    qs = _stack_halves(q_ref[0])
    off = pl.multiple_of(ti * t, t)
    kw = k_ref[0, pl.ds(off, BAND_KW), :]
    vw = v_ref[0, pl.ds(off, BAND_KW), :]
    s = _dot_nt(qs, kw) + bias_ref[...].reshape(2 * t, BAND_KW)
    vx = jnp.concatenate([vw, jnp.ones((BAND_KW, LANES), BF16)], axis=1)

    def finish(s):
        m = jnp.max(s, axis=-1, keepdims=True)
        o = _dot(jnp.exp2((s - m).astype(BF16)), vx)
        o = o[:, :LANES] * (1.0 / o[:, LANES:])
        o_ref[0] = _pick_halves(o, t).astype(BF16)

    @pl.when(off < B_WINDOW)
    def _():
        c = lax.broadcasted_iota(jnp.int32, s.shape, 1)
        finish(jnp.where(c + off >= B_WINDOW, s, NEG))

    @pl.when(off >= B_WINDOW)
    def _():
        finish(s)


def _band_prompt(q, kpad, vpad, bias):
    b, s, d = q.shape
    t = BAND_TQ
    nhp = d // LANES
    win = pl.BlockSpec((1, s + B_WINDOW, LANES), lambda hp, bi, ti: (bi, 0, hp))
    tile = pl.BlockSpec((1, t, LANES), lambda hp, bi, ti: (bi, ti, hp))
    return pl.pallas_call(
        _band_prompt_kernel,
        grid=(nhp, b, s // t),
        in_specs=[tile, win, win, pl.BlockSpec((2, t, BAND_KW), lambda hp, bi, ti: (hp, 0, 0))],
        out_specs=tile,
        out_shape=jax.ShapeDtypeStruct((b, s, d), BF16),
        compiler_params=_params("parallel", "parallel", "arbitrary"),
        name="band_attn_prompt",
    )(q, kpad, vpad, bias)


def _band_sample_kernel(q_ref, kc_ref, kn_ref, vc_ref, vn_ref, bias_ref, o_ref):
    t = q_ref.shape[1]
    wb = kc_ref.shape[1]
    qs = _stack_halves(q_ref[0])
    kc = kc_ref[0].astype(BF16)
    vc = vc_ref[0].astype(BF16)
    kn = _pad_rows(kn_ref[0].astype(BF16), LANES)
    vn = _pad_rows(vn_ref[0].astype(BF16), LANES)
    bias = bias_ref[...].reshape(2 * t, BAND_KW)
    sc = _dot_nt(qs, kc) + bias[:, :wb]
    sn = _dot_nt(qs, kn) + bias[:, wb:wb + LANES]
    col = lax.broadcasted_iota(jnp.int32, sn.shape, 1)
    sn = jnp.where(col < t, sn, NEG)
    m = jnp.maximum(jnp.max(sc, axis=-1, keepdims=True), jnp.max(sn, axis=-1, keepdims=True))
    pc = jnp.exp2(sc - m)
    pn = jnp.exp2(sn - m)
    l = jnp.sum(pc, axis=-1, keepdims=True) + jnp.sum(pn, axis=-1, keepdims=True)
    o = (_dot(pc.astype(BF16), vc) + _dot(pn.astype(BF16), vn)) * (1.0 / l)
    o_ref[0] = _pick_halves(o, t).astype(BF16)


def _band_sample(q, kc, kn, vc, vn, bias):
    b, t, d = q.shape
    wb = kc.shape[1]
    assert wb == B_WINDOW
    new = pl.BlockSpec((1, t, LANES), lambda bi, hp: (bi, 0, hp))
    old = pl.BlockSpec((1, wb, LANES), lambda bi, hp: (bi, 0, hp))
    return pl.pallas_call(
        _band_sample_kernel,
        grid=(b, d // LANES),
        in_specs=[new, old, new, old, new, pl.BlockSpec((2, t, BAND_KW), lambda bi, hp: (hp, 0, 0))],
        out_specs=new,
        out_shape=jax.ShapeDtypeStruct((b, t, d), BF16),
        compiler_params=_params("parallel", "parallel"),
        name="band_attn_sample",
    )(q, kc, kn, vc, vn, bias)


def kernel(x_prompt, x_sample, cache_k_a, cache_v_a, cache_k_b, cache_v_b, g_attn, g_ffn, w_q_a, w_k_a, w_v_a, lam_q1, lam_k1, lam_q2, lam_k2, g_sub_a, w_o_a, g_kv, w_kv_b, w_q_b, rel_bias_b, w_o_b, w_router_g, b_router_g, w_router_e, b_router_e, w_gate, w_up, w_down, g_final):
    bp, sp, d = x_prompt.shape
    bs, ts, _ = x_sample.shape
    past = cache_k_a.shape[2]
    wb = cache_k_b.shape[1]
    n_p, n_s = bp * sp, bs * ts
    assert d == D_MODEL and sp % ROW_TILE == 0 and ROW_TILE % ts == 0 and ts <= LANES
    depth = g_attn.shape[0]
    n_a = w_q_a.shape[0]
    assert depth == 2 and n_a == 1 and w_q_b.shape[0] == 1

    row = lambda v: v.reshape(1, -1).astype(F32)
    bf = lambda w: w.astype(BF16)

    def moe_weights(l):
        wr = jnp.concatenate([w_router_g[l], w_router_e[l].transpose(1, 0, 2).reshape(d, N_EXPERTS)], axis=1)
        wr = jnp.pad(wr, ((0, 0), (0, LANES - wr.shape[1])))
        br = jnp.concatenate([b_router_g[l], b_router_e[l].reshape(-1)])
        br = jnp.pad(br, (0, LANES - br.shape[0]))[None, :]
        wgu = bf(jnp.concatenate([w_gate[l], w_up[l]], axis=-1))
        return row(g_ffn[l]), wr.astype(F32), br.astype(F32), wgu, bf(w_down[l])

    xp = x_prompt.reshape(n_p, d)
    xs = x_sample.reshape(n_s, d)

    lam_init = 0.8 - 0.6 * math.exp(-0.3 * 0)
    lamv = jnp.stack([lam_q1[0], lam_k1[0], lam_q2[0], lam_k2[0]]).astype(F32)
    gsub = row(g_sub_a[0])
    wq, wk, wv, wo = bf(w_q_a[0]), bf(w_k_a[0]), bf(w_v_a[0]), bf(w_o_a[0])
    g0 = row(g_attn[0])

    cos_p, sin_p = _rope_table(sp, sp, 0, min(ROW_TILE, sp))
    qp, kfp, kbp, vfp, vbp = _qkv_a(x_prompt, g0, wq, wk, wv, cos_p, sin_p, True, A_DK ** -0.5 * LOG2E)
    ap = _diff_prompt(lamv, gsub, qp, kbp, vbp, lam_init)
    xp = _out_proj(ap.reshape(n_p, d), xp, wo)

    tm_s = min(ROW_TILE, n_s)
    cos_s, sin_s = _rope_table(tm_s, ts, past, tm_s)
    qs, kfs, _, vfs, _ = _qkv_a(xs.reshape(1, n_s, d), g0, wq, wk, wv, cos_s, sin_s, False, A_DK ** -0.5)
    kfs3, vfs3 = kfs.reshape(bs, ts, d), vfs.reshape(bs, ts, d)
    a_s = _diff_sample(lamv, gsub, qs.reshape(bs, ts, d), cache_k_a[0].reshape(bs, past, d), bf(kfs3),
                       cache_v_a[0].reshape(bs, past, d), bf(vfs3), lam_init)
    xs = _out_proj(a_s.reshape(n_s, d), xs, wo)

    mw = moe_weights(0)
    gfin = row(g_final)
    xp = _moe(xp, *mw, gfin, final_norm=False)
    xs = _moe(xs, *mw, gfin, final_norm=False)

    gkv, g1 = row(g_kv), row(g_attn[1])
    wkv, wqb, wob = bf(w_kv_b), bf(w_q_b[0]), bf(w_o_b[0])
    bias = _band_bias(rel_bias_b[0].astype(F32))

    xp3 = xp.reshape(bp, sp, d)
    kpad, vpad, qb = _kvq_b(xp3, gkv, g1, wkv, wqb, BF16, lead_pad=True)
    kbt, vbt, _ = _kvq_b(xp3, gkv, g1, wkv, wqb, F32, lead_pad=False, row_block=sp // ROW_TILE - 1)
    ab = _band_prompt(qb, kpad, vpad, bias)
    xp = _out_proj(ab.reshape(n_p, d), xp, wob)

    kns, vns, qbs = _kvq_b(xs.reshape(1, n_s, d), gkv, g1, wkv, wqb, F32, lead_pad=False)
    kns3, vns3 = kns.reshape(bs, ts, d), vns.reshape(bs, ts, d)
    abs_ = _band_sample(qbs.reshape(bs, ts, d), cache_k_b.reshape(bs, wb, d), kns3,
                        cache_v_b.reshape(bs, wb, d), vns3, bias)
    xs = _out_proj(abs_.reshape(n_s, d), xs, wob)

    mw = moe_weights(1)
    y_prompt = _moe(xp, *mw, gfin, final_norm=True).reshape(bp, sp, d)
    y_sample = _moe(xs, *mw, gfin, final_norm=True).reshape(bs, ts, d)

    wp = min(B_WINDOW, sp)
    assert wp == ROW_TILE
    k_a_prompt = kfp.reshape(1, bp, sp, 2 * A_HEADS, A_DK)
    v_a_prompt = vfp.reshape(1, bp, sp, A_HEADS, A_DV)
    k_a_sample = kfs.reshape(1, bs, ts, 2 * A_HEADS, A_DK)
    v_a_sample = vfs.reshape(1, bs, ts, A_HEADS, A_DV)
    k_b_prompt = kbt.reshape(bp, wp, B_HEADS, B_DH)
    v_b_prompt = vbt.reshape(bp, wp, B_HEADS, B_DH)
    k_b_sample = jnp.concatenate([cache_k_b.reshape(bs, wb, d)[:, ts:], kns3], axis=1).reshape(bs, wb, B_HEADS, B_DH)
    v_b_sample = jnp.concatenate([cache_v_b.reshape(bs, wb, d)[:, ts:], vns3], axis=1).reshape(bs, wb, B_HEADS, B_DH)
    return (y_prompt, y_sample, k_a_prompt, v_a_prompt, k_b_prompt, v_b_prompt,
            k_a_sample, v_a_sample, k_b_sample, v_b_sample)
```

```python
import functools
import math

import jax
import jax.numpy as jnp
from jax import lax
from jax.experimental import pallas as pl
from jax.experimental.pallas import tpu as pltpu

F32 = jnp.float32
BF16 = jnp.bfloat16

EPS = 1e-6
CHUNK = 64
D_MODEL = 1024
A_HEADS = 8
A_DK = 64
A_DV = 128
B_HEADS = 16
B_DH = 64
B_WINDOW = 512
REL_CLIP = 256
N_GROUPS = 4
EXPERTS_PER_GROUP = 4
N_EXPERTS = 16
EXPERT_FF = 512
ROPE_THETA = 10000.0

LANES = 128
NEG = -1e30
LOG2E = math.log2(math.e)
VMEM_LIMIT = 56 * 1024 * 1024

ROW_TILE = 512
MOE_TILE = 1024
MOE_CHUNK = 128
HEADS_PER_STEP = 2
ONES_ROWS = 16
BAND_TQ = 256
BAND_KW = BAND_TQ + B_WINDOW


def _params(*sem):
    return pltpu.CompilerParams(dimension_semantics=sem, vmem_limit_bytes=VMEM_LIMIT)


def _rms_scale(x):
    return x * lax.rsqrt(jnp.mean(x * x, axis=-1, keepdims=True) + EPS)


def _dot(a, b):
    return jnp.dot(a, b, preferred_element_type=F32)


def _dot_nt(a, b):
    return lax.dot_general(a, b, (((1,), (1,)), ((), ())), preferred_element_type=F32)


def _chunk_of(pos):
    return jnp.right_shift(pos, CHUNK.bit_length() - 1)


def _stack_halves(q):
    lane = lax.broadcasted_iota(jnp.int32, q.shape, 1)
    zero = jnp.zeros_like(q)
    return jnp.concatenate([jnp.where(lane < 64, q, zero), jnp.where(lane >= 64, q, zero)], axis=0)


def _rope_table_kernel(inv_ref, cos_ref, sin_ref, *, period, offset, rows):
    i = pl.program_id(0)
    row = lax.broadcasted_iota(jnp.int32, (rows, LANES), 0) + i * rows
    lane = lax.broadcasted_iota(jnp.int32, (rows, LANES), 1)
    pos = (offset + lax.rem(row, period)).astype(F32)
    ang = pos * inv_ref[...]
    s = jnp.sin(ang)
    cos_ref[...] = jnp.cos(ang)
    sin_ref[...] = jnp.where(lax.rem(lane, 64) < 32, -s, s)


def _rope_table(n_rows, period, offset, rows):
    half = A_DK // 2
    inv = jnp.power(ROPE_THETA, -jnp.arange(half, dtype=F32) / half)
    inv = jnp.tile(inv, LANES // half)[None, :]
    return pl.pallas_call(
        functools.partial(_rope_table_kernel, period=period, offset=offset, rows=rows),
        grid=(n_rows // rows,),
        in_specs=[pl.BlockSpec((1, LANES), lambda i: (0, 0))],
        out_specs=[pl.BlockSpec((rows, LANES), lambda i: (i, 0))] * 2,
        out_shape=[jax.ShapeDtypeStruct((n_rows, LANES), F32)] * 2,
        compiler_params=_params("arbitrary"),
        name="rope_table",
    )(inv)


def _qkv_a_kernel(x_ref, g_ref, wq_ref, wk_ref, wv_ref, cos_ref, sin_ref,
                  q_ref, kf_ref, kb_ref, vf_ref, vb_ref, *, transposed, q_scale):
    h = (_rms_scale(x_ref[0]) * g_ref[...]).astype(BF16)
    cos = cos_ref[...]
    sin = sin_ref[...]
    lane = lax.broadcasted_iota(jnp.int32, cos.shape, 1)
    first = lax.rem(lane, 64) < 32

    yq = _dot(h, wq_ref[...])
    yk = _dot(h, wk_ref[...])
    yv = _dot(h, wv_ref[...])
    for c in range(D_MODEL // LANES):
        sl = slice(c * LANES, (c + 1) * LANES)

        def rope(y):
            yc = y[:, sl]
            rot = jnp.where(first, pltpu.roll(yc, LANES - 32, 1), pltpu.roll(yc, 32, 1))
            return yc * cos + rot * sin

        rq = rope(yq) * q_scale
        rk = rope(yk)
        vc = yv[:, sl]
        kf_ref[0, :, sl] = rk
        vf_ref[0, :, sl] = vc
        if transposed:
            q_ref[0, c] = rq.T.astype(BF16)
            kb_ref[0, c] = rk.astype(BF16)
            vb_ref[0, c, 0, :A_DV] = vc.T.astype(BF16)
            vb_ref[0, c, 0, A_DV:] = jnp.ones((ONES_ROWS, vc.shape[0]), BF16)
        else:
            q_ref[0, :, sl] = rq.astype(BF16)
            kb_ref[0, :, sl] = rk.astype(BF16)
            vb_ref[0, :, sl] = vc.astype(BF16)


def _qkv_a(x3, g, wq, wk, wv, cos, sin, transposed, q_scale):
    b, s, d = x3.shape
    tm = min(ROW_TILE, s)
    n_tab = cos.shape[0] // tm
    nh = d // LANES
    tok = pl.BlockSpec((1, tm, d), lambda bi, i: (bi, i, 0))
    fshape = jax.ShapeDtypeStruct((b, s, d), F32)
    if transposed:
        hspecs = [pl.BlockSpec((1, nh, LANES, tm), lambda bi, i: (bi, 0, 0, i)),
                  pl.BlockSpec((1, nh, tm, LANES), lambda bi, i: (bi, 0, i, 0)),
                  pl.BlockSpec((1, nh, 1, A_DV + ONES_ROWS, tm), lambda bi, i: (bi, 0, i, 0, 0))]
        hshapes = [jax.ShapeDtypeStruct((b, nh, LANES, s), BF16),
                   jax.ShapeDtypeStruct((b, nh, s, LANES), BF16),
                   jax.ShapeDtypeStruct((b, nh, s // tm, A_DV + ONES_ROWS, tm), BF16)]
    else:
        hspecs = [tok] * 3
        hshapes = [jax.ShapeDtypeStruct((b, s, d), BF16)] * 3
    wspec = pl.BlockSpec((d, d), lambda bi, i: (0, 0))
    tab = pl.BlockSpec((tm, LANES), lambda bi, i: (i % n_tab, 0))
    return pl.pallas_call(
        functools.partial(_qkv_a_kernel, transposed=transposed, q_scale=q_scale),
        grid=(b, s // tm),
        in_specs=[tok, pl.BlockSpec((1, d), lambda bi, i: (0, 0)), wspec, wspec, wspec, tab, tab],
        out_specs=[hspecs[0], tok, hspecs[1], tok, hspecs[2]],
        out_shape=[hshapes[0], fshape, hshapes[1], fshape, hshapes[2]],
        compiler_params=_params("parallel", "parallel"),
        name="qkv_a",
    )(x3, g, wq, wk, wv, cos, sin)


def _diff_lambda(lam_ref, lam_init):
    lv = lam_ref[...]
    s1 = jnp.sum(lv[0:1] * lv[1:2], axis=-1, keepdims=True)
    s2 = jnp.sum(lv[2:3] * lv[3:4], axis=-1, keepdims=True)
    return jnp.exp(s1) - jnp.exp(s2) + lam_init


def _diff_finish(o1, o2, lam, gsub, lam_init):
    d = o1 - lam * o2
    return (_rms_scale(d) * gsub * (1.0 - lam_init)).astype(BF16)


def _diff_prompt_kernel(lam_ref, gsub_ref, qt_ref, k_ref, vt_ref, o_ref, *, t, lam_init):
    qi = pl.program_id(2)
    nh = qt_ref.shape[1]
    row = lax.broadcasted_iota(jnp.int32, (A_DV, t), 0)
    zero = jnp.zeros((A_DV, t), BF16)
    qst = [jnp.concatenate([jnp.where(row < A_DK, qt_ref[0, h], zero), jnp.where(row >= A_DK, qt_ref[0, h], zero)],
                           axis=1) for h in range(nh)]

    def step(i, carry, mask):
        out = []
        for h in range(nh):
            m, acc = carry[h]
            k = k_ref[0, h, pl.ds(pl.multiple_of(i * t, t), t), :]
            s = _dot(k, qst[h])
            if mask is not None:
                s = jnp.where(mask, s, NEG)
            m_new = jnp.maximum(m, jnp.max(s, axis=0, keepdims=True))
            p = jnp.exp2((s - m_new).astype(BF16))
            acc = jnp.exp2(m - m_new) * acc + _dot(vt_ref[0, h, i], p)
            out.append((m_new, acc))
        return tuple(out)

    init = tuple((jnp.full((1, 2 * t), NEG, F32), jnp.zeros((A_DV + ONES_ROWS, 2 * t), F32)) for _ in range(nh))
    carry = lax.fori_loop(0, qi, lambda i, c: step(i, c, None), init)
    kk = lax.broadcasted_iota(jnp.int32, (t, 2 * t), 0)
    qq = lax.broadcasted_iota(jnp.int32, (t, 2 * t), 1)
    mask = _chunk_of(kk) <= _chunk_of(jnp.bitwise_and(qq, t - 1))
    lam = _diff_lambda(lam_ref, lam_init)
    for h, (m, acc) in enumerate(step(qi, carry, mask)):
        o = (acc[:A_DV] * (1.0 / acc[A_DV:A_DV + 1])).T
        o_ref[0, :, h * A_DV:(h + 1) * A_DV] = _diff_finish(o[:t], o[t:], lam, gsub_ref[...], lam_init)


def _diff_prompt(lamv, gsub, qt, k, vt, lam_init):
    b, nh, _, s = qt.shape
    t = vt.shape[-1]
    hs = HEADS_PER_STEP
    return pl.pallas_call(
        functools.partial(_diff_prompt_kernel, t=t, lam_init=lam_init),
        grid=(b, nh // hs, s // t),
        in_specs=[pl.BlockSpec(lamv.shape, lambda bi, hi, qi: (0, 0)),
                  pl.BlockSpec((1, A_DV), lambda bi, hi, qi: (0, 0)),
                  pl.BlockSpec((1, hs, LANES, t), lambda bi, hi, qi: (bi, hi, 0, qi)),
                  pl.BlockSpec((1, hs, s, LANES), lambda bi, hi, qi: (bi, hi, 0, 0)),
                  pl.BlockSpec((1, hs, s // t, A_DV + ONES_ROWS, t), lambda bi, hi, qi: (bi, hi, 0, 0, 0))],
        out_specs=pl.BlockSpec((1, t, hs * LANES), lambda bi, hi, qi: (bi, qi, hi)),
        out_shape=jax.ShapeDtypeStruct((b, s, nh * LANES), BF16),
        compiler_params=_params("parallel", "parallel", "arbitrary"),
        name="diff_attn_prompt",
    )(lamv, gsub, qt, k, vt)


def _pad_rows(x, rows):
    return jnp.concatenate([x, jnp.zeros((rows - x.shape[0], x.shape[1]), x.dtype)], axis=0)


def _diff_sample_kernel(lam_ref, gsub_ref, q_ref, kc_ref, kn_ref, vc_ref, vn_ref, o_ref, *, lam_init):
    t = q_ref.shape[1]
    qs = _stack_halves(q_ref[0])
    kc = kc_ref[0].astype(BF16)
    vc = vc_ref[0].astype(BF16)
    kn = _pad_rows(kn_ref[0], LANES)
    vn = _pad_rows(vn_ref[0], LANES)
    sc = _dot_nt(qs, kc)
    sn = _dot_nt(qs, kn)
    col = lax.broadcasted_iota(jnp.int32, sn.shape, 1)
    sn = jnp.where(col < t, sn, NEG)
    m = jnp.maximum(jnp.max(sc, axis=-1, keepdims=True), jnp.max(sn, axis=-1, keepdims=True))
    pc = jnp.exp(sc - m)
    pn = jnp.exp(sn - m)
    l = jnp.sum(pc, axis=-1, keepdims=True) + jnp.sum(pn, axis=-1, keepdims=True)
    acc = _dot(pc.astype(BF16), vc) + _dot(pn.astype(BF16), vn)
    o = acc * (1.0 / l)
    o_ref[0] = _diff_finish(o[:t], o[t:], _diff_lambda(lam_ref, lam_init), gsub_ref[...], lam_init)


def _diff_sample(lamv, gsub, q, kc, kn, vc, vn, lam_init):
    b, t, d = q.shape
    past = kc.shape[1]
    new = pl.BlockSpec((1, t, LANES), lambda bi, hi: (bi, 0, hi))
    old = pl.BlockSpec((1, past, LANES), lambda bi, hi: (bi, 0, hi))
    return pl.pallas_call(
        functools.partial(_diff_sample_kernel, lam_init=lam_init),
        grid=(b, d // LANES),
        in_specs=[pl.BlockSpec(lamv.shape, lambda bi, hi: (0, 0)),
                  pl.BlockSpec((1, A_DV), lambda bi, hi: (0, 0)), new, old, new, old, new],
        out_specs=new,
        out_shape=jax.ShapeDtypeStruct((b, t, d), BF16),
        compiler_params=_params("parallel", "parallel"),
        name="diff_attn_sample",
    )(lamv, gsub, q, kc, kn, vc, vn)


def _out_proj_kernel(a_ref, x_ref, w_ref, o_ref):
    o_ref[...] = x_ref[...] + _dot(a_ref[...], w_ref[...])


def _out_proj(a, x, w):
    n, d = x.shape
    tm = min(ROW_TILE, n)
    tok = pl.BlockSpec((tm, d), lambda i: (i, 0))
    return pl.pallas_call(
        _out_proj_kernel,
        grid=(n // tm,),
        in_specs=[tok, tok, pl.BlockSpec((d, d), lambda i: (0, 0))],
        out_specs=tok,
        out_shape=jax.ShapeDtypeStruct((n, d), F32),
        compiler_params=_params("parallel"),
        name="out_proj",
    )(a, x, w)


def _moe_gates(hf, wr_ref, br_ref):
    lg = jnp.dot(hf, wr_ref[...], preferred_element_type=F32, precision=lax.Precision.HIGHEST) + br_ref[...]
    lane = lax.broadcasted_iota(jnp.int32, lg.shape, 1).astype(F32)
    big = float(LANES)

    def first_argmax(v):
        mx = jnp.max(v, axis=-1, keepdims=True)
        return mx, jnp.min(jnp.where(v == mx, lane, big), axis=-1, keepdims=True)

    is_g = lane < N_GROUPS
    gmax, gidx = first_argmax(jnp.where(is_g, lg, NEG))
    gval = 1.0 / jnp.sum(jnp.where(is_g, jnp.exp(lg - gmax), 0.0), axis=-1, keepdims=True)
    lo = N_GROUPS + EXPERTS_PER_GROUP * gidx
    el = jnp.where((lane >= lo) & (lane < lo + EXPERTS_PER_GROUP), lg, NEG)
    v1, i1 = first_argmax(el)
    v2, i2 = first_argmax(jnp.where(lane == i1, NEG, el))
    e2 = jnp.exp(v2 - v1)
    inv = gval / (1.0 + e2)
    gates = jnp.where(lane == i1 - lo, inv, 0.0) + jnp.where(lane == i2 - lo, inv * e2, 0.0)
    return jnp.where(lane == gidx, 1.0, 0.0), gates


def _moe_kernel(x_ref, g_ref, wr_ref, br_ref, tri_ref, wgu_ref, wd_ref, gf_ref, o_ref,
                h_ref, gsplit_ref, rank_ref, rank_t_ref, pos_ref, y_ref, base_ref, *, final_norm):
    grp = pl.program_id(1)
    nb = x_ref.shape[0]
    ff = EXPERTS_PER_GROUP * EXPERT_FF

    @pl.when(grp == 0)
    def _():
        hf = _rms_scale(x_ref[...]) * g_ref[...]
        h_ref[...] = hf.astype(BF16)
        member, gates = _moe_gates(hf, wr_ref, br_ref)
        rest = gates
        split = jnp.zeros_like(gates)
        for i in range(3):
            part = rest.astype(BF16).astype(F32)
            split = split + pltpu.roll(part, EXPERTS_PER_GROUP * i, 1)
            rest = rest - part
        gsplit_ref[...] = split.astype(BF16)
        rank = jnp.where(member > 0.0, _dot(tri_ref[...], member.astype(BF16)) + 1.0, 0.0)
        rank_ref[...] = rank
        rank_t_ref[...] = rank.T
        pos_ref[...] = jnp.full(pos_ref.shape, -1.0, F32)
        y_ref[...] = jnp.zeros_like(y_ref)
        base_ref[0] = 0

    base = base_ref[0]
    rrow = rank_t_ref[pl.ds(grp, 1), :]
    n_chunks = (jnp.max(rrow).astype(jnp.int32) + MOE_CHUNK - 1) // MOE_CHUNK
    lane = lax.broadcasted_iota(jnp.int32, (nb, LANES), 1)
    rcol = jnp.sum(jnp.where(lane == grp, rank_ref[...], 0.0), axis=-1, keepdims=True)
    pos_ref[...] = jnp.where(rcol > 0.0, base.astype(F32) + rcol - 1.0, pos_ref[...])

    def chunk(c, carry):
        want = (lax.broadcasted_iota(jnp.int32, (MOE_CHUNK, nb), 0) + (c * MOE_CHUNK + 1)).astype(F32)
        onehot = jnp.where(rrow == want, 1.0, 0.0).astype(BF16)
        xg = _dot(onehot, h_ref[...]).astype(BF16)
        gs3 = _dot(onehot, gsplit_ref[...])
        gs = (gs3 + pltpu.roll(gs3, LANES - EXPERTS_PER_GROUP, 1)
              + pltpu.roll(gs3, LANES - 2 * EXPERTS_PER_GROUP, 1))
        gu = _dot(xg, wgu_ref[0])
        parts = []
        for j in range(EXPERTS_PER_GROUP):
            a = gu[:, j * EXPERT_FF:(j + 1) * EXPERT_FF]
            u = gu[:, ff + j * EXPERT_FF:ff + (j + 1) * EXPERT_FF]
            parts.append((a * (1.0 / (1.0 + jnp.exp(-a))) * u * gs[:, j:j + 1]).astype(BF16))
        y = _dot(jnp.concatenate(parts, axis=1), wd_ref[0])
        off = pl.multiple_of(base + c * MOE_CHUNK, MOE_CHUNK)
        y_ref[pl.ds(off, MOE_CHUNK), :] = y.astype(BF16)
        return carry

    lax.fori_loop(0, n_chunks, chunk, 0)
    base_ref[0] = base + n_chunks * MOE_CHUNK

    @pl.when(grp == N_GROUPS - 1)
    def _():
        slot = lax.broadcasted_iota(jnp.int32, (nb, y_ref.shape[0]), 1).astype(F32)
        scatter = jnp.where(pos_ref[...] == slot, 1.0, 0.0).astype(BF16)
        out = x_ref[...] + _dot(scatter, y_ref[...])
        if final_norm:
            out = _rms_scale(out) * gf_ref[...]
        o_ref[...] = out


def _moe(x, g, wr, br, wgu, wd, gf, final_norm):
    n, d = x.shape
    tm = min(MOE_TILE, n)
    ff = EXPERTS_PER_GROUP * EXPERT_FF
    slots = tm + N_GROUPS * MOE_CHUNK
    tri = jnp.tri(tm, tm, -1, dtype=BF16)
    once = pl.Buffered(1)
    tok = pl.BlockSpec((tm, d), lambda i, e: (i, 0))
    vec = pl.BlockSpec((1, d), lambda i, e: (0, 0))
    return pl.pallas_call(
        functools.partial(_moe_kernel, final_norm=final_norm),
        grid=(n // tm, N_GROUPS),
        in_specs=[pl.BlockSpec((tm, d), lambda i, e: (i, 0), pipeline_mode=once), vec,
                  pl.BlockSpec((d, LANES), lambda i, e: (0, 0), pipeline_mode=once),
                  pl.BlockSpec((1, LANES), lambda i, e: (0, 0)),
                  pl.BlockSpec((tm, tm), lambda i, e: (0, 0), pipeline_mode=once),
                  pl.BlockSpec((1, d, 2 * ff), lambda i, e: (e, 0, 0)),
                  pl.BlockSpec((1, ff, d), lambda i, e: (e, 0, 0)),
                  vec],
        out_specs=tok,
        out_shape=jax.ShapeDtypeStruct((n, d), F32),
        scratch_shapes=[pltpu.VMEM((tm, d), BF16), pltpu.VMEM((tm, LANES), BF16),
                        pltpu.VMEM((tm, LANES), F32), pltpu.VMEM((LANES, tm), F32),
                        pltpu.VMEM((tm, 1), F32), pltpu.VMEM((slots, d), BF16),
                        pltpu.SMEM((1,), jnp.int32)],
        compiler_params=_params("parallel", "arbitrary"),
        name="hier_moe",
    )(x, g, wr, br, tri, wgu, wd, gf)


def _kvq_b_kernel(x_ref, gkv_ref, gq_ref, wkv_ref, wq_ref, k_ref, v_ref, q_ref, *, lead_pad):
    def compute():
        xh = _rms_scale(x_ref[0])
        kv = _dot((xh * gkv_ref[...]).astype(BF16), wkv_ref[...])
        k_ref[0] = kv[:, :D_MODEL].astype(k_ref.dtype)
        v_ref[0] = kv[:, D_MODEL:].astype(v_ref.dtype)
        q = _dot((xh * gq_ref[...]).astype(BF16), wq_ref[...])
        q_ref[0] = (q * (B_DH ** -0.5 * LOG2E)).astype(BF16)

    if lead_pad:
        i = pl.program_id(1)

        @pl.when(i == 0)
        def _():
            k_ref[...] = jnp.zeros_like(k_ref)
            v_ref[...] = jnp.zeros_like(v_ref)

        pl.when(i > 0)(compute)
    else:
        compute()


def _kvq_b(x3, gkv, gq, wkv, wq, kv_dtype, lead_pad, row_block=None):
    b, s, d = x3.shape
    tm = min(ROW_TILE, s)
    if lead_pad:
        assert tm == B_WINDOW
        grid = (b, s // tm + 1)
        src = lambda bi, i: (bi, jnp.maximum(i - 1, 0), 0)
        dst = lambda bi, i: (bi, i, 0)
        s_kv, s_q = s + B_WINDOW, s
    elif row_block is not None:
        grid = (b, 1)
        src = lambda bi, i: (bi, row_block, 0)
        dst = lambda bi, i: (bi, 0, 0)
        s_kv = s_q = tm
    else:
        grid = (b, s // tm)
        src = dst = lambda bi, i: (bi, i, 0)
        s_kv = s_q = s
    vec = pl.BlockSpec((1, d), lambda bi, i: (0, 0))
    return pl.pallas_call(
        functools.partial(_kvq_b_kernel, lead_pad=lead_pad),
        grid=grid,
        in_specs=[pl.BlockSpec((1, tm, d), src), vec, vec,
                  pl.BlockSpec((d, 2 * d), lambda bi, i: (0, 0)),
                  pl.BlockSpec((d, d), lambda bi, i: (0, 0))],
        out_specs=[pl.BlockSpec((1, tm, d), dst), pl.BlockSpec((1, tm, d), dst),
                   pl.BlockSpec((1, tm, d), src if lead_pad else dst)],
        out_shape=[jax.ShapeDtypeStruct((b, s_kv, d), kv_dtype), jax.ShapeDtypeStruct((b, s_kv, d), kv_dtype),
                   jax.ShapeDtypeStruct((b, s_q, d), BF16)],
        compiler_params=_params("parallel", "arbitrary"),
        name="kvq_b",
    )(x3, gkv, gq, wkv, wq)


def _band_bias_kernel(u_ref, o_ref):
    x = jnp.broadcast_to(u_ref[0], (BAND_TQ, 2 * B_WINDOW))
    bias = pltpu.roll(x, 0, 1, stride=1, stride_axis=0)[:, :BAND_KW]
    qc = _chunk_of(lax.broadcasted_iota(jnp.int32, bias.shape, 0))
    kc = _chunk_of(lax.broadcasted_iota(jnp.int32, bias.shape, 1))
    ok = (kc >= qc) & (kc <= qc + B_WINDOW // CHUNK)
    o_ref[0] = jnp.where(ok, bias * LOG2E, NEG)


def _band_bias(rel_bias):
    edge = jnp.broadcast_to(rel_bias[:, 2 * REL_CLIP:], (B_HEADS, REL_CLIP))
    u = jnp.concatenate([edge, rel_bias[:, :0:-1], edge], axis=1)[:, None, :]
    assert u.shape[-1] == 2 * B_WINDOW
    return pl.pallas_call(
        _band_bias_kernel,
        grid=(B_HEADS,),
        in_specs=[pl.BlockSpec((1, 1, 2 * B_WINDOW), lambda h: (h, 0, 0))],
        out_specs=pl.BlockSpec((1, BAND_TQ, BAND_KW), lambda h: (h, 0, 0)),
        out_shape=jax.ShapeDtypeStruct((B_HEADS, BAND_TQ, BAND_KW), F32),
        compiler_params=_params("parallel"),
        name="band_bias",
    )(u)


def _pick_halves(o, t):
    lane = lax.broadcasted_iota(jnp.int32, (t, LANES), 1)
    return jnp.where(lane < 64, o[:t], o[t:])


def _band_prompt_kernel(q_ref, k_ref, v_ref, bias_ref, o_ref):
    ti = pl.program_id(2)
    t = BAND_TQ
    qs = _stack_halves(q_ref[0])
    off = pl.multiple_of(ti * t, t)
    kw = k_ref[0, pl.ds(off, BAND_KW), :]
    vw = v_ref[0, pl.ds(off, BAND_KW), :]
    s = _dot_nt(qs, kw) + bias_ref[...].reshape(2 * t, BAND_KW)
    vx = jnp.concatenate([vw, jnp.ones((BAND_KW, LANES), BF16)], axis=1)

    def finish(s):
        m = jnp.max(s, axis=-1, keepdims=True)
        o = _dot(jnp.exp2((s - m).astype(BF16)), vx)
        o = o[:, :LANES] * (1.0 / o[:, LANES:])
        o_ref[0] = _pick_halves(o, t).astype(BF16)

    @pl.when(off < B_WINDOW)
    def _():
        c = lax.broadcasted_iota(jnp.int32, s.shape, 1)
        finish(jnp.where(c + off >= B_WINDOW, s, NEG))

    @pl.when(off >= B_WINDOW)
    def _():
        finish(s)


def _band_prompt(q, kpad, vpad, bias):
    b, s, d = q.shape
    t = BAND_TQ
    nhp = d // LANES
    win = pl.BlockSpec((1, s + B_WINDOW, LANES), lambda hp, bi, ti: (bi, 0, hp))
    tile = pl.BlockSpec((1, t, LANES), lambda hp, bi, ti: (bi, ti, hp))
    return pl.pallas_call(
        _band_prompt_kernel,
        grid=(nhp, b, s // t),
        in_specs=[tile, win, win, pl.BlockSpec((2, t, BAND_KW), lambda hp, bi, ti: (hp, 0, 0))],
        out_specs=tile,
        out_shape=jax.ShapeDtypeStruct((b, s, d), BF16),
        compiler_params=_params("parallel", "parallel", "arbitrary"),
        name="band_attn_prompt",
    )(q, kpad, vpad, bias)


def _band_sample_kernel(q_ref, kc_ref, kn_ref, vc_ref, vn_ref, bias_ref, o_ref):
    t = q_ref.shape[1]
    wb = kc_ref.shape[1]
    qs = _stack_halves(q_ref[0])
    kc = kc_ref[0].astype(BF16)
    vc = vc_ref[0].astype(BF16)
    kn = _pad_rows(kn_ref[0].astype(BF16), LANES)
    vn = _pad_rows(vn_ref[0].astype(BF16), LANES)
    bias = bias_ref[...].reshape(2 * t, BAND_KW)
    sc = _dot_nt(qs, kc) + bias[:, :wb]
    sn = _dot_nt(qs, kn) + bias[:, wb:wb + LANES]
    col = lax.broadcasted_iota(jnp.int32, sn.shape, 1)
    sn = jnp.where(col < t, sn, NEG)
    m = jnp.maximum(jnp.max(sc, axis=-1, keepdims=True), jnp.max(sn, axis=-1, keepdims=True))
    pc = jnp.exp2(sc - m)
    pn = jnp.exp2(sn - m)
    l = jnp.sum(pc, axis=-1, keepdims=True) + jnp.sum(pn, axis=-1, keepdims=True)
    o = (_dot(pc.astype(BF16), vc) + _dot(pn.astype(BF16), vn)) * (1.0 / l)
    o_ref[0] = _pick_halves(o, t).astype(BF16)


def _band_sample(q, kc, kn, vc, vn, bias):
    b, t, d = q.shape
    wb = kc.shape[1]
    assert wb == B_WINDOW
    new = pl.BlockSpec((1, t, LANES), lambda bi, hp: (bi, 0, hp))
    old = pl.BlockSpec((1, wb, LANES), lambda bi, hp: (bi, 0, hp))
    return pl.pallas_call(
        _band_sample_kernel,
        grid=(b, d // LANES),
        in_specs=[new, old, new, old, new, pl.BlockSpec((2, t, BAND_KW), lambda bi, hp: (hp, 0, 0))],
        out_specs=new,
        out_shape=jax.ShapeDtypeStruct((b, t, d), BF16),
        compiler_params=_params("parallel", "parallel"),
        name="band_attn_sample",
    )(q, kc, kn, vc, vn, bias)


def kernel(x_prompt, x_sample, cache_k_a, cache_v_a, cache_k_b, cache_v_b, g_attn, g_ffn, w_q_a, w_k_a, w_v_a, lam_q1, lam_k1, lam_q2, lam_k2, g_sub_a, w_o_a, g_kv, w_kv_b, w_q_b, rel_bias_b, w_o_b, w_router_g, b_router_g, w_router_e, b_router_e, w_gate, w_up, w_down, g_final):
    bp, sp, d = x_prompt.shape
    bs, ts, _ = x_sample.shape
    past = cache_k_a.shape[2]
    wb = cache_k_b.shape[1]
    n_p, n_s = bp * sp, bs * ts
    assert d == D_MODEL and sp % ROW_TILE == 0 and ROW_TILE % ts == 0 and ts <= LANES
    depth = g_attn.shape[0]
    n_a = w_q_a.shape[0]
    assert depth == 2 and n_a == 1 and w_q_b.shape[0] == 1

    row = lambda v: v.reshape(1, -1).astype(F32)
    bf = lambda w: w.astype(BF16)

    def moe_weights(l):
        wr = jnp.concatenate([w_router_g[l], w_router_e[l].transpose(1, 0, 2).reshape(d, N_EXPERTS)], axis=1)
        wr = jnp.pad(wr, ((0, 0), (0, LANES - wr.shape[1])))
        br = jnp.concatenate([b_router_g[l], b_router_e[l].reshape(-1)])
        br = jnp.pad(br, (0, LANES - br.shape[0]))[None, :]
        by_group = lambda w: bf(w).reshape(N_GROUPS, EXPERTS_PER_GROUP, d, EXPERT_FF).transpose(0, 2, 1, 3).reshape(
            N_GROUPS, d, EXPERTS_PER_GROUP * EXPERT_FF)
        wgu = jnp.concatenate([by_group(w_gate[l]), by_group(w_up[l])], axis=-1)
        wd = bf(w_down[l]).reshape(N_GROUPS, EXPERTS_PER_GROUP * EXPERT_FF, d)
        return row(g_ffn[l]), wr.astype(F32), br.astype(F32), wgu, wd

    xp = x_prompt.reshape(n_p, d)
    xs = x_sample.reshape(n_s, d)

    lam_init = 0.8 - 0.6 * math.exp(-0.3 * 0)
    lamv = jnp.stack([lam_q1[0], lam_k1[0], lam_q2[0], lam_k2[0]]).astype(F32)
    gsub = row(g_sub_a[0])
    wq, wk, wv, wo = bf(w_q_a[0]), bf(w_k_a[0]), bf(w_v_a[0]), bf(w_o_a[0])
    g0 = row(g_attn[0])

    cos_p, sin_p = _rope_table(sp, sp, 0, min(ROW_TILE, sp))
    qp, kfp, kbp, vfp, vbp = _qkv_a(x_prompt, g0, wq, wk, wv, cos_p, sin_p, True, A_DK ** -0.5 * LOG2E)
    ap = _diff_prompt(lamv, gsub, qp, kbp, vbp, lam_init)
    xp = _out_proj(ap.reshape(n_p, d), xp, wo)

    tm_s = min(ROW_TILE, n_s)
    cos_s, sin_s = _rope_table(tm_s, ts, past, tm_s)
    qs, kfs, _, vfs, _ = _qkv_a(xs.reshape(1, n_s, d), g0, wq, wk, wv, cos_s, sin_s, False, A_DK ** -0.5)
    kfs3, vfs3 = kfs.reshape(bs, ts, d), vfs.reshape(bs, ts, d)
    a_s = _diff_sample(lamv, gsub, qs.reshape(bs, ts, d), cache_k_a[0].reshape(bs, past, d), bf(kfs3),
                       cache_v_a[0].reshape(bs, past, d), bf(vfs3), lam_init)
    xs = _out_proj(a_s.reshape(n_s, d), xs, wo)

    mw = moe_weights(0)
    gfin = row(g_final)
    xp = _moe(xp, *mw, gfin, final_norm=False)
    xs = _moe(xs, *mw, gfin, final_norm=False)

    gkv, g1 = row(g_kv), row(g_attn[1])
    wkv, wqb, wob = bf(w_kv_b), bf(w_q_b[0]), bf(w_o_b[0])
    bias = _band_bias(rel_bias_b[0].astype(F32))

    xp3 = xp.reshape(bp, sp, d)
    kpad, vpad, qb = _kvq_b(xp3, gkv, g1, wkv, wqb, BF16, lead_pad=True)
    kbt, vbt, _ = _kvq_b(xp3, gkv, g1, wkv, wqb, F32, lead_pad=False, row_block=sp // ROW_TILE - 1)
    ab = _band_prompt(qb, kpad, vpad, bias)
    xp = _out_proj(ab.reshape(n_p, d), xp, wob)

    kns, vns, qbs = _kvq_b(xs.reshape(1, n_s, d), gkv, g1, wkv, wqb, F32, lead_pad=False)
    kns3, vns3 = kns.reshape(bs, ts, d), vns.reshape(bs, ts, d)
    abs_ = _band_sample(qbs.reshape(bs, ts, d), cache_k_b.reshape(bs, wb, d), kns3,
                        cache_v_b.reshape(bs, wb, d), vns3, bias)
    xs = _out_proj(abs_.reshape(n_s, d), xs, wob)

    mw = moe_weights(1)
    y_prompt = _moe(xp, *mw, gfin, final_norm=True).reshape(bp, sp, d)
    y_sample = _moe(xs, *mw, gfin, final_norm=True).reshape(bs, ts, d)

    wp = min(B_WINDOW, sp)
    assert wp == ROW_TILE
    k_a_prompt = kfp.reshape(1, bp, sp, 2 * A_HEADS, A_DK)
    v_a_prompt = vfp.reshape(1, bp, sp, A_HEADS, A_DV)
    k_a_sample = kfs.reshape(1, bs, ts, 2 * A_HEADS, A_DK)
    v_a_sample = vfs.reshape(1, bs, ts, A_HEADS, A_DV)
    k_b_prompt = kbt.reshape(bp, wp, B_HEADS, B_DH)
    v_b_prompt = vbt.reshape(bp, wp, B_HEADS, B_DH)
    k_b_sample = jnp.concatenate([cache_k_b.reshape(bs, wb, d)[:, ts:], kns3], axis=1).reshape(bs, wb, B_HEADS, B_DH)
    v_b_sample = jnp.concatenate([cache_v_b.reshape(bs, wb, d)[:, ts:], vns3], axis=1).reshape(bs, wb, B_HEADS, B_DH)
    return (y_prompt, y_sample, k_a_prompt, v_a_prompt, k_b_prompt, v_b_prompt,
            k_a_sample, v_a_sample, k_b_sample, v_b_sample)
```

```python
import functools
import math

import jax
import jax.numpy as jnp
from jax import lax
from jax.experimental import pallas as pl
from jax.experimental.pallas import tpu as pltpu

F32 = jnp.float32
BF16 = jnp.bfloat16

EPS = 1e-6
CHUNK = 64
D_MODEL = 1024
A_HEADS = 8
A_DK = 64
A_DV = 128
B_HEADS = 16
B_DH = 64
B_WINDOW = 512
REL_CLIP = 256
N_GROUPS = 4
EXPERTS_PER_GROUP = 4
N_EXPERTS = 16
EXPERT_FF = 512
ROPE_THETA = 10000.0

LANES = 128
NEG = -1e30
LOG2E = math.log2(math.e)
VMEM_LIMIT = 56 * 1024 * 1024

ROW_TILE = 512
MOE_TILE = 1024
MOE_CHUNK = 128
HEADS_PER_STEP = 4
ONES_ROWS = 16
BAND_TQ = 256
BAND_KW = BAND_TQ + B_WINDOW
BAND_ROWS = 128


def _params(*sem):
    return pltpu.CompilerParams(dimension_semantics=sem, vmem_limit_bytes=VMEM_LIMIT)


def _rms_scale(x):
    return x * lax.rsqrt(jnp.mean(x * x, axis=-1, keepdims=True) + EPS)


def _dot(a, b):
    return jnp.dot(a, b, preferred_element_type=F32)


def _dot_nt(a, b):
    return lax.dot_general(a, b, (((1,), (1,)), ((), ())), preferred_element_type=F32)


def _chunk_of(pos):
    return jnp.right_shift(pos, CHUNK.bit_length() - 1)


def _stack_halves(q):
    lane = lax.broadcasted_iota(jnp.int32, q.shape, 1)
    zero = jnp.zeros_like(q)
    return jnp.concatenate([jnp.where(lane < 64, q, zero), jnp.where(lane >= 64, q, zero)], axis=0)


def _rope_table_kernel(inv_ref, cos_ref, sin_ref, *, period, offset, rows):
    i = pl.program_id(0)
    row = lax.broadcasted_iota(jnp.int32, (rows, LANES), 0) + i * rows
    lane = lax.broadcasted_iota(jnp.int32, (rows, LANES), 1)
    pos = (offset + lax.rem(row, period)).astype(F32)
    ang = pos * inv_ref[...]
    s = jnp.sin(ang)
    cos_ref[...] = jnp.cos(ang)
    sin_ref[...] = jnp.where(lax.rem(lane, 64) < 32, -s, s)


def _rope_table(n_rows, period, offset, rows):
    half = A_DK // 2
    inv = jnp.power(ROPE_THETA, -jnp.arange(half, dtype=F32) / half)
    inv = jnp.tile(inv, LANES // half)[None, :]
    return pl.pallas_call(
        functools.partial(_rope_table_kernel, period=period, offset=offset, rows=rows),
        grid=(n_rows // rows,),
        in_specs=[pl.BlockSpec((1, LANES), lambda i: (0, 0))],
        out_specs=[pl.BlockSpec((rows, LANES), lambda i: (i, 0))] * 2,
        out_shape=[jax.ShapeDtypeStruct((n_rows, LANES), F32)] * 2,
        compiler_params=_params("arbitrary"),
        name="rope_table",
    )(inv)


def _qkv_a_kernel(x_ref, g_ref, wq_ref, wk_ref, wv_ref, cos_ref, sin_ref,
                  q_ref, kf_ref, kb_ref, vf_ref, vb_ref, *, transposed, q_scale):
    h = (_rms_scale(x_ref[0]) * g_ref[...]).astype(BF16)
    cos = cos_ref[...]
    sin = sin_ref[...]
    lane = lax.broadcasted_iota(jnp.int32, cos.shape, 1)
    first = lax.rem(lane, 64) < 32

    yq = _dot(h, wq_ref[...])
    yk = _dot(h, wk_ref[...])
    yv = _dot(h, wv_ref[...])
    for c in range(D_MODEL // LANES):
        sl = slice(c * LANES, (c + 1) * LANES)

        def rope(y):
            yc = y[:, sl]
            rot = jnp.where(first, pltpu.roll(yc, LANES - 32, 1), pltpu.roll(yc, 32, 1))
            return yc * cos + rot * sin

        rq = rope(yq) * q_scale
        rk = rope(yk)
        vc = yv[:, sl]
        kf_ref[0, :, sl] = rk
        vf_ref[0, :, sl] = vc
        if transposed:
            q_ref[0, c] = rq.T.astype(BF16)
            kb_ref[0, c] = rk.astype(BF16)
            vb_ref[0, c, 0, :A_DV] = vc.T.astype(BF16)
            vb_ref[0, c, 0, A_DV:] = jnp.ones((ONES_ROWS, vc.shape[0]), BF16)
        else:
            q_ref[0, :, sl] = rq.astype(BF16)
            kb_ref[0, :, sl] = rk.astype(BF16)
            vb_ref[0, :, sl] = vc.astype(BF16)


def _qkv_a(x3, g, wq, wk, wv, cos, sin, transposed, q_scale):
    b, s, d = x3.shape
    tm = min(ROW_TILE, s)
    n_tab = cos.shape[0] // tm
    nh = d // LANES
    tok = pl.BlockSpec((1, tm, d), lambda bi, i: (bi, i, 0))
    fshape = jax.ShapeDtypeStruct((b, s, d), F32)
    if transposed:
        hspecs = [pl.BlockSpec((1, nh, LANES, tm), lambda bi, i: (bi, 0, 0, i)),
                  pl.BlockSpec((1, nh, tm, LANES), lambda bi, i: (bi, 0, i, 0)),
                  pl.BlockSpec((1, nh, 1, A_DV + ONES_ROWS, tm), lambda bi, i: (bi, 0, i, 0, 0))]
        hshapes = [jax.ShapeDtypeStruct((b, nh, LANES, s), BF16),
                   jax.ShapeDtypeStruct((b, nh, s, LANES), BF16),
                   jax.ShapeDtypeStruct((b, nh, s // tm, A_DV + ONES_ROWS, tm), BF16)]
    else:
        hspecs = [tok] * 3
        hshapes = [jax.ShapeDtypeStruct((b, s, d), BF16)] * 3
    wspec = pl.BlockSpec((d, d), lambda bi, i: (0, 0))
    tab = pl.BlockSpec((tm, LANES), lambda bi, i: (i % n_tab, 0))
    return pl.pallas_call(
        functools.partial(_qkv_a_kernel, transposed=transposed, q_scale=q_scale),
        grid=(b, s // tm),
        in_specs=[tok, pl.BlockSpec((1, d), lambda bi, i: (0, 0)), wspec, wspec, wspec, tab, tab],
        out_specs=[hspecs[0], tok, hspecs[1], tok, hspecs[2]],
        out_shape=[hshapes[0], fshape, hshapes[1], fshape, hshapes[2]],
        compiler_params=_params("parallel", "parallel"),
        name="qkv_a",
    )(x3, g, wq, wk, wv, cos, sin)


def _diff_lambda(lam_ref, lam_init):
    lv = lam_ref[...]
    s1 = jnp.sum(lv[0:1] * lv[1:2], axis=-1, keepdims=True)
    s2 = jnp.sum(lv[2:3] * lv[3:4], axis=-1, keepdims=True)
    return jnp.exp(s1) - jnp.exp(s2) + lam_init


def _diff_finish(o1, o2, lam, gsub, lam_init):
    d = o1 - lam * o2
    return (_rms_scale(d) * gsub * (1.0 - lam_init)).astype(BF16)


def _diff_prompt_kernel(lam_ref, gsub_ref, qt_ref, k_ref, vt_ref, o_ref, s_ref, p_ref, *, t, lam_init):
    qi = pl.program_id(2)
    nh = qt_ref.shape[1]
    row = lax.broadcasted_iota(jnp.int32, (A_DV, t), 0)
    zero = jnp.zeros((A_DV, t), BF16)
    qst = [jnp.concatenate([jnp.where(row < A_DK, qt_ref[0, h], zero), jnp.where(row >= A_DK, qt_ref[0, h], zero)],
                           axis=1) for h in range(nh)]

    def scores(h, tile, m_prev, mask):
        s = _dot(k_ref[0, h, pl.ds(pl.multiple_of(tile * t, t), t), :], qst[h])
        if mask is not None:
            s = jnp.where(mask, s, NEG)
        m = jnp.maximum(m_prev, jnp.max(s, axis=0, keepdims=True))
        s_ref[h] = s
        return m, jnp.exp2(m_prev - m)

    def accumulate(h, tile, alpha, acc):
        return alpha * acc + _dot(vt_ref[0, h, tile], p_ref[h])

    def exponentials(h, m):
        p_ref[h] = jnp.exp2((s_ref[h] - m).astype(BF16))

    kk = lax.broadcasted_iota(jnp.int32, (t, 2 * t), 0)
    qq = lax.broadcasted_iota(jnp.int32, (t, 2 * t), 1)
    mask = _chunk_of(kk) <= _chunk_of(jnp.bitwise_and(qq, t - 1))
    m0 = jnp.full((1, 2 * t), NEG, F32)
    carry = []
    for h in range(nh):
        m, alpha = scores(h, qi, m0, mask)
        p_ref[h] = jnp.zeros((t, 2 * t), BF16)
        carry.append((m, alpha, jnp.ones((1, 2 * t), F32), jnp.zeros((A_DV + ONES_ROWS, 2 * t), F32)))

    def v_tile(n):
        return jnp.where(n == 0, qi, jnp.maximum(n - 1, 0))

    def body(j, carry):
        out = []
        for h in range(nh):
            m_s, alpha_s, alpha_p, acc = carry[h]
            acc = accumulate(h, v_tile(j - 1), alpha_p, acc)
            exponentials(h, m_s)
            m_new, alpha_new = scores(h, j, m_s, None)
            out.append((m_new, alpha_new, alpha_s, acc))
        return tuple(out)

    carry = lax.fori_loop(0, qi, body, tuple(carry))
    lam = _diff_lambda(lam_ref, lam_init)
    for h in range(nh):
        m_s, alpha_s, alpha_p, acc = carry[h]
        acc = accumulate(h, v_tile(qi - 1), alpha_p, acc)
        exponentials(h, m_s)
        acc = accumulate(h, v_tile(qi), alpha_s, acc)
        o = (acc[:A_DV] * (1.0 / acc[A_DV:A_DV + 1])).T
        o_ref[0, :, h * A_DV:(h + 1) * A_DV] = _diff_finish(o[:t], o[t:], lam, gsub_ref[...], lam_init)


def _diff_prompt(lamv, gsub, qt, k, vt, lam_init):
    b, nh, _, s = qt.shape
    t = vt.shape[-1]
    hs = HEADS_PER_STEP
    return pl.pallas_call(
        functools.partial(_diff_prompt_kernel, t=t, lam_init=lam_init),
        grid=(b, nh // hs, s // t),
        in_specs=[pl.BlockSpec(lamv.shape, lambda bi, hi, qi: (0, 0)),
                  pl.BlockSpec((1, A_DV), lambda bi, hi, qi: (0, 0)),
                  pl.BlockSpec((1, hs, LANES, t), lambda bi, hi, qi: (bi, hi, 0, qi)),
                  pl.BlockSpec((1, hs, s, LANES), lambda bi, hi, qi: (bi, hi, 0, 0), pipeline_mode=pl.Buffered(1)),
                  pl.BlockSpec((1, hs, s // t, A_DV + ONES_ROWS, t), lambda bi, hi, qi: (bi, hi, 0, 0, 0),
                               pipeline_mode=pl.Buffered(1))],
        out_specs=pl.BlockSpec((1, t, hs * LANES), lambda bi, hi, qi: (bi, qi, hi)),
        out_shape=jax.ShapeDtypeStruct((b, s, nh * LANES), BF16),
        scratch_shapes=[pltpu.VMEM((hs, t, 2 * t), F32), pltpu.VMEM((hs, t, 2 * t), BF16)],
        compiler_params=_params("parallel", "parallel", "arbitrary"),
        name="diff_attn_prompt",
    )(lamv, gsub, qt, k, vt)


def _pad_rows(x, rows):
    return jnp.concatenate([x, jnp.zeros((rows - x.shape[0], x.shape[1]), x.dtype)], axis=0)


def _diff_sample_kernel(lam_ref, gsub_ref, q_ref, kc_ref, kn_ref, vc_ref, vn_ref, o_ref, *, lam_init):
    t = q_ref.shape[1]
    qs = _stack_halves(q_ref[0])
    kc = kc_ref[0].astype(BF16)
    vc = vc_ref[0].astype(BF16)
    kn = _pad_rows(kn_ref[0], LANES)
    vn = _pad_rows(vn_ref[0], LANES)
    sc = _dot_nt(qs, kc)
    sn = _dot_nt(qs, kn)
    col = lax.broadcasted_iota(jnp.int32, sn.shape, 1)
    sn = jnp.where(col < t, sn, NEG)
    m = jnp.maximum(jnp.max(sc, axis=-1, keepdims=True), jnp.max(sn, axis=-1, keepdims=True))
    pc = jnp.exp(sc - m)
    pn = jnp.exp(sn - m)
    l = jnp.sum(pc, axis=-1, keepdims=True) + jnp.sum(pn, axis=-1, keepdims=True)
    acc = _dot(pc.astype(BF16), vc) + _dot(pn.astype(BF16), vn)
    o = acc * (1.0 / l)
    o_ref[0] = _diff_finish(o[:t], o[t:], _diff_lambda(lam_ref, lam_init), gsub_ref[...], lam_init)


def _diff_sample(lamv, gsub, q, kc, kn, vc, vn, lam_init):
    b, t, d = q.shape
    past = kc.shape[1]
    new = pl.BlockSpec((1, t, LANES), lambda bi, hi: (bi, 0, hi))
    old = pl.BlockSpec((1, past, LANES), lambda bi, hi: (bi, 0, hi))
    return pl.pallas_call(
        functools.partial(_diff_sample_kernel, lam_init=lam_init),
        grid=(b, d // LANES),
        in_specs=[pl.BlockSpec(lamv.shape, lambda bi, hi: (0, 0)),
                  pl.BlockSpec((1, A_DV), lambda bi, hi: (0, 0)), new, old, new, old, new],
        out_specs=new,
        out_shape=jax.ShapeDtypeStruct((b, t, d), BF16),
        compiler_params=_params("parallel", "parallel"),
        name="diff_attn_sample",
    )(lamv, gsub, q, kc, kn, vc, vn)


def _out_proj_kernel(a_ref, x_ref, w_ref, o_ref):
    o_ref[...] = x_ref[...] + _dot(a_ref[...], w_ref[...])


def _out_proj(a, x, w):
    n, d = x.shape
    tm = min(ROW_TILE, n)
    tok = pl.BlockSpec((tm, d), lambda i: (i, 0))
    return pl.pallas_call(
        _out_proj_kernel,
        grid=(n // tm,),
        in_specs=[tok, tok, pl.BlockSpec((d, d), lambda i: (0, 0))],
        out_specs=tok,
        out_shape=jax.ShapeDtypeStruct((n, d), F32),
        compiler_params=_params("parallel"),
        name="out_proj",
    )(a, x, w)


def _moe_gates(hf, wr_ref, br_ref):
    lg = jnp.dot(hf, wr_ref[...], preferred_element_type=F32, precision=lax.Precision.HIGHEST) + br_ref[...]
    lane = lax.broadcasted_iota(jnp.int32, lg.shape, 1).astype(F32)
    big = float(LANES)

    def first_argmax(v):
        mx = jnp.max(v, axis=-1, keepdims=True)
        return mx, jnp.min(jnp.where(v == mx, lane, big), axis=-1, keepdims=True)

    is_g = lane < N_GROUPS
    gmax, gidx = first_argmax(jnp.where(is_g, lg, NEG))
    gval = 1.0 / jnp.sum(jnp.where(is_g, jnp.exp(lg - gmax), 0.0), axis=-1, keepdims=True)
    lo = N_GROUPS + EXPERTS_PER_GROUP * gidx
    el = jnp.where((lane >= lo) & (lane < lo + EXPERTS_PER_GROUP), lg, NEG)
    v1, i1 = first_argmax(el)
    v2, i2 = first_argmax(jnp.where(lane == i1, NEG, el))
    e2 = jnp.exp(v2 - v1)
    inv = gval / (1.0 + e2)
    gates = jnp.where(lane == i1 - lo, inv, 0.0) + jnp.where(lane == i2 - lo, inv * e2, 0.0)
    return jnp.where(lane == gidx, 1.0, 0.0), gates


def _moe_kernel(x_ref, g_ref, wr_ref, br_ref, tri_ref, wgu_ref, wd_ref, gf_ref, o_ref,
                h_ref, gsplit_ref, rank_ref, rank_t_ref, pos_ref, y_ref, base_ref, *, final_norm):
    grp = pl.program_id(1)
    nb = x_ref.shape[0]
    ff = EXPERTS_PER_GROUP * EXPERT_FF

    @pl.when(grp == 0)
    def _():
        hf = _rms_scale(x_ref[...]) * g_ref[...]
        h_ref[...] = hf.astype(BF16)
        member, gates = _moe_gates(hf, wr_ref, br_ref)
        rest = gates
        split = jnp.zeros_like(gates)
        for i in range(3):
            part = rest.astype(BF16).astype(F32)
            split = split + pltpu.roll(part, EXPERTS_PER_GROUP * i, 1)
            rest = rest - part
        gsplit_ref[...] = split.astype(BF16)
        rank = jnp.where(member > 0.0, _dot(tri_ref[...], member.astype(BF16)) + 1.0, 0.0)
        rank_ref[...] = rank
        rank_t_ref[...] = rank.T
        pos_ref[...] = jnp.full(pos_ref.shape, -1.0, F32)
        y_ref[...] = jnp.zeros_like(y_ref)
        base_ref[0] = 0

    base = base_ref[0]
    rrow = rank_t_ref[pl.ds(grp, 1), :]
    n_chunks = (jnp.max(rrow).astype(jnp.int32) + MOE_CHUNK - 1) // MOE_CHUNK
    lane = lax.broadcasted_iota(jnp.int32, (nb, LANES), 1)
    rcol = jnp.sum(jnp.where(lane == grp, rank_ref[...], 0.0), axis=-1, keepdims=True)
    pos_ref[...] = jnp.where(rcol > 0.0, base.astype(F32) + rcol - 1.0, pos_ref[...])

    def chunk(c, carry):
        want = (lax.broadcasted_iota(jnp.int32, (MOE_CHUNK, nb), 0) + (c * MOE_CHUNK + 1)).astype(F32)
        onehot = jnp.where(rrow == want, 1.0, 0.0).astype(BF16)
        xg = _dot(onehot, h_ref[...]).astype(BF16)
        gs3 = _dot(onehot, gsplit_ref[...])
        gs = (gs3 + pltpu.roll(gs3, LANES - EXPERTS_PER_GROUP, 1)
              + pltpu.roll(gs3, LANES - 2 * EXPERTS_PER_GROUP, 1))
        gu = _dot(xg, wgu_ref[0])
        parts = []
        for j in range(EXPERTS_PER_GROUP):
            a = gu[:, j * EXPERT_FF:(j + 1) * EXPERT_FF]
            u = gu[:, ff + j * EXPERT_FF:ff + (j + 1) * EXPERT_FF]
            parts.append((a * (1.0 / (1.0 + jnp.exp(-a))) * u * gs[:, j:j + 1]).astype(BF16))
        y = _dot(jnp.concatenate(parts, axis=1), wd_ref[0])
        off = pl.multiple_of(base + c * MOE_CHUNK, MOE_CHUNK)
        y_ref[pl.ds(off, MOE_CHUNK), :] = y.astype(BF16)
        return carry

    lax.fori_loop(0, n_chunks, chunk, 0)
    base_ref[0] = base + n_chunks * MOE_CHUNK

    @pl.when(grp == N_GROUPS - 1)
    def _():
        slot = lax.broadcasted_iota(jnp.int32, (nb, y_ref.shape[0]), 1).astype(F32)
        scatter = jnp.where(pos_ref[...] == slot, 1.0, 0.0).astype(BF16)
        out = x_ref[...] + _dot(scatter, y_ref[...])
        if final_norm:
            out = _rms_scale(out) * gf_ref[...]
        o_ref[...] = out


def _moe(x, g, wr, br, wgu, wd, gf, final_norm):
    n, d = x.shape
    tm = min(MOE_TILE, n)
    ff = EXPERTS_PER_GROUP * EXPERT_FF
    slots = tm + N_GROUPS * MOE_CHUNK
    tri = jnp.tri(tm, tm, -1, dtype=BF16)
    once = pl.Buffered(1)
    tok = pl.BlockSpec((tm, d), lambda i, e: (i, 0))
    vec = pl.BlockSpec((1, d), lambda i, e: (0, 0))
    return pl.pallas_call(
        functools.partial(_moe_kernel, final_norm=final_norm),
        grid=(n // tm, N_GROUPS),
        in_specs=[pl.BlockSpec((tm, d), lambda i, e: (i, 0), pipeline_mode=once), vec,
                  pl.BlockSpec((d, LANES), lambda i, e: (0, 0), pipeline_mode=once),
                  pl.BlockSpec((1, LANES), lambda i, e: (0, 0)),
                  pl.BlockSpec((tm, tm), lambda i, e: (0, 0), pipeline_mode=once),
                  pl.BlockSpec((1, d, 2 * ff), lambda i, e: (e, 0, 0)),
                  pl.BlockSpec((1, ff, d), lambda i, e: (e, 0, 0)),
                  vec],
        out_specs=tok,
        out_shape=jax.ShapeDtypeStruct((n, d), F32),
        scratch_shapes=[pltpu.VMEM((tm, d), BF16), pltpu.VMEM((tm, LANES), BF16),
                        pltpu.VMEM((tm, LANES), F32), pltpu.VMEM((LANES, tm), F32),
                        pltpu.VMEM((tm, 1), F32), pltpu.VMEM((slots, d), BF16),
                        pltpu.SMEM((1,), jnp.int32)],
        compiler_params=_params("parallel", "arbitrary"),
        name="hier_moe",
    )(x, g, wr, br, tri, wgu, wd, gf)


def _kvq_b_kernel(x_ref, gkv_ref, gq_ref, wkv_ref, wq_ref, k_ref, v_ref, q_ref, *, lead_pad):
    def compute():
        xh = _rms_scale(x_ref[0])
        kv = _dot((xh * gkv_ref[...]).astype(BF16), wkv_ref[...])
        k_ref[0] = kv[:, :D_MODEL].astype(k_ref.dtype)
        v_ref[0] = kv[:, D_MODEL:].astype(v_ref.dtype)
        q = _dot((xh * gq_ref[...]).astype(BF16), wq_ref[...])
        q_ref[0] = (q * (B_DH ** -0.5 * LOG2E)).astype(BF16)

    if lead_pad:
        i = pl.program_id(1)

        @pl.when(i == 0)
        def _():
            k_ref[...] = jnp.zeros_like(k_ref)
            v_ref[...] = jnp.zeros_like(v_ref)

        pl.when(i > 0)(compute)
    else:
        compute()


def _kvq_b(x3, gkv, gq, wkv, wq, kv_dtype, lead_pad, row_block=None):
    b, s, d = x3.shape
    tm = min(ROW_TILE, s)
    if lead_pad:
        assert tm == B_WINDOW
        grid = (b, s // tm + 1)
        src = lambda bi, i: (bi, jnp.maximum(i - 1, 0), 0)
        dst = lambda bi, i: (bi, i, 0)
        s_kv, s_q = s + B_WINDOW, s
    elif row_block is not None:
        grid = (b, 1)
        src = lambda bi, i: (bi, row_block, 0)
        dst = lambda bi, i: (bi, 0, 0)
        s_kv = s_q = tm
    else:
        grid = (b, s // tm)
        src = dst = lambda bi, i: (bi, i, 0)
        s_kv = s_q = s
    vec = pl.BlockSpec((1, d), lambda bi, i: (0, 0))
    return pl.pallas_call(
        functools.partial(_kvq_b_kernel, lead_pad=lead_pad),
        grid=grid,
        in_specs=[pl.BlockSpec((1, tm, d), src), vec, vec,
                  pl.BlockSpec((d, 2 * d), lambda bi, i: (0, 0)),
                  pl.BlockSpec((d, d), lambda bi, i: (0, 0))],
        out_specs=[pl.BlockSpec((1, tm, d), dst), pl.BlockSpec((1, tm, d), dst),
                   pl.BlockSpec((1, tm, d), src if lead_pad else dst)],
        out_shape=[jax.ShapeDtypeStruct((b, s_kv, d), kv_dtype), jax.ShapeDtypeStruct((b, s_kv, d), kv_dtype),
                   jax.ShapeDtypeStruct((b, s_q, d), BF16)],
        compiler_params=_params("parallel", "arbitrary"),
        name="kvq_b",
    )(x3, gkv, gq, wkv, wq)


def _band_bias_kernel(u_ref, o_ref):
    x = jnp.broadcast_to(u_ref[0], (BAND_TQ, 2 * B_WINDOW))
    bias = pltpu.roll(x, 0, 1, stride=1, stride_axis=0)[:, :BAND_KW]
    qc = _chunk_of(lax.broadcasted_iota(jnp.int32, bias.shape, 0))
    kc = _chunk_of(lax.broadcasted_iota(jnp.int32, bias.shape, 1))
    ok = (kc >= qc) & (kc <= qc + B_WINDOW // CHUNK)
    o_ref[0] = jnp.where(ok, bias * LOG2E, NEG)


def _band_bias(rel_bias):
    edge = jnp.broadcast_to(rel_bias[:, 2 * REL_CLIP:], (B_HEADS, REL_CLIP))
    u = jnp.concatenate([edge, rel_bias[:, :0:-1], edge], axis=1)[:, None, :]
    assert u.shape[-1] == 2 * B_WINDOW
    return pl.pallas_call(
        _band_bias_kernel,
        grid=(B_HEADS,),
        in_specs=[pl.BlockSpec((1, 1, 2 * B_WINDOW), lambda h: (h, 0, 0))],
        out_specs=pl.BlockSpec((1, BAND_TQ, BAND_KW), lambda h: (h, 0, 0)),
        out_shape=jax.ShapeDtypeStruct((B_HEADS, BAND_TQ, BAND_KW), F32),
        compiler_params=_params("parallel"),
        name="band_bias",
    )(u)


def _pick_halves(o, t):
    lane = lax.broadcasted_iota(jnp.int32, (t, LANES), 1)
    return jnp.where(lane < 64, o[:t], o[t:])


def _band_prompt_kernel(q_ref, k_ref, v_ref, bias_ref, o_ref):
    ti = pl.program_id(2)
    t = BAND_TQ
    off = pl.multiple_of(ti * t, t)
    kw = k_ref[0, pl.ds(off, BAND_KW), :]
    vx = jnp.concatenate([v_ref[0, pl.ds(off, BAND_KW), :], jnp.ones((BAND_KW, LANES), BF16)], axis=1)
    qs = _stack_halves(q_ref[0])
    valid = lax.broadcasted_iota(jnp.int32, (BAND_ROWS, BAND_KW), 1) >= B_WINDOW - off
    out = []
    for g in range(2 * t // BAND_ROWS):
        rows = slice(g * BAND_ROWS, (g + 1) * BAND_ROWS)
        s = _dot_nt(qs[rows], kw) + bias_ref[g // (t // BAND_ROWS), rows.start % t:rows.start % t + BAND_ROWS, :]
        s = jnp.where(valid, s, NEG)
        m = jnp.max(s, axis=-1, keepdims=True)
        o = _dot(jnp.exp2((s - m).astype(BF16)), vx)
        out.append(o[:, :LANES] * (1.0 / o[:, LANES:]))
    o_ref[0] = _pick_halves(jnp.concatenate(out, axis=0), t).astype(BF16)


def _band_prompt(q, kpad, vpad, bias):
    b, s, d = q.shape
    t = BAND_TQ
    nhp = d // LANES
    win = pl.BlockSpec((1, s + B_WINDOW, LANES), lambda hp, bi, ti: (bi, 0, hp))
    tile = pl.BlockSpec((1, t, LANES), lambda hp, bi, ti: (bi, ti, hp))
    return pl.pallas_call(
        _band_prompt_kernel,
        grid=(nhp, b, s // t),
        in_specs=[tile, win, win, pl.BlockSpec((2, t, BAND_KW), lambda hp, bi, ti: (hp, 0, 0))],
        out_specs=tile,
        out_shape=jax.ShapeDtypeStruct((b, s, d), BF16),
        compiler_params=_params("parallel", "parallel", "arbitrary"),
        name="band_attn_prompt",
    )(q, kpad, vpad, bias)


def _band_sample_kernel(q_ref, kc_ref, kn_ref, vc_ref, vn_ref, bias_ref, o_ref):
    t = q_ref.shape[1]
    wb = kc_ref.shape[1]
    qs = _stack_halves(q_ref[0])
    kc = kc_ref[0].astype(BF16)
    vc = vc_ref[0].astype(BF16)
    kn = _pad_rows(kn_ref[0].astype(BF16), LANES)
    vn = _pad_rows(vn_ref[0].astype(BF16), LANES)
    bias = bias_ref[...].reshape(2 * t, BAND_KW)
    sc = _dot_nt(qs, kc) + bias[:, :wb]
    sn = _dot_nt(qs, kn) + bias[:, wb:wb + LANES]
    col = lax.broadcasted_iota(jnp.int32, sn.shape, 1)
    sn = jnp.where(col < t, sn, NEG)
    m = jnp.maximum(jnp.max(sc, axis=-1, keepdims=True), jnp.max(sn, axis=-1, keepdims=True))
    pc = jnp.exp2(sc - m)
    pn = jnp.exp2(sn - m)
    l = jnp.sum(pc, axis=-1, keepdims=True) + jnp.sum(pn, axis=-1, keepdims=True)
    o = (_dot(pc.astype(BF16), vc) + _dot(pn.astype(BF16), vn)) * (1.0 / l)
    o_ref[0] = _pick_halves(o, t).astype(BF16)


def _band_sample(q, kc, kn, vc, vn, bias):
    b, t, d = q.shape
    wb = kc.shape[1]
    assert wb == B_WINDOW
    new = pl.BlockSpec((1, t, LANES), lambda bi, hp: (bi, 0, hp))
    old = pl.BlockSpec((1, wb, LANES), lambda bi, hp: (bi, 0, hp))
    return pl.pallas_call(
        _band_sample_kernel,
        grid=(b, d // LANES),
        in_specs=[new, old, new, old, new, pl.BlockSpec((2, t, BAND_KW), lambda bi, hp: (hp, 0, 0))],
        out_specs=new,
        out_shape=jax.ShapeDtypeStruct((b, t, d), BF16),
        compiler_params=_params("parallel", "parallel"),
        name="band_attn_sample",
    )(q, kc, kn, vc, vn, bias)


def kernel(x_prompt, x_sample, cache_k_a, cache_v_a, cache_k_b, cache_v_b, g_attn, g_ffn, w_q_a, w_k_a, w_v_a, lam_q1, lam_k1, lam_q2, lam_k2, g_sub_a, w_o_a, g_kv, w_kv_b, w_q_b, rel_bias_b, w_o_b, w_router_g, b_router_g, w_router_e, b_router_e, w_gate, w_up, w_down, g_final):
    bp, sp, d = x_prompt.shape
    bs, ts, _ = x_sample.shape
    past = cache_k_a.shape[2]
    wb = cache_k_b.shape[1]
    n_p, n_s = bp * sp, bs * ts
    assert d == D_MODEL and sp % ROW_TILE == 0 and ROW_TILE % ts == 0 and ts <= LANES
    depth = g_attn.shape[0]
    n_a = w_q_a.shape[0]
    assert depth == 2 and n_a == 1 and w_q_b.shape[0] == 1

    row = lambda v: v.reshape(1, -1).astype(F32)
    bf = lambda w: w.astype(BF16)

    def moe_weights(l):
        wr = jnp.concatenate([w_router_g[l], w_router_e[l].transpose(1, 0, 2).reshape(d, N_EXPERTS)], axis=1)
        wr = jnp.pad(wr, ((0, 0), (0, LANES - wr.shape[1])))
        br = jnp.concatenate([b_router_g[l], b_router_e[l].reshape(-1)])
        br = jnp.pad(br, (0, LANES - br.shape[0]))[None, :]
        by_group = lambda w: bf(w).reshape(N_GROUPS, EXPERTS_PER_GROUP, d, EXPERT_FF).transpose(0, 2, 1, 3).reshape(
            N_GROUPS, d, EXPERTS_PER_GROUP * EXPERT_FF)
        wgu = jnp.concatenate([by_group(w_gate[l]), by_group(w_up[l])], axis=-1)
        wd = bf(w_down[l]).reshape(N_GROUPS, EXPERTS_PER_GROUP * EXPERT_FF, d)
        return row(g_ffn[l]), wr.astype(F32), br.astype(F32), wgu, wd

    xp = x_prompt.reshape(n_p, d)
    xs = x_sample.reshape(n_s, d)

    lam_init = 0.8 - 0.6 * math.exp(-0.3 * 0)
    lamv = jnp.stack([lam_q1[0], lam_k1[0], lam_q2[0], lam_k2[0]]).astype(F32)
    gsub = row(g_sub_a[0])
    wq, wk, wv, wo = bf(w_q_a[0]), bf(w_k_a[0]), bf(w_v_a[0]), bf(w_o_a[0])
    g0 = row(g_attn[0])

    cos_p, sin_p = _rope_table(sp, sp, 0, min(ROW_TILE, sp))
    qp, kfp, kbp, vfp, vbp = _qkv_a(x_prompt, g0, wq, wk, wv, cos_p, sin_p, True, A_DK ** -0.5 * LOG2E)
    ap = _diff_prompt(lamv, gsub, qp, kbp, vbp, lam_init)
    xp = _out_proj(ap.reshape(n_p, d), xp, wo)

    tm_s = min(ROW_TILE, n_s)
    cos_s, sin_s = _rope_table(tm_s, ts, past, tm_s)
    qs, kfs, _, vfs, _ = _qkv_a(xs.reshape(1, n_s, d), g0, wq, wk, wv, cos_s, sin_s, False, A_DK ** -0.5)
    kfs3, vfs3 = kfs.reshape(bs, ts, d), vfs.reshape(bs, ts, d)
    a_s = _diff_sample(lamv, gsub, qs.reshape(bs, ts, d), cache_k_a[0].reshape(bs, past, d), bf(kfs3),
                       cache_v_a[0].reshape(bs, past, d), bf(vfs3), lam_init)
    xs = _out_proj(a_s.reshape(n_s, d), xs, wo)

    mw = moe_weights(0)
    gfin = row(g_final)
    xp = _moe(xp, *mw, gfin, final_norm=False)
    xs = _moe(xs, *mw, gfin, final_norm=False)

    gkv, g1 = row(g_kv), row(g_attn[1])
    wkv, wqb, wob = bf(w_kv_b), bf(w_q_b[0]), bf(w_o_b[0])
    bias = _band_bias(rel_bias_b[0].astype(F32))

    xp3 = xp.reshape(bp, sp, d)
    kpad, vpad, qb = _kvq_b(xp3, gkv, g1, wkv, wqb, BF16, lead_pad=True)
    kbt, vbt, _ = _kvq_b(xp3, gkv, g1, wkv, wqb, F32, lead_pad=False, row_block=sp // ROW_TILE - 1)
    ab = _band_prompt(qb, kpad, vpad, bias)
    xp = _out_proj(ab.reshape(n_p, d), xp, wob)

    kns, vns, qbs = _kvq_b(xs.reshape(1, n_s, d), gkv, g1, wkv, wqb, F32, lead_pad=False)
    kns3, vns3 = kns.reshape(bs, ts, d), vns.reshape(bs, ts, d)
    abs_ = _band_sample(qbs.reshape(bs, ts, d), cache_k_b.reshape(bs, wb, d), kns3,
                        cache_v_b.reshape(bs, wb, d), vns3, bias)
    xs = _out_proj(abs_.reshape(n_s, d), xs, wob)

    mw = moe_weights(1)
    y_prompt = _moe(xp, *mw, gfin, final_norm=True).reshape(bp, sp, d)
    y_sample = _moe(xs, *mw, gfin, final_norm=True).reshape(bs, ts, d)

    wp = min(B_WINDOW, sp)
    assert wp == ROW_TILE
    k_a_prompt = kfp.reshape(1, bp, sp, 2 * A_HEADS, A_DK)
    v_a_prompt = vfp.reshape(1, bp, sp, A_HEADS, A_DV)
    k_a_sample = kfs.reshape(1, bs, ts, 2 * A_HEADS, A_DK)
    v_a_sample = vfs.reshape(1, bs, ts, A_HEADS, A_DV)
    k_b_prompt = kbt.reshape(bp, wp, B_HEADS, B_DH)
    v_b_prompt = vbt.reshape(bp, wp, B_HEADS, B_DH)
    k_b_sample = jnp.concatenate([cache_k_b.reshape(bs, wb, d)[:, ts:], kns3], axis=1).reshape(bs, wb, B_HEADS, B_DH)
    v_b_sample = jnp.concatenate([cache_v_b.reshape(bs, wb, d)[:, ts:], vns3], axis=1).reshape(bs, wb, B_HEADS, B_DH)
    return (y_prompt, y_sample, k_a_prompt, v_a_prompt, k_b_prompt, v_b_prompt,
            k_a_sample, v_a_sample, k_b_sample, v_b_sample)
```

```python
import functools
import math

import jax
import jax.numpy as jnp
from jax import lax
from jax.experimental import pallas as pl
from jax.experimental.pallas import tpu as pltpu

F32 = jnp.float32
BF16 = jnp.bfloat16

EPS = 1e-6
CHUNK = 64
D_MODEL = 1024
A_HEADS = 8
A_DK = 64
A_DV = 128
B_HEADS = 16
B_DH = 64
B_WINDOW = 512
REL_CLIP = 256
N_GROUPS = 4
EXPERTS_PER_GROUP = 4
N_EXPERTS = 16
EXPERT_FF = 512
ROPE_THETA = 10000.0

LANES = 128
NEG = -1e30
LOG2E = math.log2(math.e)
VMEM_LIMIT = 56 * 1024 * 1024

ROW_TILE = 512
MOE_TILE = 1024
MOE_CHUNK = 128
HEADS_PER_STEP = 4
ONES_ROWS = 16
BAND_TQ = 256
BAND_KW = BAND_TQ + B_WINDOW
BAND_ROWS = 128


def _params(*sem):
    return pltpu.CompilerParams(dimension_semantics=sem, vmem_limit_bytes=VMEM_LIMIT)


def _rms_scale(x):
    return x * lax.rsqrt(jnp.mean(x * x, axis=-1, keepdims=True) + EPS)


def _dot(a, b):
    return jnp.dot(a, b, preferred_element_type=F32)


def _dot_nt(a, b):
    return lax.dot_general(a, b, (((1,), (1,)), ((), ())), preferred_element_type=F32)


def _chunk_of(pos):
    return jnp.right_shift(pos, CHUNK.bit_length() - 1)


def _stack_halves(q):
    lane = lax.broadcasted_iota(jnp.int32, q.shape, 1)
    zero = jnp.zeros_like(q)
    return jnp.concatenate([jnp.where(lane < 64, q, zero), jnp.where(lane >= 64, q, zero)], axis=0)


def _rope_table_kernel(inv_ref, cos_ref, sin_ref, *, period, offset, rows):
    i = pl.program_id(0)
    row = lax.broadcasted_iota(jnp.int32, (rows, LANES), 0) + i * rows
    lane = lax.broadcasted_iota(jnp.int32, (rows, LANES), 1)
    pos = (offset + lax.rem(row, period)).astype(F32)
    ang = pos * inv_ref[...]
    s = jnp.sin(ang)
    cos_ref[...] = jnp.cos(ang)
    sin_ref[...] = jnp.where(lax.rem(lane, 64) < 32, -s, s)


def _rope_table(n_rows, period, offset, rows):
    half = A_DK // 2
    inv = jnp.power(ROPE_THETA, -jnp.arange(half, dtype=F32) / half)
    inv = jnp.tile(inv, LANES // half)[None, :]
    return pl.pallas_call(
        functools.partial(_rope_table_kernel, period=period, offset=offset, rows=rows),
        grid=(n_rows // rows,),
        in_specs=[pl.BlockSpec((1, LANES), lambda i: (0, 0))],
        out_specs=[pl.BlockSpec((rows, LANES), lambda i: (i, 0))] * 2,
        out_shape=[jax.ShapeDtypeStruct((n_rows, LANES), F32)] * 2,
        compiler_params=_params("arbitrary"),
        name="rope_table",
    )(inv)


def _qkv_a_kernel(x_ref, g_ref, wq_ref, wk_ref, wv_ref, cos_ref, sin_ref,
                  q_ref, kf_ref, kb_ref, vf_ref, vb_ref, *, transposed, q_scale):
    h = (_rms_scale(x_ref[0]) * g_ref[...]).astype(BF16)
    cos = cos_ref[...]
    sin = sin_ref[...]
    lane = lax.broadcasted_iota(jnp.int32, cos.shape, 1)
    first = lax.rem(lane, 64) < 32

    yq = _dot(h, wq_ref[...])
    yk = _dot(h, wk_ref[...])
    yv = _dot(h, wv_ref[...])
    for c in range(D_MODEL // LANES):
        sl = slice(c * LANES, (c + 1) * LANES)

        def rope(y):
            yc = y[:, sl]
            rot = jnp.where(first, pltpu.roll(yc, LANES - 32, 1), pltpu.roll(yc, 32, 1))
            return yc * cos + rot * sin

        rq = rope(yq) * q_scale
        rk = rope(yk)
        vc = yv[:, sl]
        kf_ref[0, :, sl] = rk
        vf_ref[0, :, sl] = vc
        if transposed:
            q_ref[0, c] = rq.T.astype(BF16)
            kb_ref[0, c] = rk.astype(BF16)
            vb_ref[0, c, 0, :A_DV] = vc.T.astype(BF16)
            vb_ref[0, c, 0, A_DV:] = jnp.ones((ONES_ROWS, vc.shape[0]), BF16)
        else:
            q_ref[0, :, sl] = rq.astype(BF16)
            kb_ref[0, :, sl] = rk.astype(BF16)
            vb_ref[0, :, sl] = vc.astype(BF16)


def _qkv_a(x3, g, wq, wk, wv, cos, sin, transposed, q_scale):
    b, s, d = x3.shape
    tm = min(ROW_TILE, s)
    n_tab = cos.shape[0] // tm
    nh = d // LANES
    tok = pl.BlockSpec((1, tm, d), lambda bi, i: (bi, i, 0))
    fshape = jax.ShapeDtypeStruct((b, s, d), F32)
    if transposed:
        hspecs = [pl.BlockSpec((1, nh, LANES, tm), lambda bi, i: (bi, 0, 0, i)),
                  pl.BlockSpec((1, nh, tm, LANES), lambda bi, i: (bi, 0, i, 0)),
                  pl.BlockSpec((1, nh, 1, A_DV + ONES_ROWS, tm), lambda bi, i: (bi, 0, i, 0, 0))]
        hshapes = [jax.ShapeDtypeStruct((b, nh, LANES, s), BF16),
                   jax.ShapeDtypeStruct((b, nh, s, LANES), BF16),
                   jax.ShapeDtypeStruct((b, nh, s // tm, A_DV + ONES_ROWS, tm), BF16)]
    else:
        hspecs = [tok] * 3
        hshapes = [jax.ShapeDtypeStruct((b, s, d), BF16)] * 3
    wspec = pl.BlockSpec((d, d), lambda bi, i: (0, 0))
    tab = pl.BlockSpec((tm, LANES), lambda bi, i: (i % n_tab, 0))
    return pl.pallas_call(
        functools.partial(_qkv_a_kernel, transposed=transposed, q_scale=q_scale),
        grid=(b, s // tm),
        in_specs=[tok, pl.BlockSpec((1, d), lambda bi, i: (0, 0)), wspec, wspec, wspec, tab, tab],
        out_specs=[hspecs[0], tok, hspecs[1], tok, hspecs[2]],
        out_shape=[hshapes[0], fshape, hshapes[1], fshape, hshapes[2]],
        compiler_params=_params("parallel", "parallel"),
        name="qkv_a",
    )(x3, g, wq, wk, wv, cos, sin)


def _diff_lambda(lam_ref, lam_init):
    lv = lam_ref[...]
    s1 = jnp.sum(lv[0:1] * lv[1:2], axis=-1, keepdims=True)
    s2 = jnp.sum(lv[2:3] * lv[3:4], axis=-1, keepdims=True)
    return jnp.exp(s1) - jnp.exp(s2) + lam_init


def _diff_finish(o1, o2, lam, gsub, lam_init):
    d = o1 - lam * o2
    return (_rms_scale(d) * gsub * (1.0 - lam_init)).astype(BF16)


def _diff_prompt_kernel(lam_ref, gsub_ref, qt_ref, k_ref, vt_ref, o_ref, s_ref, p_ref, *, t, lam_init):
    qi = pl.program_id(2)
    nh = qt_ref.shape[1]
    row = lax.broadcasted_iota(jnp.int32, (A_DV, t), 0)
    zero = jnp.zeros((A_DV, t), BF16)
    qst = [jnp.concatenate([jnp.where(row < A_DK, qt_ref[0, h], zero), jnp.where(row >= A_DK, qt_ref[0, h], zero)],
                           axis=1) for h in range(nh)]

    def scores(h, tile, m_prev, mask):
        s = _dot(k_ref[0, h, pl.ds(pl.multiple_of(tile * t, t), t), :], qst[h])
        if mask is not None:
            s = jnp.where(mask, s, NEG)
        m = jnp.maximum(m_prev, jnp.max(s, axis=0, keepdims=True))
        s_ref[h] = s
        return m, jnp.exp2(m_prev - m)

    def accumulate(h, tile, alpha, acc):
        return alpha * acc + _dot(vt_ref[0, h, tile], p_ref[h])

    def exponentials(h, m):
        p_ref[h] = jnp.exp2((s_ref[h] - m).astype(BF16))

    kk = lax.broadcasted_iota(jnp.int32, (t, 2 * t), 0)
    qq = lax.broadcasted_iota(jnp.int32, (t, 2 * t), 1)
    mask = _chunk_of(kk) <= _chunk_of(jnp.bitwise_and(qq, t - 1))
    m0 = jnp.full((1, 2 * t), NEG, F32)
    carry = []
    for h in range(nh):
        m, alpha = scores(h, qi, m0, mask)
        p_ref[h] = jnp.zeros((t, 2 * t), BF16)
        carry.append((m, alpha, jnp.ones((1, 2 * t), F32), jnp.zeros((A_DV + ONES_ROWS, 2 * t), F32)))

    def v_tile(n):
        return jnp.where(n == 0, qi, jnp.maximum(n - 1, 0))

    def body(j, carry):
        out = []
        for h in range(nh):
            m_s, alpha_s, alpha_p, acc = carry[h]
            acc = accumulate(h, v_tile(j - 1), alpha_p, acc)
            exponentials(h, m_s)
            m_new, alpha_new = scores(h, j, m_s, None)
            out.append((m_new, alpha_new, alpha_s, acc))
        return tuple(out)

    carry = lax.fori_loop(0, qi, body, tuple(carry))
    lam = _diff_lambda(lam_ref, lam_init)
    for h in range(nh):
        m_s, alpha_s, alpha_p, acc = carry[h]
        acc = accumulate(h, v_tile(qi - 1), alpha_p, acc)
        exponentials(h, m_s)
        acc = accumulate(h, v_tile(qi), alpha_s, acc)
        o = (acc[:A_DV] * (1.0 / acc[A_DV:A_DV + 1])).T
        o_ref[0, :, h * A_DV:(h + 1) * A_DV] = _diff_finish(o[:t], o[t:], lam, gsub_ref[...], lam_init)


def _diff_prompt(lamv, gsub, qt, k, vt, lam_init):
    b, nh, _, s = qt.shape
    t = vt.shape[-1]
    hs = HEADS_PER_STEP
    return pl.pallas_call(
        functools.partial(_diff_prompt_kernel, t=t, lam_init=lam_init),
        grid=(b, nh // hs, s // t),
        in_specs=[pl.BlockSpec(lamv.shape, lambda bi, hi, qi: (0, 0)),
                  pl.BlockSpec((1, A_DV), lambda bi, hi, qi: (0, 0)),
                  pl.BlockSpec((1, hs, LANES, t), lambda bi, hi, qi: (bi, hi, 0, qi)),
                  pl.BlockSpec((1, hs, s, LANES), lambda bi, hi, qi: (bi, hi, 0, 0), pipeline_mode=pl.Buffered(1)),
                  pl.BlockSpec((1, hs, s // t, A_DV + ONES_ROWS, t), lambda bi, hi, qi: (bi, hi, 0, 0, 0),
                               pipeline_mode=pl.Buffered(1))],
        out_specs=pl.BlockSpec((1, t, hs * LANES), lambda bi, hi, qi: (bi, qi, hi)),
        out_shape=jax.ShapeDtypeStruct((b, s, nh * LANES), BF16),
        scratch_shapes=[pltpu.VMEM((hs, t, 2 * t), F32), pltpu.VMEM((hs, t, 2 * t), BF16)],
        compiler_params=_params("parallel", "parallel", "arbitrary"),
        name="diff_attn_prompt",
    )(lamv, gsub, qt, k, vt)


def _pad_rows(x, rows):
    return jnp.concatenate([x, jnp.zeros((rows - x.shape[0], x.shape[1]), x.dtype)], axis=0)


def _diff_sample_kernel(lam_ref, gsub_ref, q_ref, kc_ref, kn_ref, vc_ref, vn_ref, o_ref, *, lam_init):
    t = q_ref.shape[1]
    qs = _stack_halves(q_ref[0])
    kc = kc_ref[0].astype(BF16)
    vc = vc_ref[0].astype(BF16)
    kn = _pad_rows(kn_ref[0], LANES)
    vn = _pad_rows(vn_ref[0], LANES)
    sc = _dot_nt(qs, kc)
    sn = _dot_nt(qs, kn)
    col = lax.broadcasted_iota(jnp.int32, sn.shape, 1)
    sn = jnp.where(col < t, sn, NEG)
    m = jnp.maximum(jnp.max(sc, axis=-1, keepdims=True), jnp.max(sn, axis=-1, keepdims=True))
    pc = jnp.exp(sc - m)
    pn = jnp.exp(sn - m)
    l = jnp.sum(pc, axis=-1, keepdims=True) + jnp.sum(pn, axis=-1, keepdims=True)
    acc = _dot(pc.astype(BF16), vc) + _dot(pn.astype(BF16), vn)
    o = acc * (1.0 / l)
    o_ref[0] = _diff_finish(o[:t], o[t:], _diff_lambda(lam_ref, lam_init), gsub_ref[...], lam_init)


def _diff_sample(lamv, gsub, q, kc, kn, vc, vn, lam_init):
    b, t, d = q.shape
    past = kc.shape[1]
    new = pl.BlockSpec((1, t, LANES), lambda bi, hi: (bi, 0, hi))
    old = pl.BlockSpec((1, past, LANES), lambda bi, hi: (bi, 0, hi))
    return pl.pallas_call(
        functools.partial(_diff_sample_kernel, lam_init=lam_init),
        grid=(b, d // LANES),
        in_specs=[pl.BlockSpec(lamv.shape, lambda bi, hi: (0, 0)),
                  pl.BlockSpec((1, A_DV), lambda bi, hi: (0, 0)), new, old, new, old, new],
        out_specs=new,
        out_shape=jax.ShapeDtypeStruct((b, t, d), BF16),
        compiler_params=_params("parallel", "parallel"),
        name="diff_attn_sample",
    )(lamv, gsub, q, kc, kn, vc, vn)


def _out_proj_kernel(a_ref, x_ref, w_ref, o_ref):
    o_ref[...] = x_ref[...] + _dot(a_ref[...], w_ref[...])


def _out_proj(a, x, w):
    n, d = x.shape
    tm = min(ROW_TILE, n)
    tok = pl.BlockSpec((tm, d), lambda i: (i, 0))
    return pl.pallas_call(
        _out_proj_kernel,
        grid=(n // tm,),
        in_specs=[tok, tok, pl.BlockSpec((d, d), lambda i: (0, 0))],
        out_specs=tok,
        out_shape=jax.ShapeDtypeStruct((n, d), F32),
        compiler_params=_params("parallel"),
        name="out_proj",
    )(a, x, w)


def _moe_gates(hf, wr_ref, br_ref):
    lg = jnp.dot(hf, wr_ref[...], preferred_element_type=F32, precision=lax.Precision.HIGHEST) + br_ref[...]
    lane = lax.broadcasted_iota(jnp.int32, lg.shape, 1).astype(F32)
    big = float(LANES)

    def first_argmax(v):
        mx = jnp.max(v, axis=-1, keepdims=True)
        return mx, jnp.min(jnp.where(v == mx, lane, big), axis=-1, keepdims=True)

    is_g = lane < N_GROUPS
    gmax, gidx = first_argmax(jnp.where(is_g, lg, NEG))
    gval = 1.0 / jnp.sum(jnp.where(is_g, jnp.exp(lg - gmax), 0.0), axis=-1, keepdims=True)
    lo = N_GROUPS + EXPERTS_PER_GROUP * gidx
    el = jnp.where((lane >= lo) & (lane < lo + EXPERTS_PER_GROUP), lg, NEG)
    v1, i1 = first_argmax(el)
    v2, i2 = first_argmax(jnp.where(lane == i1, NEG, el))
    e2 = jnp.exp(v2 - v1)
    inv = gval / (1.0 + e2)
    gates = jnp.where(lane == i1 - lo, inv, 0.0) + jnp.where(lane == i2 - lo, inv * e2, 0.0)
    return jnp.where(lane == gidx, 1.0, 0.0), gates


def _moe_kernel(x_ref, g_ref, wr_ref, br_ref, tri_ref, wgu_ref, wd_ref, gf_ref, o_ref,
                h_ref, gsplit_ref, rank_ref, rank_t_ref, pos_ref, y_ref, base_ref, *, final_norm):
    grp = pl.program_id(1)
    nb = x_ref.shape[0]
    ff = EXPERTS_PER_GROUP * EXPERT_FF

    @pl.when(grp == 0)
    def _():
        hf = _rms_scale(x_ref[...]) * g_ref[...]
        h_ref[...] = hf.astype(BF16)
        member, gates = _moe_gates(hf, wr_ref, br_ref)
        rest = gates
        split = jnp.zeros_like(gates)
        for i in range(3):
            part = rest.astype(BF16).astype(F32)
            split = split + pltpu.roll(part, EXPERTS_PER_GROUP * i, 1)
            rest = rest - part
        gsplit_ref[...] = split.astype(BF16)
        rank = jnp.where(member > 0.0, _dot(tri_ref[...], member.astype(BF16)) + 1.0, 0.0)
        rank_ref[...] = rank
        rank_t_ref[...] = rank.T
        pos_ref[...] = jnp.full(pos_ref.shape, -1.0, F32)
        y_ref[...] = jnp.zeros_like(y_ref)
        base_ref[0] = 0

    base = base_ref[0]
    rrow = rank_t_ref[pl.ds(grp, 1), :]
    n_chunks = (jnp.max(rrow).astype(jnp.int32) + MOE_CHUNK - 1) // MOE_CHUNK
    lane = lax.broadcasted_iota(jnp.int32, (nb, LANES), 1)
    rcol = jnp.sum(jnp.where(lane == grp, rank_ref[...], 0.0), axis=-1, keepdims=True)
    pos_ref[...] = jnp.where(rcol > 0.0, base.astype(F32) + rcol - 1.0, pos_ref[...])

    def chunk(c, carry):
        want = (lax.broadcasted_iota(jnp.int32, (MOE_CHUNK, nb), 0) + (c * MOE_CHUNK + 1)).astype(F32)
        onehot = jnp.where(rrow == want, 1.0, 0.0).astype(BF16)
        xg = _dot(onehot, h_ref[...]).astype(BF16)
        gs3 = _dot(onehot, gsplit_ref[...])
        gs = (gs3 + pltpu.roll(gs3, LANES - EXPERTS_PER_GROUP, 1)
              + pltpu.roll(gs3, LANES - 2 * EXPERTS_PER_GROUP, 1))
        gu = _dot(xg, wgu_ref[0])
        parts = []
        for j in range(EXPERTS_PER_GROUP):
            a = gu[:, j * EXPERT_FF:(j + 1) * EXPERT_FF]
            u = gu[:, ff + j * EXPERT_FF:ff + (j + 1) * EXPERT_FF]
            parts.append((a * (1.0 / (1.0 + jnp.exp(-a))) * u * gs[:, j:j + 1]).astype(BF16))
        y = _dot(jnp.concatenate(parts, axis=1), wd_ref[0])
        off = pl.multiple_of(base + c * MOE_CHUNK, MOE_CHUNK)
        y_ref[pl.ds(off, MOE_CHUNK), :] = y.astype(BF16)
        return carry

    lax.fori_loop(0, n_chunks, chunk, 0)
    base_ref[0] = base + n_chunks * MOE_CHUNK

    @pl.when(grp == N_GROUPS - 1)
    def _():
        slot = lax.broadcasted_iota(jnp.int32, (nb, y_ref.shape[0]), 1).astype(F32)
        scatter = jnp.where(pos_ref[...] == slot, 1.0, 0.0).astype(BF16)
        out = x_ref[...] + _dot(scatter, y_ref[...])
        if final_norm:
            out = _rms_scale(out) * gf_ref[...]
        o_ref[...] = out


def _moe(x, g, wr, br, wgu, wd, gf, final_norm):
    n, d = x.shape
    tm = min(MOE_TILE, n)
    ff = EXPERTS_PER_GROUP * EXPERT_FF
    slots = tm + N_GROUPS * MOE_CHUNK
    tri = jnp.tri(tm, tm, -1, dtype=BF16)
    once = pl.Buffered(1)
    tok = pl.BlockSpec((tm, d), lambda i, e: (i, 0))
    vec = pl.BlockSpec((1, d), lambda i, e: (0, 0))
    return pl.pallas_call(
        functools.partial(_moe_kernel, final_norm=final_norm),
        grid=(n // tm, N_GROUPS),
        in_specs=[pl.BlockSpec((tm, d), lambda i, e: (i, 0), pipeline_mode=once), vec,
                  pl.BlockSpec((d, LANES), lambda i, e: (0, 0), pipeline_mode=once),
                  pl.BlockSpec((1, LANES), lambda i, e: (0, 0)),
                  pl.BlockSpec((tm, tm), lambda i, e: (0, 0), pipeline_mode=once),
                  pl.BlockSpec((1, d, 2 * ff), lambda i, e: (e, 0, 0)),
                  pl.BlockSpec((1, ff, d), lambda i, e: (e, 0, 0)),
                  vec],
        out_specs=tok,
        out_shape=jax.ShapeDtypeStruct((n, d), F32),
        scratch_shapes=[pltpu.VMEM((tm, d), BF16), pltpu.VMEM((tm, LANES), BF16),
                        pltpu.VMEM((tm, LANES), F32), pltpu.VMEM((LANES, tm), F32),
                        pltpu.VMEM((tm, 1), F32), pltpu.VMEM((slots, d), BF16),
                        pltpu.SMEM((1,), jnp.int32)],
        compiler_params=_params("parallel", "arbitrary"),
        name="hier_moe",
    )(x, g, wr, br, tri, wgu, wd, gf)


def _kvq_b_kernel(x_ref, gkv_ref, gq_ref, wkv_ref, wq_ref, k_ref, v_ref, q_ref, *, lead_pad):
    def compute():
        xh = _rms_scale(x_ref[0])
        kv = _dot((xh * gkv_ref[...]).astype(BF16), wkv_ref[...])
        k_ref[0] = kv[:, :D_MODEL].astype(k_ref.dtype)
        v_ref[0] = kv[:, D_MODEL:].astype(v_ref.dtype)
        q = _dot((xh * gq_ref[...]).astype(BF16), wq_ref[...])
        q_ref[0] = (q * (B_DH ** -0.5 * LOG2E)).astype(BF16)

    if lead_pad:
        i = pl.program_id(1)

        @pl.when(i == 0)
        def _():
            k_ref[...] = jnp.zeros_like(k_ref)
            v_ref[...] = jnp.zeros_like(v_ref)

        pl.when(i > 0)(compute)
    else:
        compute()


def _kvq_b(x3, gkv, gq, wkv, wq, kv_dtype, lead_pad, row_block=None):
    b, s, d = x3.shape
    tm = min(ROW_TILE, s)
    if lead_pad:
        assert tm == B_WINDOW
        grid = (b, s // tm + 1)
        src = lambda bi, i: (bi, jnp.maximum(i - 1, 0), 0)
        dst = lambda bi, i: (bi, i, 0)
        s_kv, s_q = s + B_WINDOW, s
    elif row_block is not None:
        grid = (b, 1)
        src = lambda bi, i: (bi, row_block, 0)
        dst = lambda bi, i: (bi, 0, 0)
        s_kv = s_q = tm
    else:
        grid = (b, s // tm)
        src = dst = lambda bi, i: (bi, i, 0)
        s_kv = s_q = s
    vec = pl.BlockSpec((1, d), lambda bi, i: (0, 0))
    return pl.pallas_call(
        functools.partial(_kvq_b_kernel, lead_pad=lead_pad),
        grid=grid,
        in_specs=[pl.BlockSpec((1, tm, d), src), vec, vec,
                  pl.BlockSpec((d, 2 * d), lambda bi, i: (0, 0)),
                  pl.BlockSpec((d, d), lambda bi, i: (0, 0))],
        out_specs=[pl.BlockSpec((1, tm, d), dst), pl.BlockSpec((1, tm, d), dst),
                   pl.BlockSpec((1, tm, d), src if lead_pad else dst)],
        out_shape=[jax.ShapeDtypeStruct((b, s_kv, d), kv_dtype), jax.ShapeDtypeStruct((b, s_kv, d), kv_dtype),
                   jax.ShapeDtypeStruct((b, s_q, d), BF16)],
        compiler_params=_params("parallel", "arbitrary"),
        name="kvq_b",
    )(x3, gkv, gq, wkv, wq)


def _band_bias_kernel(u_ref, o_ref):
    x = jnp.broadcast_to(u_ref[0], (BAND_TQ, 2 * B_WINDOW))
    bias = pltpu.roll(x, 0, 1, stride=1, stride_axis=0)[:, :BAND_KW]
    qc = _chunk_of(lax.broadcasted_iota(jnp.int32, bias.shape, 0))
    kc = _chunk_of(lax.broadcasted_iota(jnp.int32, bias.shape, 1))
    ok = (kc >= qc) & (kc <= qc + B_WINDOW // CHUNK)
    o_ref[0] = jnp.where(ok, bias * LOG2E, NEG)


def _band_bias(rel_bias):
    edge = jnp.broadcast_to(rel_bias[:, 2 * REL_CLIP:], (B_HEADS, REL_CLIP))
    u = jnp.concatenate([edge, rel_bias[:, :0:-1], edge], axis=1)[:, None, :]
    assert u.shape[-1] == 2 * B_WINDOW
    return pl.pallas_call(
        _band_bias_kernel,
        grid=(B_HEADS,),
        in_specs=[pl.BlockSpec((1, 1, 2 * B_WINDOW), lambda h: (h, 0, 0))],
        out_specs=pl.BlockSpec((1, BAND_TQ, BAND_KW), lambda h: (h, 0, 0)),
        out_shape=jax.ShapeDtypeStruct((B_HEADS, BAND_TQ, BAND_KW), F32),
        compiler_params=_params("parallel"),
        name="band_bias",
    )(u)


def _pick_halves(o, t):
    lane = lax.broadcasted_iota(jnp.int32, (t, LANES), 1)
    return jnp.where(lane < 64, o[:t], o[t:])


def _band_prompt_kernel(q_ref, k_ref, v_ref, bias_ref, o_ref, p_ref, *, n_tiles):
    t = BAND_TQ

    def probabilities(off, lead):
        kw = k_ref[0, pl.ds(off, BAND_KW), :]
        qs = _stack_halves(q_ref[0, pl.ds(off, t), :])
        for g in range(2 * t // BAND_ROWS):
            r0 = g * BAND_ROWS
            s = _dot_nt(qs[r0:r0 + BAND_ROWS], kw) + bias_ref[r0 // t, r0 % t:r0 % t + BAND_ROWS, :]
            if lead:
                c = lax.broadcasted_iota(jnp.int32, s.shape, 1)
                s = jnp.where(c >= B_WINDOW - off, s, NEG)
            m = jnp.max(s, axis=-1, keepdims=True)
            p_ref[r0:r0 + BAND_ROWS, :] = jnp.exp2((s - m).astype(BF16))

    def outputs(off):
        vx = jnp.concatenate([v_ref[0, pl.ds(off, BAND_KW), :], jnp.ones((BAND_KW, LANES), BF16)], axis=1)
        o = _dot(p_ref[...], vx)
        o = o[:, :LANES] * (1.0 / o[:, LANES:])
        o_ref[0, pl.ds(off, t), :] = _pick_halves(o, t).astype(BF16)

    n_lead = B_WINDOW // t
    for ti in range(n_lead):
        if ti > 0:
            outputs((ti - 1) * t)
        probabilities(ti * t, True)

    def body(ti, carry):
        outputs(pl.multiple_of((ti - 1) * t, t))
        probabilities(pl.multiple_of(ti * t, t), False)
        return carry

    lax.fori_loop(n_lead, n_tiles, body, 0)
    outputs((n_tiles - 1) * t)


def _band_prompt(q, kpad, vpad, bias):
    b, s, d = q.shape
    t = BAND_TQ
    nhp = d // LANES
    win = pl.BlockSpec((1, s + B_WINDOW, LANES), lambda hp, bi: (bi, 0, hp))
    seq = pl.BlockSpec((1, s, LANES), lambda hp, bi: (bi, 0, hp))
    return pl.pallas_call(
        functools.partial(_band_prompt_kernel, n_tiles=s // t),
        grid=(nhp, b),
        in_specs=[seq, win, win, pl.BlockSpec((2, t, BAND_KW), lambda hp, bi: (hp, 0, 0))],
        out_specs=seq,
        out_shape=jax.ShapeDtypeStruct((b, s, d), BF16),
        scratch_shapes=[pltpu.VMEM((2 * t, BAND_KW), BF16)],
        compiler_params=_params("parallel", "parallel"),
        name="band_attn_prompt",
    )(q, kpad, vpad, bias)


def _band_sample_kernel(q_ref, kc_ref, kn_ref, vc_ref, vn_ref, bias_ref, o_ref):
    t = q_ref.shape[1]
    wb = kc_ref.shape[1]
    qs = _stack_halves(q_ref[0])
    kc = kc_ref[0].astype(BF16)
    vc = vc_ref[0].astype(BF16)
    kn = _pad_rows(kn_ref[0].astype(BF16), LANES)
    vn = _pad_rows(vn_ref[0].astype(BF16), LANES)
    bias = bias_ref[...].reshape(2 * t, BAND_KW)
    sc = _dot_nt(qs, kc) + bias[:, :wb]
    sn = _dot_nt(qs, kn) + bias[:, wb:wb + LANES]
    col = lax.broadcasted_iota(jnp.int32, sn.shape, 1)
    sn = jnp.where(col < t, sn, NEG)
    m = jnp.maximum(jnp.max(sc, axis=-1, keepdims=True), jnp.max(sn, axis=-1, keepdims=True))
    pc = jnp.exp2(sc - m)
    pn = jnp.exp2(sn - m)
    l = jnp.sum(pc, axis=-1, keepdims=True) + jnp.sum(pn, axis=-1, keepdims=True)
    o = (_dot(pc.astype(BF16), vc) + _dot(pn.astype(BF16), vn)) * (1.0 / l)
    o_ref[0] = _pick_halves(o, t).astype(BF16)


def _band_sample(q, kc, kn, vc, vn, bias):
    b, t, d = q.shape
    wb = kc.shape[1]
    assert wb == B_WINDOW
    new = pl.BlockSpec((1, t, LANES), lambda bi, hp: (bi, 0, hp))
    old = pl.BlockSpec((1, wb, LANES), lambda bi, hp: (bi, 0, hp))
    return pl.pallas_call(
        _band_sample_kernel,
        grid=(b, d // LANES),
        in_specs=[new, old, new, old, new, pl.BlockSpec((2, t, BAND_KW), lambda bi, hp: (hp, 0, 0))],
        out_specs=new,
        out_shape=jax.ShapeDtypeStruct((b, t, d), BF16),
        compiler_params=_params("parallel", "parallel"),
        name="band_attn_sample",
    )(q, kc, kn, vc, vn, bias)


def kernel(x_prompt, x_sample, cache_k_a, cache_v_a, cache_k_b, cache_v_b, g_attn, g_ffn, w_q_a, w_k_a, w_v_a, lam_q1, lam_k1, lam_q2, lam_k2, g_sub_a, w_o_a, g_kv, w_kv_b, w_q_b, rel_bias_b, w_o_b, w_router_g, b_router_g, w_router_e, b_router_e, w_gate, w_up, w_down, g_final):
    bp, sp, d = x_prompt.shape
    bs, ts, _ = x_sample.shape
    past = cache_k_a.shape[2]
    wb = cache_k_b.shape[1]
    n_p, n_s = bp * sp, bs * ts
    assert d == D_MODEL and sp % ROW_TILE == 0 and ROW_TILE % ts == 0 and ts <= LANES
    depth = g_attn.shape[0]
    n_a = w_q_a.shape[0]
    assert depth == 2 and n_a == 1 and w_q_b.shape[0] == 1

    row = lambda v: v.reshape(1, -1).astype(F32)
    bf = lambda w: w.astype(BF16)

    def moe_weights(l):
        wr = jnp.concatenate([w_router_g[l], w_router_e[l].transpose(1, 0, 2).reshape(d, N_EXPERTS)], axis=1)
        wr = jnp.pad(wr, ((0, 0), (0, LANES - wr.shape[1])))
        br = jnp.concatenate([b_router_g[l], b_router_e[l].reshape(-1)])
        br = jnp.pad(br, (0, LANES - br.shape[0]))[None, :]
        by_group = lambda w: bf(w).reshape(N_GROUPS, EXPERTS_PER_GROUP, d, EXPERT_FF).transpose(0, 2, 1, 3).reshape(
            N_GROUPS, d, EXPERTS_PER_GROUP * EXPERT_FF)
        wgu = jnp.concatenate([by_group(w_gate[l]), by_group(w_up[l])], axis=-1)
        wd = bf(w_down[l]).reshape(N_GROUPS, EXPERTS_PER_GROUP * EXPERT_FF, d)
        return row(g_ffn[l]), wr.astype(F32), br.astype(F32), wgu, wd

    xp = x_prompt.reshape(n_p, d)
    xs = x_sample.reshape(n_s, d)

    lam_init = 0.8 - 0.6 * math.exp(-0.3 * 0)
    lamv = jnp.stack([lam_q1[0], lam_k1[0], lam_q2[0], lam_k2[0]]).astype(F32)
    gsub = row(g_sub_a[0])
    wq, wk, wv, wo = bf(w_q_a[0]), bf(w_k_a[0]), bf(w_v_a[0]), bf(w_o_a[0])
    g0 = row(g_attn[0])

    cos_p, sin_p = _rope_table(sp, sp, 0, min(ROW_TILE, sp))
    qp, kfp, kbp, vfp, vbp = _qkv_a(x_prompt, g0, wq, wk, wv, cos_p, sin_p, True, A_DK ** -0.5 * LOG2E)
    ap = _diff_prompt(lamv, gsub, qp, kbp, vbp, lam_init)
    xp = _out_proj(ap.reshape(n_p, d), xp, wo)

    tm_s = min(ROW_TILE, n_s)
    cos_s, sin_s = _rope_table(tm_s, ts, past, tm_s)
    qs, kfs, _, vfs, _ = _qkv_a(xs.reshape(1, n_s, d), g0, wq, wk, wv, cos_s, sin_s, False, A_DK ** -0.5)
    kfs3, vfs3 = kfs.reshape(bs, ts, d), vfs.reshape(bs, ts, d)
    a_s = _diff_sample(lamv, gsub, qs.reshape(bs, ts, d), cache_k_a[0].reshape(bs, past, d), bf(kfs3),
                       cache_v_a[0].reshape(bs, past, d), bf(vfs3), lam_init)
    xs = _out_proj(a_s.reshape(n_s, d), xs, wo)

    mw = moe_weights(0)
    gfin = row(g_final)
    xp = _moe(xp, *mw, gfin, final_norm=False)
    xs = _moe(xs, *mw, gfin, final_norm=False)

    gkv, g1 = row(g_kv), row(g_attn[1])
    wkv, wqb, wob = bf(w_kv_b), bf(w_q_b[0]), bf(w_o_b[0])
    bias = _band_bias(rel_bias_b[0].astype(F32))

    xp3 = xp.reshape(bp, sp, d)
    kpad, vpad, qb = _kvq_b(xp3, gkv, g1, wkv, wqb, BF16, lead_pad=True)
    kbt, vbt, _ = _kvq_b(xp3, gkv, g1, wkv, wqb, F32, lead_pad=False, row_block=sp // ROW_TILE - 1)
    ab = _band_prompt(qb, kpad, vpad, bias)
    xp = _out_proj(ab.reshape(n_p, d), xp, wob)

    kns, vns, qbs = _kvq_b(xs.reshape(1, n_s, d), gkv, g1, wkv, wqb, F32, lead_pad=False)
    kns3, vns3 = kns.reshape(bs, ts, d), vns.reshape(bs, ts, d)
    abs_ = _band_sample(qbs.reshape(bs, ts, d), cache_k_b.reshape(bs, wb, d), kns3,
                        cache_v_b.reshape(bs, wb, d), vns3, bias)
    xs = _out_proj(abs_.reshape(n_s, d), xs, wob)

    mw = moe_weights(1)
    y_prompt = _moe(xp, *mw, gfin, final_norm=True).reshape(bp, sp, d)
    y_sample = _moe(xs, *mw, gfin, final_norm=True).reshape(bs, ts, d)

    wp = min(B_WINDOW, sp)
    assert wp == ROW_TILE
    k_a_prompt = kfp.reshape(1, bp, sp, 2 * A_HEADS, A_DK)
    v_a_prompt = vfp.reshape(1, bp, sp, A_HEADS, A_DV)
    k_a_sample = kfs.reshape(1, bs, ts, 2 * A_HEADS, A_DK)
    v_a_sample = vfs.reshape(1, bs, ts, A_HEADS, A_DV)
    k_b_prompt = kbt.reshape(bp, wp, B_HEADS, B_DH)
    v_b_prompt = vbt.reshape(bp, wp, B_HEADS, B_DH)
    k_b_sample = jnp.concatenate([cache_k_b.reshape(bs, wb, d)[:, ts:], kns3], axis=1).reshape(bs, wb, B_HEADS, B_DH)
    v_b_sample = jnp.concatenate([cache_v_b.reshape(bs, wb, d)[:, ts:], vns3], axis=1).reshape(bs, wb, B_HEADS, B_DH)
    return (y_prompt, y_sample, k_a_prompt, v_a_prompt, k_b_prompt, v_b_prompt,
            k_a_sample, v_a_sample, k_b_sample, v_b_sample)
```

```python
import functools
import math

import jax
import jax.numpy as jnp
from jax import lax
from jax.experimental import pallas as pl
from jax.experimental.pallas import tpu as pltpu

F32 = jnp.float32
BF16 = jnp.bfloat16

EPS = 1e-6
CHUNK = 64
D_MODEL = 1024
A_HEADS = 8
A_DK = 64
A_DV = 128
B_HEADS = 16
B_DH = 64
B_WINDOW = 512
REL_CLIP = 256
N_GROUPS = 4
EXPERTS_PER_GROUP = 4
N_EXPERTS = 16
EXPERT_FF = 512
ROPE_THETA = 10000.0

LANES = 128
NEG = -1e30
LOG2E = math.log2(math.e)
VMEM_LIMIT = 56 * 1024 * 1024

ROW_TILE = 512
MOE_TILE = 1024
MOE_CHUNK = 128
HEADS_PER_STEP = 4
ONES_ROWS = 16
BAND_TQ = 256
BAND_KW = BAND_TQ + B_WINDOW
BAND_ROWS = 128


def _params(*sem):
    return pltpu.CompilerParams(dimension_semantics=sem, vmem_limit_bytes=VMEM_LIMIT)


def _rms_scale(x):
    return x * lax.rsqrt(jnp.mean(x * x, axis=-1, keepdims=True) + EPS)


def _dot(a, b):
    return jnp.dot(a, b, preferred_element_type=F32)


def _dot_nt(a, b):
    return lax.dot_general(a, b, (((1,), (1,)), ((), ())), preferred_element_type=F32)


def _chunk_of(pos):
    return jnp.right_shift(pos, CHUNK.bit_length() - 1)


def _stack_halves(q):
    lane = lax.broadcasted_iota(jnp.int32, q.shape, 1)
    zero = jnp.zeros_like(q)
    return jnp.concatenate([jnp.where(lane < 64, q, zero), jnp.where(lane >= 64, q, zero)], axis=0)


def _rope_table_kernel(inv_ref, cos_ref, sin_ref, *, period, offset, rows):
    i = pl.program_id(0)
    row = lax.broadcasted_iota(jnp.int32, (rows, LANES), 0) + i * rows
    lane = lax.broadcasted_iota(jnp.int32, (rows, LANES), 1)
    pos = (offset + lax.rem(row, period)).astype(F32)
    ang = pos * inv_ref[...]
    s = jnp.sin(ang)
    cos_ref[...] = jnp.cos(ang)
    sin_ref[...] = jnp.where(lax.rem(lane, 64) < 32, -s, s)


def _rope_table(n_rows, period, offset, rows):
    half = A_DK // 2
    inv = jnp.power(ROPE_THETA, -jnp.arange(half, dtype=F32) / half)
    inv = jnp.tile(inv, LANES // half)[None, :]
    return pl.pallas_call(
        functools.partial(_rope_table_kernel, period=period, offset=offset, rows=rows),
        grid=(n_rows // rows,),
        in_specs=[pl.BlockSpec((1, LANES), lambda i: (0, 0))],
        out_specs=[pl.BlockSpec((rows, LANES), lambda i: (i, 0))] * 2,
        out_shape=[jax.ShapeDtypeStruct((n_rows, LANES), F32)] * 2,
        compiler_params=_params("arbitrary"),
        name="rope_table",
    )(inv)


def _qkv_a_kernel(x_ref, g_ref, wq_ref, wk_ref, wv_ref, cos_ref, sin_ref,
                  q_ref, kf_ref, kb_ref, vf_ref, vb_ref, *, transposed, q_scale):
    h = (_rms_scale(x_ref[0]) * g_ref[...]).astype(BF16)
    cos = cos_ref[...]
    sin = sin_ref[...]
    lane = lax.broadcasted_iota(jnp.int32, cos.shape, 1)
    first = lax.rem(lane, 64) < 32

    yq = _dot(h, wq_ref[...])
    yk = _dot(h, wk_ref[...])
    yv = _dot(h, wv_ref[...])
    for c in range(D_MODEL // LANES):
        sl = slice(c * LANES, (c + 1) * LANES)

        def rope(y):
            yc = y[:, sl]
            rot = jnp.where(first, pltpu.roll(yc, LANES - 32, 1), pltpu.roll(yc, 32, 1))
            return yc * cos + rot * sin

        rq = rope(yq) * q_scale
        rk = rope(yk)
        vc = yv[:, sl]
        kf_ref[0, :, sl] = rk
        vf_ref[0, :, sl] = vc
        if transposed:
            q_ref[0, c] = rq.T.astype(BF16)
            kb_ref[0, c] = rk.astype(BF16)
            vb_ref[0, c, 0, :A_DV] = vc.T.astype(BF16)
            vb_ref[0, c, 0, A_DV:] = jnp.ones((ONES_ROWS, vc.shape[0]), BF16)
        else:
            q_ref[0, :, sl] = rq.astype(BF16)
            kb_ref[0, :, sl] = rk.astype(BF16)
            vb_ref[0, :, sl] = vc.astype(BF16)


def _qkv_a(x3, g, wq, wk, wv, cos, sin, transposed, q_scale):
    b, s, d = x3.shape
    tm = min(ROW_TILE, s)
    n_tab = cos.shape[0] // tm
    nh = d // LANES
    tok = pl.BlockSpec((1, tm, d), lambda bi, i: (bi, i, 0))
    fshape = jax.ShapeDtypeStruct((b, s, d), F32)
    if transposed:
        hspecs = [pl.BlockSpec((1, nh, LANES, tm), lambda bi, i: (bi, 0, 0, i)),
                  pl.BlockSpec((1, nh, tm, LANES), lambda bi, i: (bi, 0, i, 0)),
                  pl.BlockSpec((1, nh, 1, A_DV + ONES_ROWS, tm), lambda bi, i: (bi, 0, i, 0, 0))]
        hshapes = [jax.ShapeDtypeStruct((b, nh, LANES, s), BF16),
                   jax.ShapeDtypeStruct((b, nh, s, LANES), BF16),
                   jax.ShapeDtypeStruct((b, nh, s // tm, A_DV + ONES_ROWS, tm), BF16)]
    else:
        hspecs = [tok] * 3
        hshapes = [jax.ShapeDtypeStruct((b, s, d), BF16)] * 3
    wspec = pl.BlockSpec((d, d), lambda bi, i: (0, 0))
    tab = pl.BlockSpec((tm, LANES), lambda bi, i: (i % n_tab, 0))
    return pl.pallas_call(
        functools.partial(_qkv_a_kernel, transposed=transposed, q_scale=q_scale),
        grid=(b, s // tm),
        in_specs=[tok, pl.BlockSpec((1, d), lambda bi, i: (0, 0)), wspec, wspec, wspec, tab, tab],
        out_specs=[hspecs[0], tok, hspecs[1], tok, hspecs[2]],
        out_shape=[hshapes[0], fshape, hshapes[1], fshape, hshapes[2]],
        compiler_params=_params("parallel", "parallel"),
        name="qkv_a",
    )(x3, g, wq, wk, wv, cos, sin)


def _diff_lambda(lam_ref, lam_init):
    lv = lam_ref[...]
    s1 = jnp.sum(lv[0:1] * lv[1:2], axis=-1, keepdims=True)
    s2 = jnp.sum(lv[2:3] * lv[3:4], axis=-1, keepdims=True)
    return jnp.exp(s1) - jnp.exp(s2) + lam_init


def _diff_finish(o1, o2, lam, gsub, lam_init):
    d = o1 - lam * o2
    return (_rms_scale(d) * gsub * (1.0 - lam_init)).astype(BF16)


def _diff_prompt_kernel(lam_ref, gsub_ref, qt_ref, k_ref, vt_ref, o_ref, s_ref, p_ref, *, t, lam_init):
    qi = pl.program_id(2)
    nh = qt_ref.shape[1]
    row = lax.broadcasted_iota(jnp.int32, (A_DV, t), 0)
    zero = jnp.zeros((A_DV, t), BF16)
    qst = [jnp.concatenate([jnp.where(row < A_DK, qt_ref[0, h], zero), jnp.where(row >= A_DK, qt_ref[0, h], zero)],
                           axis=1) for h in range(nh)]

    def scores(h, tile, m_prev, mask):
        s = _dot(k_ref[0, h, pl.ds(pl.multiple_of(tile * t, t), t), :], qst[h])
        if mask is not None:
            s = jnp.where(mask, s, NEG)
        m = jnp.maximum(m_prev, jnp.max(s, axis=0, keepdims=True))
        s_ref[h] = s
        return m, jnp.exp2(m_prev - m)

    def accumulate(h, tile, alpha, acc):
        return alpha * acc + _dot(vt_ref[0, h, tile], p_ref[h])

    def exponentials(h, m):
        p_ref[h] = jnp.exp2(s_ref[h] - m).astype(BF16)

    kk = lax.broadcasted_iota(jnp.int32, (t, 2 * t), 0)
    qq = lax.broadcasted_iota(jnp.int32, (t, 2 * t), 1)
    mask = _chunk_of(kk) <= _chunk_of(jnp.bitwise_and(qq, t - 1))
    m0 = jnp.full((1, 2 * t), NEG, F32)
    carry = []
    for h in range(nh):
        m, alpha = scores(h, qi, m0, mask)
        p_ref[h] = jnp.zeros((t, 2 * t), BF16)
        carry.append((m, alpha, jnp.ones((1, 2 * t), F32), jnp.zeros((A_DV + ONES_ROWS, 2 * t), F32)))

    def v_tile(n):
        return jnp.where(n == 0, qi, jnp.maximum(n - 1, 0))

    def body(j, carry):
        out = []
        for h in range(nh):
            m_s, alpha_s, alpha_p, acc = carry[h]
            acc = accumulate(h, v_tile(j - 1), alpha_p, acc)
            exponentials(h, m_s)
            m_new, alpha_new = scores(h, j, m_s, None)
            out.append((m_new, alpha_new, alpha_s, acc))
        return tuple(out)

    carry = lax.fori_loop(0, qi, body, tuple(carry))
    lam = _diff_lambda(lam_ref, lam_init)
    for h in range(nh):
        m_s, alpha_s, alpha_p, acc = carry[h]
        acc = accumulate(h, v_tile(qi - 1), alpha_p, acc)
        exponentials(h, m_s)
        acc = accumulate(h, v_tile(qi), alpha_s, acc)
        o = (acc[:A_DV] * (1.0 / acc[A_DV:A_DV + 1])).T
        o_ref[0, :, h * A_DV:(h + 1) * A_DV] = _diff_finish(o[:t], o[t:], lam, gsub_ref[...], lam_init)


def _diff_prompt(lamv, gsub, qt, k, vt, lam_init):
    b, nh, _, s = qt.shape
    t = vt.shape[-1]
    hs = HEADS_PER_STEP
    return pl.pallas_call(
        functools.partial(_diff_prompt_kernel, t=t, lam_init=lam_init),
        grid=(b, nh // hs, s // t),
        in_specs=[pl.BlockSpec(lamv.shape, lambda bi, hi, qi: (0, 0)),
                  pl.BlockSpec((1, A_DV), lambda bi, hi, qi: (0, 0)),
                  pl.BlockSpec((1, hs, LANES, t), lambda bi, hi, qi: (bi, hi, 0, qi)),
                  pl.BlockSpec((1, hs, s, LANES), lambda bi, hi, qi: (bi, hi, 0, 0), pipeline_mode=pl.Buffered(1)),
                  pl.BlockSpec((1, hs, s // t, A_DV + ONES_ROWS, t), lambda bi, hi, qi: (bi, hi, 0, 0, 0),
                               pipeline_mode=pl.Buffered(1))],
        out_specs=pl.BlockSpec((1, t, hs * LANES), lambda bi, hi, qi: (bi, qi, hi)),
        out_shape=jax.ShapeDtypeStruct((b, s, nh * LANES), BF16),
        scratch_shapes=[pltpu.VMEM((hs, t, 2 * t), F32), pltpu.VMEM((hs, t, 2 * t), BF16)],
        compiler_params=_params("parallel", "parallel", "arbitrary"),
        name="diff_attn_prompt",
    )(lamv, gsub, qt, k, vt)


def _pad_rows(x, rows):
    return jnp.concatenate([x, jnp.zeros((rows - x.shape[0], x.shape[1]), x.dtype)], axis=0)


def _diff_sample_kernel(lam_ref, gsub_ref, q_ref, kc_ref, kn_ref, vc_ref, vn_ref, o_ref, *, lam_init):
    t = q_ref.shape[1]
    qs = _stack_halves(q_ref[0])
    kc = kc_ref[0].astype(BF16)
    vc = vc_ref[0].astype(BF16)
    kn = _pad_rows(kn_ref[0], LANES)
    vn = _pad_rows(vn_ref[0], LANES)
    sc = _dot_nt(qs, kc)
    sn = _dot_nt(qs, kn)
    col = lax.broadcasted_iota(jnp.int32, sn.shape, 1)
    sn = jnp.where(col < t, sn, NEG)
    m = jnp.maximum(jnp.max(sc, axis=-1, keepdims=True), jnp.max(sn, axis=-1, keepdims=True))
    pc = jnp.exp(sc - m)
    pn = jnp.exp(sn - m)
    l = jnp.sum(pc, axis=-1, keepdims=True) + jnp.sum(pn, axis=-1, keepdims=True)
    acc = _dot(pc.astype(BF16), vc) + _dot(pn.astype(BF16), vn)
    o = acc * (1.0 / l)
    o_ref[0] = _diff_finish(o[:t], o[t:], _diff_lambda(lam_ref, lam_init), gsub_ref[...], lam_init)


def _diff_sample(lamv, gsub, q, kc, kn, vc, vn, lam_init):
    b, t, d = q.shape
    past = kc.shape[1]
    new = pl.BlockSpec((1, t, LANES), lambda bi, hi: (bi, 0, hi))
    old = pl.BlockSpec((1, past, LANES), lambda bi, hi: (bi, 0, hi))
    return pl.pallas_call(
        functools.partial(_diff_sample_kernel, lam_init=lam_init),
        grid=(b, d // LANES),
        in_specs=[pl.BlockSpec(lamv.shape, lambda bi, hi: (0, 0)),
                  pl.BlockSpec((1, A_DV), lambda bi, hi: (0, 0)), new, old, new, old, new],
        out_specs=new,
        out_shape=jax.ShapeDtypeStruct((b, t, d), BF16),
        compiler_params=_params("parallel", "parallel"),
        name="diff_attn_sample",
    )(lamv, gsub, q, kc, kn, vc, vn)


def _moe_gates(hf, wr_ref, br_ref):
    hi = hf.astype(BF16)
    lo = (hf - hi.astype(F32)).astype(BF16)
    both = _dot(hi, wr_ref[...])
    lg = both[:, :LANES] + both[:, LANES:] + _dot(lo, wr_ref[:, :LANES]) + br_ref[...]
    lane = lax.broadcasted_iota(jnp.int32, lg.shape, 1).astype(F32)
    big = float(LANES)

    def first_argmax(v):
        mx = jnp.max(v, axis=-1, keepdims=True)
        return mx, jnp.min(jnp.where(v == mx, lane, big), axis=-1, keepdims=True)

    is_g = lane < N_GROUPS
    gmax, gidx = first_argmax(jnp.where(is_g, lg, NEG))
    gval = 1.0 / jnp.sum(jnp.where(is_g, jnp.exp(lg - gmax), 0.0), axis=-1, keepdims=True)
    lo = N_GROUPS + EXPERTS_PER_GROUP * gidx
    el = jnp.where((lane >= lo) & (lane < lo + EXPERTS_PER_GROUP), lg, NEG)
    v1, i1 = first_argmax(el)
    v2, i2 = first_argmax(jnp.where(lane == i1, NEG, el))
    e2 = jnp.exp(v2 - v1)
    inv = gval / (1.0 + e2)
    gates = jnp.where(lane == i1 - lo, inv, 0.0) + jnp.where(lane == i2 - lo, inv * e2, 0.0)
    return jnp.where(lane == gidx, 1.0, 0.0), gates


def _moe_kernel(a_ref, wo_ref, x_ref, g_ref, wr_ref, br_ref, tri_ref, wg_ref, wu_ref, wd_ref, gf_ref, o_ref,
                h_ref, gsplit_ref, rank_ref, rank_t_ref, pos_ref, y_ref, base_ref, *, final_norm):
    grp = pl.program_id(1)
    nb = x_ref.shape[0]

    @pl.when(grp == 0)
    def _():
        x = x_ref[...] + _dot(a_ref[...], wo_ref[...])
        o_ref[...] = x
        hf = _rms_scale(x) * g_ref[...]
        h_ref[...] = hf.astype(BF16)
        member, gates = _moe_gates(hf, wr_ref, br_ref)
        rest = gates
        split = jnp.zeros_like(gates)
        for i in range(3):
            part = rest.astype(BF16).astype(F32)
            split = split + pltpu.roll(part, EXPERTS_PER_GROUP * i, 1)
            rest = rest - part
        gsplit_ref[...] = split.astype(BF16)
        rank = jnp.where(member > 0.0, _dot(tri_ref[...], member.astype(BF16)) + 1.0, 0.0)
        rank_ref[...] = rank
        rank_t_ref[...] = rank.T
        pos_ref[...] = jnp.full(pos_ref.shape, -1.0, F32)
        y_ref[...] = jnp.zeros_like(y_ref)
        base_ref[0] = 0

    base = base_ref[0]
    rrow = rank_t_ref[pl.ds(grp, 1), :]
    n_chunks = (jnp.max(rrow).astype(jnp.int32) + MOE_CHUNK - 1) // MOE_CHUNK
    lane = lax.broadcasted_iota(jnp.int32, (nb, LANES), 1)
    rcol = jnp.sum(jnp.where(lane == grp, rank_ref[...], 0.0), axis=-1, keepdims=True)
    pos_ref[...] = jnp.where(rcol > 0.0, base.astype(F32) + rcol - 1.0, pos_ref[...])

    def chunk(c, carry):
        want = (lax.broadcasted_iota(jnp.int32, (MOE_CHUNK, nb), 0) + (c * MOE_CHUNK + 1)).astype(F32)
        onehot = jnp.where(rrow == want, 1.0, 0.0).astype(BF16)
        xg = _dot(onehot, h_ref[...]).astype(BF16)
        gs3 = _dot(onehot, gsplit_ref[...])
        gs = (gs3 + pltpu.roll(gs3, LANES - EXPERTS_PER_GROUP, 1)
              + pltpu.roll(gs3, LANES - 2 * EXPERTS_PER_GROUP, 1))
        parts = []
        for j in range(EXPERTS_PER_GROUP):
            a = _dot(xg, wg_ref[j])
            u = _dot(xg, wu_ref[j])
            parts.append((a * (1.0 / (1.0 + jnp.exp(-a))) * u * gs[:, j:j + 1]).astype(BF16))
        y = _dot(jnp.concatenate(parts, axis=1), wd_ref[0])
        off = pl.multiple_of(base + c * MOE_CHUNK, MOE_CHUNK)
        y_ref[pl.ds(off, MOE_CHUNK), :] = y.astype(BF16)
        return carry

    lax.fori_loop(0, n_chunks, chunk, 0)
    base_ref[0] = base + n_chunks * MOE_CHUNK

    @pl.when(grp == N_GROUPS - 1)
    def _():
        slot = lax.broadcasted_iota(jnp.int32, (nb, y_ref.shape[0]), 1).astype(F32)
        scatter = jnp.where(pos_ref[...] == slot, 1.0, 0.0).astype(BF16)
        out = o_ref[...] + _dot(scatter, y_ref[...])
        if final_norm:
            out = _rms_scale(out) * gf_ref[...]
        o_ref[...] = out


def _moe(a, wo, x, g, wr, br, wg, wu, wd, gf, final_norm):
    n, d = x.shape
    tm = min(MOE_TILE, n)
    ff = EXPERTS_PER_GROUP * EXPERT_FF
    slots = tm + N_GROUPS * MOE_CHUNK
    tri = jnp.tri(tm, tm, -1, dtype=BF16)
    once = pl.Buffered(1)
    tok = pl.BlockSpec((tm, d), lambda i, e: (i, 0))
    tok_once = pl.BlockSpec((tm, d), lambda i, e: (i, 0), pipeline_mode=once)
    vec = pl.BlockSpec((1, d), lambda i, e: (0, 0))
    experts = pl.BlockSpec((EXPERTS_PER_GROUP, d, EXPERT_FF), lambda i, e: (e, 0, 0))
    return pl.pallas_call(
        functools.partial(_moe_kernel, final_norm=final_norm),
        grid=(n // tm, N_GROUPS),
        in_specs=[tok_once, pl.BlockSpec((d, d), lambda i, e: (0, 0), pipeline_mode=once), tok_once, vec,
                  pl.BlockSpec((d, 2 * LANES), lambda i, e: (0, 0), pipeline_mode=once),
                  pl.BlockSpec((1, LANES), lambda i, e: (0, 0)),
                  pl.BlockSpec((tm, tm), lambda i, e: (0, 0), pipeline_mode=once),
                  experts, experts,
                  pl.BlockSpec((1, ff, d), lambda i, e: (e, 0, 0)),
                  vec],
        out_specs=tok,
        out_shape=jax.ShapeDtypeStruct((n, d), F32),
        scratch_shapes=[pltpu.VMEM((tm, d), BF16), pltpu.VMEM((tm, LANES), BF16),
                        pltpu.VMEM((tm, LANES), F32), pltpu.VMEM((LANES, tm), F32),
                        pltpu.VMEM((tm, 1), F32), pltpu.VMEM((slots, d), BF16),
                        pltpu.SMEM((1,), jnp.int32)],
        compiler_params=_params("parallel", "arbitrary"),
        name="hier_moe",
    )(a, wo, x, g, wr, br, tri, wg, wu, wd, gf)


def _kvq_b_kernel(x_ref, gkv_ref, gq_ref, wkv_ref, wq_ref, k_ref, v_ref, q_ref, *, lead_pad):
    def compute():
        xh = _rms_scale(x_ref[0])
        kv = _dot((xh * gkv_ref[...]).astype(BF16), wkv_ref[...])
        k_ref[0] = kv[:, :D_MODEL].astype(k_ref.dtype)
        v_ref[0] = kv[:, D_MODEL:].astype(v_ref.dtype)
        q = _dot((xh * gq_ref[...]).astype(BF16), wq_ref[...])
        q_ref[0] = (q * (B_DH ** -0.5 * LOG2E)).astype(BF16)

    if lead_pad:
        i = pl.program_id(1)

        @pl.when(i == 0)
        def _():
            k_ref[...] = jnp.zeros_like(k_ref)
            v_ref[...] = jnp.zeros_like(v_ref)

        pl.when(i > 0)(compute)
    else:
        compute()


def _kvq_b(x3, gkv, gq, wkv, wq, kv_dtype, lead_pad, row_block=None):
    b, s, d = x3.shape
    tm = min(ROW_TILE, s)
    if lead_pad:
        assert tm == B_WINDOW
        grid = (b, s // tm + 1)
        src = lambda bi, i: (bi, jnp.maximum(i - 1, 0), 0)
        dst = lambda bi, i: (bi, i, 0)
        s_kv, s_q = s + B_WINDOW, s
    elif row_block is not None:
        grid = (b, 1)
        src = lambda bi, i: (bi, row_block, 0)
        dst = lambda bi, i: (bi, 0, 0)
        s_kv = s_q = tm
    else:
        grid = (b, s // tm)
        src = dst = lambda bi, i: (bi, i, 0)
        s_kv = s_q = s
    vec = pl.BlockSpec((1, d), lambda bi, i: (0, 0))
    return pl.pallas_call(
        functools.partial(_kvq_b_kernel, lead_pad=lead_pad),
        grid=grid,
        in_specs=[pl.BlockSpec((1, tm, d), src), vec, vec,
                  pl.BlockSpec((d, 2 * d), lambda bi, i: (0, 0)),
                  pl.BlockSpec((d, d), lambda bi, i: (0, 0))],
        out_specs=[pl.BlockSpec((1, tm, d), dst), pl.BlockSpec((1, tm, d), dst),
                   pl.BlockSpec((1, tm, d), src if lead_pad else dst)],
        out_shape=[jax.ShapeDtypeStruct((b, s_kv, d), kv_dtype), jax.ShapeDtypeStruct((b, s_kv, d), kv_dtype),
                   jax.ShapeDtypeStruct((b, s_q, d), BF16)],
        compiler_params=_params("parallel", "arbitrary"),
        name="kvq_b",
    )(x3, gkv, gq, wkv, wq)


def _band_bias_kernel(u_ref, o_ref):
    x = jnp.broadcast_to(u_ref[0], (BAND_TQ, 2 * B_WINDOW))
    bias = pltpu.roll(x, 0, 1, stride=1, stride_axis=0)[:, :BAND_KW]
    qc = _chunk_of(lax.broadcasted_iota(jnp.int32, bias.shape, 0))
    kc = _chunk_of(lax.broadcasted_iota(jnp.int32, bias.shape, 1))
    ok = (kc >= qc) & (kc <= qc + B_WINDOW // CHUNK)
    o_ref[0] = jnp.where(ok, bias * LOG2E, NEG)


def _band_bias(rel_bias):
    edge = jnp.broadcast_to(rel_bias[:, 2 * REL_CLIP:], (B_HEADS, REL_CLIP))
    u = jnp.concatenate([edge, rel_bias[:, :0:-1], edge], axis=1)[:, None, :]
    assert u.shape[-1] == 2 * B_WINDOW
    return pl.pallas_call(
        _band_bias_kernel,
        grid=(B_HEADS,),
        in_specs=[pl.BlockSpec((1, 1, 2 * B_WINDOW), lambda h: (h, 0, 0))],
        out_specs=pl.BlockSpec((1, BAND_TQ, BAND_KW), lambda h: (h, 0, 0)),
        out_shape=jax.ShapeDtypeStruct((B_HEADS, BAND_TQ, BAND_KW), F32),
        compiler_params=_params("parallel"),
        name="band_bias",
    )(u)


def _pick_halves(o, t):
    lane = lax.broadcasted_iota(jnp.int32, (t, LANES), 1)
    return jnp.where(lane < 64, o[:t], o[t:])


def _band_prompt_kernel(q_ref, k_ref, v_ref, bias_ref, o_ref, p_ref, *, n_tiles):
    t = BAND_TQ

    def probabilities(off, lead):
        kw = k_ref[0, pl.ds(off, BAND_KW), :]
        qs = _stack_halves(q_ref[0, pl.ds(off, t), :])
        for g in range(2 * t // BAND_ROWS):
            r0 = g * BAND_ROWS
            s = _dot_nt(qs[r0:r0 + BAND_ROWS], kw) + bias_ref[r0 // t, r0 % t:r0 % t + BAND_ROWS, :]
            if lead:
                c = lax.broadcasted_iota(jnp.int32, s.shape, 1)
                s = jnp.where(c >= B_WINDOW - off, s, NEG)
            m = jnp.max(s, axis=-1, keepdims=True)
            p_ref[r0:r0 + BAND_ROWS, :] = jnp.exp2(s - m).astype(BF16)

    def outputs(off):
        vx = jnp.concatenate([v_ref[0, pl.ds(off, BAND_KW), :], jnp.ones((BAND_KW, LANES), BF16)], axis=1)
        o = _dot(p_ref[...], vx)
        o = o[:, :LANES] * (1.0 / o[:, LANES:])
        o_ref[0, pl.ds(off, t), :] = _pick_halves(o, t).astype(BF16)

    n_lead = B_WINDOW // t
    for ti in range(n_lead):
        if ti > 0:
            outputs((ti - 1) * t)
        probabilities(ti * t, True)

    def body(ti, carry):
        outputs(pl.multiple_of((ti - 1) * t, t))
        probabilities(pl.multiple_of(ti * t, t), False)
        return carry

    lax.fori_loop(n_lead, n_tiles, body, 0)
    outputs((n_tiles - 1) * t)


def _band_prompt(q, kpad, vpad, bias):
    b, s, d = q.shape
    t = BAND_TQ
    nhp = d // LANES
    win = pl.BlockSpec((1, s + B_WINDOW, LANES), lambda hp, bi: (bi, 0, hp))
    seq = pl.BlockSpec((1, s, LANES), lambda hp, bi: (bi, 0, hp))
    return pl.pallas_call(
        functools.partial(_band_prompt_kernel, n_tiles=s // t),
        grid=(nhp, b),
        in_specs=[seq, win, win, pl.BlockSpec((2, t, BAND_KW), lambda hp, bi: (hp, 0, 0))],
        out_specs=seq,
        out_shape=jax.ShapeDtypeStruct((b, s, d), BF16),
        scratch_shapes=[pltpu.VMEM((2 * t, BAND_KW), BF16)],
        compiler_params=_params("parallel", "parallel"),
        name="band_attn_prompt",
    )(q, kpad, vpad, bias)


def _band_sample_kernel(q_ref, kc_ref, kn_ref, vc_ref, vn_ref, bias_ref, o_ref):
    t = q_ref.shape[1]
    wb = kc_ref.shape[1]
    qs = _stack_halves(q_ref[0])
    kc = kc_ref[0].astype(BF16)
    vc = vc_ref[0].astype(BF16)
    kn = _pad_rows(kn_ref[0].astype(BF16), LANES)
    vn = _pad_rows(vn_ref[0].astype(BF16), LANES)
    bias = bias_ref[...].reshape(2 * t, BAND_KW)
    sc = _dot_nt(qs, kc) + bias[:, :wb]
    sn = _dot_nt(qs, kn) + bias[:, wb:wb + LANES]
    col = lax.broadcasted_iota(jnp.int32, sn.shape, 1)
    sn = jnp.where(col < t, sn, NEG)
    m = jnp.maximum(jnp.max(sc, axis=-1, keepdims=True), jnp.max(sn, axis=-1, keepdims=True))
    pc = jnp.exp2(sc - m)
    pn = jnp.exp2(sn - m)
    l = jnp.sum(pc, axis=-1, keepdims=True) + jnp.sum(pn, axis=-1, keepdims=True)
    o = (_dot(pc.astype(BF16), vc) + _dot(pn.astype(BF16), vn)) * (1.0 / l)
    o_ref[0] = _pick_halves(o, t).astype(BF16)


def _band_sample(q, kc, kn, vc, vn, bias):
    b, t, d = q.shape
    wb = kc.shape[1]
    assert wb == B_WINDOW
    new = pl.BlockSpec((1, t, LANES), lambda bi, hp: (bi, 0, hp))
    old = pl.BlockSpec((1, wb, LANES), lambda bi, hp: (bi, 0, hp))
    return pl.pallas_call(
        _band_sample_kernel,
        grid=(b, d // LANES),
        in_specs=[new, old, new, old, new, pl.BlockSpec((2, t, BAND_KW), lambda bi, hp: (hp, 0, 0))],
        out_specs=new,
        out_shape=jax.ShapeDtypeStruct((b, t, d), BF16),
        compiler_params=_params("parallel", "parallel"),
        name="band_attn_sample",
    )(q, kc, kn, vc, vn, bias)


def kernel(x_prompt, x_sample, cache_k_a, cache_v_a, cache_k_b, cache_v_b, g_attn, g_ffn, w_q_a, w_k_a, w_v_a, lam_q1, lam_k1, lam_q2, lam_k2, g_sub_a, w_o_a, g_kv, w_kv_b, w_q_b, rel_bias_b, w_o_b, w_router_g, b_router_g, w_router_e, b_router_e, w_gate, w_up, w_down, g_final):
    bp, sp, d = x_prompt.shape
    bs, ts, _ = x_sample.shape
    past = cache_k_a.shape[2]
    wb = cache_k_b.shape[1]
    n_p, n_s = bp * sp, bs * ts
    assert d == D_MODEL and sp % ROW_TILE == 0 and ROW_TILE % ts == 0 and ts <= LANES
    depth = g_attn.shape[0]
    n_a = w_q_a.shape[0]
    assert depth == 2 and n_a == 1 and w_q_b.shape[0] == 1

    row = lambda v: v.reshape(1, -1).astype(F32)
    bf = lambda w: w.astype(BF16)

    def moe_weights(l):
        wr = jnp.concatenate([w_router_g[l], w_router_e[l].transpose(1, 0, 2).reshape(d, N_EXPERTS)], axis=1)
        wr = jnp.pad(wr, ((0, 0), (0, LANES - wr.shape[1])))
        br = jnp.concatenate([b_router_g[l], b_router_e[l].reshape(-1)])
        br = jnp.pad(br, (0, LANES - br.shape[0]))[None, :]
        wr = wr.astype(F32)
        wr_hi = bf(wr)
        wr_split = jnp.concatenate([wr_hi, bf(wr - wr_hi.astype(F32))], axis=1)
        wd = bf(w_down[l]).reshape(N_GROUPS, EXPERTS_PER_GROUP * EXPERT_FF, d)
        return row(g_ffn[l]), wr_split, br.astype(F32), bf(w_gate[l]), bf(w_up[l]), wd

    xp = x_prompt.reshape(n_p, d)
    xs = x_sample.reshape(n_s, d)

    lam_init = 0.8 - 0.6 * math.exp(-0.3 * 0)
    lamv = jnp.stack([lam_q1[0], lam_k1[0], lam_q2[0], lam_k2[0]]).astype(F32)
    gsub = row(g_sub_a[0])
    wq, wk, wv, wo = bf(w_q_a[0]), bf(w_k_a[0]), bf(w_v_a[0]), bf(w_o_a[0])
    g0 = row(g_attn[0])

    cos_p, sin_p = _rope_table(sp, sp, 0, min(ROW_TILE, sp))
    qp, kfp, kbp, vfp, vbp = _qkv_a(x_prompt, g0, wq, wk, wv, cos_p, sin_p, True, A_DK ** -0.5 * LOG2E)
    ap = _diff_prompt(lamv, gsub, qp, kbp, vbp, lam_init)

    tm_s = min(ROW_TILE, n_s)
    cos_s, sin_s = _rope_table(tm_s, ts, past, tm_s)
    qs, kfs, _, vfs, _ = _qkv_a(xs.reshape(1, n_s, d), g0, wq, wk, wv, cos_s, sin_s, False, A_DK ** -0.5)
    kfs3, vfs3 = kfs.reshape(bs, ts, d), vfs.reshape(bs, ts, d)
    a_s = _diff_sample(lamv, gsub, qs.reshape(bs, ts, d), cache_k_a[0].reshape(bs, past, d), bf(kfs3),
                       cache_v_a[0].reshape(bs, past, d), bf(vfs3), lam_init)

    mw = moe_weights(0)
    gfin = row(g_final)
    xp = _moe(ap.reshape(n_p, d), wo, xp, *mw, gfin, final_norm=False)
    xs = _moe(a_s.reshape(n_s, d), wo, xs, *mw, gfin, final_norm=False)

    gkv, g1 = row(g_kv), row(g_attn[1])
    wkv, wqb, wob = bf(w_kv_b), bf(w_q_b[0]), bf(w_o_b[0])
    bias = _band_bias(rel_bias_b[0].astype(F32))

    xp3 = xp.reshape(bp, sp, d)
    kpad, vpad, qb = _kvq_b(xp3, gkv, g1, wkv, wqb, BF16, lead_pad=True)
    kbt, vbt, _ = _kvq_b(xp3, gkv, g1, wkv, wqb, F32, lead_pad=False, row_block=sp // ROW_TILE - 1)
    ab = _band_prompt(qb, kpad, vpad, bias)

    kns, vns, qbs = _kvq_b(xs.reshape(1, n_s, d), gkv, g1, wkv, wqb, F32, lead_pad=False)
    kns3, vns3 = kns.reshape(bs, ts, d), vns.reshape(bs, ts, d)
    abs_ = _band_sample(qbs.reshape(bs, ts, d), cache_k_b.reshape(bs, wb, d), kns3,
                        cache_v_b.reshape(bs, wb, d), vns3, bias)

    mw = moe_weights(1)
    y_prompt = _moe(ab.reshape(n_p, d), wob, xp, *mw, gfin, final_norm=True).reshape(bp, sp, d)
    y_sample = _moe(abs_.reshape(n_s, d), wob, xs, *mw, gfin, final_norm=True).reshape(bs, ts, d)

    wp = min(B_WINDOW, sp)
    assert wp == ROW_TILE
    k_a_prompt = kfp.reshape(1, bp, sp, 2 * A_HEADS, A_DK)
    v_a_prompt = vfp.reshape(1, bp, sp, A_HEADS, A_DV)
    k_a_sample = kfs.reshape(1, bs, ts, 2 * A_HEADS, A_DK)
    v_a_sample = vfs.reshape(1, bs, ts, A_HEADS, A_DV)
    k_b_prompt = kbt.reshape(bp, wp, B_HEADS, B_DH)
    v_b_prompt = vbt.reshape(bp, wp, B_HEADS, B_DH)
    k_b_sample = jnp.concatenate([cache_k_b[:, ts:], kns.reshape(bs, ts, B_HEADS, B_DH)], axis=1)
    v_b_sample = jnp.concatenate([cache_v_b[:, ts:], vns.reshape(bs, ts, B_HEADS, B_DH)], axis=1)
    return (y_prompt, y_sample, k_a_prompt, v_a_prompt, k_b_prompt, v_b_prompt,
            k_a_sample, v_a_sample, k_b_sample, v_b_sample)
```

```python
import functools
import math

import jax
import jax.numpy as jnp
from jax import lax
from jax.experimental import pallas as pl
from jax.experimental.pallas import tpu as pltpu

F32 = jnp.float32
BF16 = jnp.bfloat16

EPS = 1e-6
CHUNK = 64
D_MODEL = 1024
A_HEADS = 8
A_DK = 64
A_DV = 128
B_HEADS = 16
B_DH = 64
B_WINDOW = 512
REL_CLIP = 256
N_GROUPS = 4
EXPERTS_PER_GROUP = 4
N_EXPERTS = 16
EXPERT_FF = 512
ROPE_THETA = 10000.0

LANES = 128
NEG = -1e30
LOG2E = math.log2(math.e)
VMEM_LIMIT = 56 * 1024 * 1024

ROW_TILE = 512
MOE_TILE = 1024
MOE_CHUNK = 128
HEADS_PER_STEP = 4
ONES_ROWS = 16
BAND_TQ = 256
BAND_KW = BAND_TQ + B_WINDOW
BAND_ROWS = 128
SAMPLE_KV_ROWS = 512


def _params(*sem):
    return pltpu.CompilerParams(dimension_semantics=sem, vmem_limit_bytes=VMEM_LIMIT)


def _rms_scale(x):
    return x * lax.rsqrt(jnp.mean(x * x, axis=-1, keepdims=True) + EPS)


def _dot(a, b):
    return jnp.dot(a, b, preferred_element_type=F32)


def _dot_nt(a, b):
    return lax.dot_general(a, b, (((1,), (1,)), ((), ())), preferred_element_type=F32)


def _chunk_of(pos):
    return jnp.right_shift(pos, CHUNK.bit_length() - 1)


def _stack_halves(q):
    lane = lax.broadcasted_iota(jnp.int32, q.shape, 1)
    zero = jnp.zeros_like(q)
    return jnp.concatenate([jnp.where(lane < 64, q, zero), jnp.where(lane >= 64, q, zero)], axis=0)


def _rope_table_kernel(inv_ref, cos_ref, sin_ref, *, period, offset, rows):
    i = pl.program_id(0)
    row = lax.broadcasted_iota(jnp.int32, (rows, LANES), 0) + i * rows
    lane = lax.broadcasted_iota(jnp.int32, (rows, LANES), 1)
    pos = (offset + lax.rem(row, period)).astype(F32)
    ang = pos * inv_ref[...]
    s = jnp.sin(ang)
    cos_ref[...] = jnp.cos(ang)
    sin_ref[...] = jnp.where(lax.rem(lane, 64) < 32, -s, s)


def _rope_table(n_rows, period, offset, rows):
    half = A_DK // 2
    inv = jnp.power(ROPE_THETA, -jnp.arange(half, dtype=F32) / half)
    inv = jnp.tile(inv, LANES // half)[None, :]
    return pl.pallas_call(
        functools.partial(_rope_table_kernel, period=period, offset=offset, rows=rows),
        grid=(n_rows // rows,),
        in_specs=[pl.BlockSpec((1, LANES), lambda i: (0, 0))],
        out_specs=[pl.BlockSpec((rows, LANES), lambda i: (i, 0))] * 2,
        out_shape=[jax.ShapeDtypeStruct((n_rows, LANES), F32)] * 2,
        compiler_params=_params("arbitrary"),
        name="rope_table",
    )(inv)


def _qkv_a_kernel(x_ref, g_ref, wq_ref, wk_ref, wv_ref, cos_ref, sin_ref,
                  q_ref, kf_ref, kb_ref, vf_ref, vb_ref, *, transposed, q_scale):
    h = (_rms_scale(x_ref[0]) * g_ref[...]).astype(BF16)
    cos = cos_ref[...]
    sin = sin_ref[...]
    lane = lax.broadcasted_iota(jnp.int32, cos.shape, 1)
    first = lax.rem(lane, 64) < 32

    yq = _dot(h, wq_ref[...])
    yk = _dot(h, wk_ref[...])
    yv = _dot(h, wv_ref[...])
    for c in range(D_MODEL // LANES):
        sl = slice(c * LANES, (c + 1) * LANES)

        def rope(y):
            yc = y[:, sl]
            rot = jnp.where(first, pltpu.roll(yc, LANES - 32, 1), pltpu.roll(yc, 32, 1))
            return yc * cos + rot * sin

        rq = rope(yq) * q_scale
        rk = rope(yk)
        vc = yv[:, sl]
        kf_ref[0, :, sl] = rk
        vf_ref[0, :, sl] = vc
        if transposed:
            q_ref[0, c] = rq.T.astype(BF16)
            kb_ref[0, c] = rk.astype(BF16)
            vb_ref[0, c, 0, :A_DV] = vc.T.astype(BF16)
            vb_ref[0, c, 0, A_DV:] = jnp.ones((ONES_ROWS, vc.shape[0]), BF16)
        else:
            q_ref[0, :, sl] = rq.astype(BF16)
            kb_ref[0, :, sl] = rk.astype(BF16)
            vb_ref[0, :, sl] = vc.astype(BF16)


def _qkv_a(x3, g, wq, wk, wv, cos, sin, transposed, q_scale):
    b, s, d = x3.shape
    tm = min(ROW_TILE, s)
    n_tab = cos.shape[0] // tm
    nh = d // LANES
    tok = pl.BlockSpec((1, tm, d), lambda bi, i: (bi, i, 0))
    fshape = jax.ShapeDtypeStruct((b, s, d), F32)
    if transposed:
        hspecs = [pl.BlockSpec((1, nh, LANES, tm), lambda bi, i: (bi, 0, 0, i)),
                  pl.BlockSpec((1, nh, tm, LANES), lambda bi, i: (bi, 0, i, 0)),
                  pl.BlockSpec((1, nh, 1, A_DV + ONES_ROWS, tm), lambda bi, i: (bi, 0, i, 0, 0))]
        hshapes = [jax.ShapeDtypeStruct((b, nh, LANES, s), BF16),
                   jax.ShapeDtypeStruct((b, nh, s, LANES), BF16),
                   jax.ShapeDtypeStruct((b, nh, s // tm, A_DV + ONES_ROWS, tm), BF16)]
    else:
        hspecs = [tok] * 3
        hshapes = [jax.ShapeDtypeStruct((b, s, d), BF16)] * 3
    wspec = pl.BlockSpec((d, d), lambda bi, i: (0, 0))
    tab = pl.BlockSpec((tm, LANES), lambda bi, i: (i % n_tab, 0))
    return pl.pallas_call(
        functools.partial(_qkv_a_kernel, transposed=transposed, q_scale=q_scale),
        grid=(b, s // tm),
        in_specs=[tok, pl.BlockSpec((1, d), lambda bi, i: (0, 0)), wspec, wspec, wspec, tab, tab],
        out_specs=[hspecs[0], tok, hspecs[1], tok, hspecs[2]],
        out_shape=[hshapes[0], fshape, hshapes[1], fshape, hshapes[2]],
        compiler_params=_params("parallel", "parallel"),
        name="qkv_a",
    )(x3, g, wq, wk, wv, cos, sin)


def _diff_lambda(lam_ref, lam_init):
    lv = lam_ref[...]
    s1 = jnp.sum(lv[0:1] * lv[1:2], axis=-1, keepdims=True)
    s2 = jnp.sum(lv[2:3] * lv[3:4], axis=-1, keepdims=True)
    return jnp.exp(s1) - jnp.exp(s2) + lam_init


def _diff_finish(o1, o2, lam, gsub, lam_init):
    d = o1 - lam * o2
    return (_rms_scale(d) * gsub * (1.0 - lam_init)).astype(BF16)


def _diff_prompt_kernel(lam_ref, gsub_ref, qt_ref, k_ref, vt_ref, o_ref, s_ref, p_ref, *, t, lam_init):
    qi = pl.program_id(2)
    nh = qt_ref.shape[1]
    row = lax.broadcasted_iota(jnp.int32, (A_DV, t), 0)
    zero = jnp.zeros((A_DV, t), BF16)
    qst = [jnp.concatenate([jnp.where(row < A_DK, qt_ref[0, h], zero), jnp.where(row >= A_DK, qt_ref[0, h], zero)],
                           axis=1) for h in range(nh)]

    def scores(h, tile, m_prev, mask):
        s = _dot(k_ref[0, h, pl.ds(pl.multiple_of(tile * t, t), t), :], qst[h])
        if mask is not None:
            s = jnp.where(mask, s, NEG)
        m = jnp.maximum(m_prev, jnp.max(s, axis=0, keepdims=True))
        s_ref[h] = s
        return m, jnp.exp2(m_prev - m)

    def accumulate(h, tile, alpha, acc):
        return alpha * acc + _dot(vt_ref[0, h, tile], p_ref[h])

    def exponentials(h, m):
        p_ref[h] = jnp.exp2(s_ref[h] - m).astype(BF16)

    kk = lax.broadcasted_iota(jnp.int32, (t, 2 * t), 0)
    qq = lax.broadcasted_iota(jnp.int32, (t, 2 * t), 1)
    mask = _chunk_of(kk) <= _chunk_of(jnp.bitwise_and(qq, t - 1))
    m0 = jnp.full((1, 2 * t), NEG, F32)
    carry = []
    for h in range(nh):
        m, alpha = scores(h, qi, m0, mask)
        p_ref[h] = jnp.zeros((t, 2 * t), BF16)
        carry.append((m, alpha, jnp.ones((1, 2 * t), F32), jnp.zeros((A_DV + ONES_ROWS, 2 * t), F32)))

    def v_tile(n):
        return jnp.where(n == 0, qi, jnp.maximum(n - 1, 0))

    def body(j, carry):
        out = []
        for h in range(nh):
            m_s, alpha_s, alpha_p, acc = carry[h]
            acc = accumulate(h, v_tile(j - 1), alpha_p, acc)
            exponentials(h, m_s)
            m_new, alpha_new = scores(h, j, m_s, None)
            out.append((m_new, alpha_new, alpha_s, acc))
        return tuple(out)

    carry = lax.fori_loop(0, qi // 2, lambda j, c: body(2 * j + 1, body(2 * j, c)), tuple(carry))
    carry = lax.cond(qi % 2 == 1, lambda c: body(qi - 1, c), lambda c: c, carry)
    lam = _diff_lambda(lam_ref, lam_init)
    for h in range(nh):
        m_s, alpha_s, alpha_p, acc = carry[h]
        acc = accumulate(h, v_tile(qi - 1), alpha_p, acc)
        exponentials(h, m_s)
        acc = accumulate(h, v_tile(qi), alpha_s, acc)
        o = (acc[:A_DV] * (1.0 / acc[A_DV:A_DV + 1])).T
        o_ref[0, :, h * A_DV:(h + 1) * A_DV] = _diff_finish(o[:t], o[t:], lam, gsub_ref[...], lam_init)


def _diff_prompt(lamv, gsub, qt, k, vt, lam_init):
    b, nh, _, s = qt.shape
    t = vt.shape[-1]
    hs = HEADS_PER_STEP
    return pl.pallas_call(
        functools.partial(_diff_prompt_kernel, t=t, lam_init=lam_init),
        grid=(b, nh // hs, s // t),
        in_specs=[pl.BlockSpec(lamv.shape, lambda bi, hi, qi: (0, 0)),
                  pl.BlockSpec((1, A_DV), lambda bi, hi, qi: (0, 0)),
                  pl.BlockSpec((1, hs, LANES, t), lambda bi, hi, qi: (bi, hi, 0, qi)),
                  pl.BlockSpec((1, hs, s, LANES), lambda bi, hi, qi: (bi, hi, 0, 0), pipeline_mode=pl.Buffered(1)),
                  pl.BlockSpec((1, hs, s // t, A_DV + ONES_ROWS, t), lambda bi, hi, qi: (bi, hi, 0, 0, 0),
                               pipeline_mode=pl.Buffered(1))],
        out_specs=pl.BlockSpec((1, t, hs * LANES), lambda bi, hi, qi: (bi, qi, hi)),
        out_shape=jax.ShapeDtypeStruct((b, s, nh * LANES), BF16),
        scratch_shapes=[pltpu.VMEM((hs, t, 2 * t), F32), pltpu.VMEM((hs, t, 2 * t), BF16)],
        compiler_params=_params("parallel", "parallel", "arbitrary"),
        name="diff_attn_prompt",
    )(lamv, gsub, qt, k, vt)


def _pad_rows(x, rows):
    return jnp.concatenate([x, jnp.zeros((rows - x.shape[0], x.shape[1]), x.dtype)], axis=0)


def _diff_sample_kernel(lam_ref, gsub_ref, q_ref, kc_ref, kn_ref, vc_ref, vn_ref, o_ref, m_ref, l_ref, acc_ref,
                        *, lam_init, rows):
    c = pl.program_id(1)
    t = q_ref.shape[2]
    n_sub = q_ref.shape[1]

    @pl.when(c == 0)
    def _():
        m_ref[...] = jnp.full(m_ref.shape, NEG, F32)
        l_ref[...] = jnp.zeros(l_ref.shape, F32)
        acc_ref[...] = jnp.zeros(acc_ref.shape, F32)

    def update(keys, values, mask):
        s = jnp.concatenate([_dot_nt(q_ref[0, j], keys(j)) for j in range(n_sub)], axis=0)
        if mask is not None:
            s = jnp.where(mask(s.shape), s, NEG)
        m_old = m_ref[...]
        m = jnp.maximum(m_old, jnp.max(s, axis=-1, keepdims=True))
        p = jnp.exp(s - m)
        alpha = jnp.exp(m_old - m)
        l_ref[...] = alpha * l_ref[...] + jnp.sum(p, axis=-1, keepdims=True)
        p = p.astype(BF16)
        pv = jnp.concatenate([_dot(p[2 * t * h:2 * t * (h + 1)], values(h)) for h in range(A_HEADS)], axis=0)
        acc_ref[...] = alpha * acc_ref[...] + pv
        m_ref[...] = m

    update(lambda j: kc_ref[0, pl.ds(j, rows, stride=n_sub), :].astype(BF16),
           lambda h: vc_ref[0, pl.ds(h, rows, stride=A_HEADS), :].astype(BF16), None)

    @pl.when(c == pl.num_programs(1) - 1)
    def _():
        update(lambda j: _pad_rows(kn_ref[0, j], LANES), lambda h: _pad_rows(vn_ref[0, h], LANES),
               lambda shape: lax.broadcasted_iota(jnp.int32, shape, 1) < t)
        lam = _diff_lambda(lam_ref, lam_init)
        o = acc_ref[...] * (1.0 / l_ref[...])
        for h in range(A_HEADS):
            r = 2 * t * h
            o_ref[0, :, h * A_DV:(h + 1) * A_DV] = _diff_finish(o[r:r + t], o[r + t:r + 2 * t], lam, gsub_ref[...],
                                                                lam_init)


def _diff_sample(lamv, gsub, q, kc, kn, vc, vn, lam_init):
    b, n_sub, t, dk = q.shape
    past = vc.shape[1] // A_HEADS
    rows = min(SAMPLE_KV_ROWS, past)
    fixed = lambda *shape: pl.BlockSpec((1,) + shape, lambda bi, c: (bi,) + (0,) * len(shape))
    return pl.pallas_call(
        functools.partial(_diff_sample_kernel, lam_init=lam_init, rows=rows),
        grid=(b, past // rows),
        in_specs=[pl.BlockSpec(lamv.shape, lambda bi, c: (0, 0)),
                  pl.BlockSpec((1, A_DV), lambda bi, c: (0, 0)),
                  fixed(n_sub, t, dk),
                  pl.BlockSpec((1, rows * n_sub, dk), lambda bi, c: (bi, c, 0)),
                  fixed(n_sub, t, dk),
                  pl.BlockSpec((1, rows * A_HEADS, A_DV), lambda bi, c: (bi, c, 0)),
                  fixed(A_HEADS, t, A_DV)],
        out_specs=fixed(t, A_HEADS * A_DV),
        out_shape=jax.ShapeDtypeStruct((b, t, A_HEADS * A_DV), BF16),
        scratch_shapes=[pltpu.VMEM((A_HEADS * 2 * t, 1), F32), pltpu.VMEM((A_HEADS * 2 * t, 1), F32),
                        pltpu.VMEM((A_HEADS * 2 * t, A_DV), F32)],
        compiler_params=_params("parallel", "arbitrary"),
        name="diff_attn_sample",
    )(lamv, gsub, q, kc, kn, vc, vn)


def _moe_gates(hf, wr_ref, br_ref):
    hi = hf.astype(BF16)
    lo = (hf - hi.astype(F32)).astype(BF16)
    both = _dot(hi, wr_ref[...])
    lg = both[:, :LANES] + both[:, LANES:] + _dot(lo, wr_ref[:, :LANES]) + br_ref[...]
    lane = lax.broadcasted_iota(jnp.int32, lg.shape, 1).astype(F32)
    big = float(LANES)

    def first_argmax(v):
        mx = jnp.max(v, axis=-1, keepdims=True)
        return mx, jnp.min(jnp.where(v == mx, lane, big), axis=-1, keepdims=True)

    is_g = lane < N_GROUPS
    gmax, gidx = first_argmax(jnp.where(is_g, lg, NEG))
    gval = 1.0 / jnp.sum(jnp.where(is_g, jnp.exp(lg - gmax), 0.0), axis=-1, keepdims=True)
    lo = N_GROUPS + EXPERTS_PER_GROUP * gidx
    el = jnp.where((lane >= lo) & (lane < lo + EXPERTS_PER_GROUP), lg, NEG)
    v1, i1 = first_argmax(el)
    v2, i2 = first_argmax(jnp.where(lane == i1, NEG, el))
    e2 = jnp.exp(v2 - v1)
    inv = gval / (1.0 + e2)
    gates = jnp.where(lane == i1 - lo, inv, 0.0) + jnp.where(lane == i2 - lo, inv * e2, 0.0)
    return jnp.where(lane == gidx, 1.0, 0.0), gates


def _moe_kernel(a_ref, wo_ref, x_ref, g_ref, wr_ref, br_ref, tri_ref, wg_ref, wu_ref, wd_ref, gf_ref, o_ref,
                h_ref, gsplit_ref, rank_ref, rank_t_ref, pos_ref, y_ref, base_ref, *, final_norm):
    grp = pl.program_id(1)
    nb = x_ref.shape[0]

    @pl.when(grp == 0)
    def _():
        x = x_ref[...] + _dot(a_ref[...], wo_ref[...])
        o_ref[...] = x
        hf = _rms_scale(x) * g_ref[...]
        h_ref[...] = hf.astype(BF16)
        member, gates = _moe_gates(hf, wr_ref, br_ref)
        rest = gates
        split = jnp.zeros_like(gates)
        for i in range(3):
            part = rest.astype(BF16).astype(F32)
            split = split + pltpu.roll(part, EXPERTS_PER_GROUP * i, 1)
            rest = rest - part
        gsplit_ref[...] = split.astype(BF16)
        rank = jnp.where(member > 0.0, _dot(tri_ref[...], member.astype(BF16)) + 1.0, 0.0)
        rank_ref[...] = rank
        rank_t_ref[...] = rank.T
        pos_ref[...] = jnp.full(pos_ref.shape, -1.0, F32)
        y_ref[...] = jnp.zeros_like(y_ref)
        base_ref[0] = 0

    base = base_ref[0]
    rrow = rank_t_ref[pl.ds(grp, 1), :]
    n_chunks = (jnp.max(rrow).astype(jnp.int32) + MOE_CHUNK - 1) // MOE_CHUNK
    lane = lax.broadcasted_iota(jnp.int32, (nb, LANES), 1)
    rcol = jnp.sum(jnp.where(lane == grp, rank_ref[...], 0.0), axis=-1, keepdims=True)
    pos_ref[...] = jnp.where(rcol > 0.0, base.astype(F32) + rcol - 1.0, pos_ref[...])

    def chunk(c, carry):
        want = (lax.broadcasted_iota(jnp.int32, (MOE_CHUNK, nb), 0) + (c * MOE_CHUNK + 1)).astype(F32)
        onehot = jnp.where(rrow == want, 1.0, 0.0).astype(BF16)
        xg = _dot(onehot, h_ref[...]).astype(BF16)
        gs3 = _dot(onehot, gsplit_ref[...])
        gs = (gs3 + pltpu.roll(gs3, LANES - EXPERTS_PER_GROUP, 1)
              + pltpu.roll(gs3, LANES - 2 * EXPERTS_PER_GROUP, 1))
        parts = []
        for j in range(EXPERTS_PER_GROUP):
            a = _dot(xg, wg_ref[j])
            u = _dot(xg, wu_ref[j])
            parts.append((a * (1.0 / (1.0 + jnp.exp(-a))) * u * gs[:, j:j + 1]).astype(BF16))
        y = _dot(jnp.concatenate(parts, axis=1), wd_ref[0])
        off = pl.multiple_of(base + c * MOE_CHUNK, MOE_CHUNK)
        y_ref[pl.ds(off, MOE_CHUNK), :] = y.astype(BF16)
        return carry

    lax.fori_loop(0, n_chunks, chunk, 0)
    base_ref[0] = base + n_chunks * MOE_CHUNK

    @pl.when(grp == N_GROUPS - 1)
    def _():
        slot = lax.broadcasted_iota(jnp.int32, (nb, y_ref.shape[0]), 1).astype(F32)
        scatter = jnp.where(pos_ref[...] == slot, 1.0, 0.0).astype(BF16)
        out = o_ref[...] + _dot(scatter, y_ref[...])
        if final_norm:
            out = _rms_scale(out) * gf_ref[...]
        o_ref[...] = out


def _moe(a, wo, x, g, wr, br, wg, wu, wd, gf, final_norm):
    n, d = x.shape
    tm = min(MOE_TILE, n)
    ff = EXPERTS_PER_GROUP * EXPERT_FF
    slots = tm + N_GROUPS * MOE_CHUNK
    tri = jnp.tri(tm, tm, -1, dtype=BF16)
    once = pl.Buffered(1)
    tok = pl.BlockSpec((tm, d), lambda i, e: (i, 0))
    tok_once = pl.BlockSpec((tm, d), lambda i, e: (i, 0), pipeline_mode=once)
    vec = pl.BlockSpec((1, d), lambda i, e: (0, 0))
    experts = pl.BlockSpec((EXPERTS_PER_GROUP, d, EXPERT_FF), lambda i, e: (e, 0, 0))
    return pl.pallas_call(
        functools.partial(_moe_kernel, final_norm=final_norm),
        grid=(n // tm, N_GROUPS),
        in_specs=[tok_once, pl.BlockSpec((d, d), lambda i, e: (0, 0), pipeline_mode=once), tok_once, vec,
                  pl.BlockSpec((d, 2 * LANES), lambda i, e: (0, 0), pipeline_mode=once),
                  pl.BlockSpec((1, LANES), lambda i, e: (0, 0)),
                  pl.BlockSpec((tm, tm), lambda i, e: (0, 0), pipeline_mode=once),
                  experts, experts,
                  pl.BlockSpec((1, ff, d), lambda i, e: (e, 0, 0)),
                  vec],
        out_specs=tok,
        out_shape=jax.ShapeDtypeStruct((n, d), F32),
        scratch_shapes=[pltpu.VMEM((tm, d), BF16), pltpu.VMEM((tm, LANES), BF16),
                        pltpu.VMEM((tm, LANES), F32), pltpu.VMEM((LANES, tm), F32),
                        pltpu.VMEM((tm, 1), F32), pltpu.VMEM((slots, d), BF16),
                        pltpu.SMEM((1,), jnp.int32)],
        compiler_params=_params("parallel", "arbitrary"),
        name="hier_moe",
    )(a, wo, x, g, wr, br, tri, wg, wu, wd, gf)


def _kvq_b_kernel(x_ref, gkv_ref, gq_ref, wkv_ref, wq_ref, k_ref, v_ref, q_ref, *, lead_pad):
    def compute():
        xh = _rms_scale(x_ref[0])
        kv = _dot((xh * gkv_ref[...]).astype(BF16), wkv_ref[...])
        k_ref[0] = kv[:, :D_MODEL].astype(k_ref.dtype)
        v_ref[0] = kv[:, D_MODEL:].astype(v_ref.dtype)
        q = _dot((xh * gq_ref[...]).astype(BF16), wq_ref[...])
        q_ref[0] = (q * (B_DH ** -0.5 * LOG2E)).astype(BF16)

    if lead_pad:
        i = pl.program_id(1)

        @pl.when(i == 0)
        def _():
            k_ref[...] = jnp.zeros_like(k_ref)
            v_ref[...] = jnp.zeros_like(v_ref)

        pl.when(i > 0)(compute)
    else:
        compute()


def _kvq_b(x3, gkv, gq, wkv, wq, kv_dtype, lead_pad, row_block=None):
    b, s, d = x3.shape
    tm = min(ROW_TILE, s)
    if lead_pad:
        assert tm == B_WINDOW
        grid = (b, s // tm + 1)
        src = lambda bi, i: (bi, jnp.maximum(i - 1, 0), 0)
        dst = lambda bi, i: (bi, i, 0)
        s_kv, s_q = s + B_WINDOW, s
    elif row_block is not None:
        grid = (b, 1)
        src = lambda bi, i: (bi, row_block, 0)
        dst = lambda bi, i: (bi, 0, 0)
        s_kv = s_q = tm
    else:
        grid = (b, s // tm)
        src = dst = lambda bi, i: (bi, i, 0)
        s_kv = s_q = s
    vec = pl.BlockSpec((1, d), lambda bi, i: (0, 0))
    return pl.pallas_call(
        functools.partial(_kvq_b_kernel, lead_pad=lead_pad),
        grid=grid,
        in_specs=[pl.BlockSpec((1, tm, d), src), vec, vec,
                  pl.BlockSpec((d, 2 * d), lambda bi, i: (0, 0)),
                  pl.BlockSpec((d, d), lambda bi, i: (0, 0))],
        out_specs=[pl.BlockSpec((1, tm, d), dst), pl.BlockSpec((1, tm, d), dst),
                   pl.BlockSpec((1, tm, d), src if lead_pad else dst)],
        out_shape=[jax.ShapeDtypeStruct((b, s_kv, d), kv_dtype), jax.ShapeDtypeStruct((b, s_kv, d), kv_dtype),
                   jax.ShapeDtypeStruct((b, s_q, d), BF16)],
        compiler_params=_params("parallel", "arbitrary"),
        name="kvq_b",
    )(x3, gkv, gq, wkv, wq)


def _band_bias_kernel(u_ref, o_ref):
    x = jnp.broadcast_to(u_ref[0], (BAND_TQ, 2 * B_WINDOW))
    bias = pltpu.roll(x, 0, 1, stride=1, stride_axis=0)[:, :BAND_KW]
    qc = _chunk_of(lax.broadcasted_iota(jnp.int32, bias.shape, 0))
    kc = _chunk_of(lax.broadcasted_iota(jnp.int32, bias.shape, 1))
    ok = (kc >= qc) & (kc <= qc + B_WINDOW // CHUNK)
    o_ref[0] = jnp.where(ok, bias * LOG2E, NEG)


def _band_bias(rel_bias):
    edge = jnp.broadcast_to(rel_bias[:, 2 * REL_CLIP:], (B_HEADS, REL_CLIP))
    u = jnp.concatenate([edge, rel_bias[:, :0:-1], edge], axis=1)[:, None, :]
    assert u.shape[-1] == 2 * B_WINDOW
    return pl.pallas_call(
        _band_bias_kernel,
        grid=(B_HEADS,),
        in_specs=[pl.BlockSpec((1, 1, 2 * B_WINDOW), lambda h: (h, 0, 0))],
        out_specs=pl.BlockSpec((1, BAND_TQ, BAND_KW), lambda h: (h, 0, 0)),
        out_shape=jax.ShapeDtypeStruct((B_HEADS, BAND_TQ, BAND_KW), F32),
        compiler_params=_params("parallel"),
        name="band_bias",
    )(u)


def _pick_halves(o, t):
    lane = lax.broadcasted_iota(jnp.int32, (t, LANES), 1)
    return jnp.where(lane < 64, o[:t], o[t:])


def _band_prompt_kernel(q_ref, k_ref, v_ref, bias_ref, o_ref, p_ref, *, n_tiles):
    t = BAND_TQ

    def probabilities(off, lead):
        kw = k_ref[0, pl.ds(off, BAND_KW), :]
        qs = _stack_halves(q_ref[0, pl.ds(off, t), :])
        for g in range(2 * t // BAND_ROWS):
            r0 = g * BAND_ROWS
            s = _dot_nt(qs[r0:r0 + BAND_ROWS], kw) + bias_ref[r0 // t, r0 % t:r0 % t + BAND_ROWS, :]
            if lead:
                c = lax.broadcasted_iota(jnp.int32, s.shape, 1)
                s = jnp.where(c >= B_WINDOW - off, s, NEG)
            m = jnp.max(s, axis=-1, keepdims=True)
            p_ref[r0:r0 + BAND_ROWS, :] = jnp.exp2(s - m).astype(BF16)

    def outputs(off):
        vx = jnp.concatenate([v_ref[0, pl.ds(off, BAND_KW), :], jnp.ones((BAND_KW, LANES), BF16)], axis=1)
        o = _dot(p_ref[...], vx)
        o = o[:, :LANES] * (1.0 / o[:, LANES:])
        o_ref[0, pl.ds(off, t), :] = _pick_halves(o, t).astype(BF16)

    n_lead = B_WINDOW // t
    for ti in range(n_lead):
        if ti > 0:
            outputs((ti - 1) * t)
        probabilities(ti * t, True)

    def body(ti, carry):
        outputs(pl.multiple_of((ti - 1) * t, t))
        probabilities(pl.multiple_of(ti * t, t), False)
        return carry

    lax.fori_loop(n_lead, n_tiles, body, 0)
    outputs((n_tiles - 1) * t)


def _band_prompt(q, kpad, vpad, bias):
    b, s, d = q.shape
    t = BAND_TQ
    nhp = d // LANES
    win = pl.BlockSpec((1, s + B_WINDOW, LANES), lambda hp, bi: (bi, 0, hp))
    seq = pl.BlockSpec((1, s, LANES), lambda hp, bi: (bi, 0, hp))
    return pl.pallas_call(
        functools.partial(_band_prompt_kernel, n_tiles=s // t),
        grid=(nhp, b),
        in_specs=[seq, win, win, pl.BlockSpec((2, t, BAND_KW), lambda hp, bi: (hp, 0, 0))],
        out_specs=seq,
        out_shape=jax.ShapeDtypeStruct((b, s, d), BF16),
        scratch_shapes=[pltpu.VMEM((2 * t, BAND_KW), BF16)],
        compiler_params=_params("parallel", "parallel"),
        name="band_attn_prompt",
    )(q, kpad, vpad, bias)


def _band_sample_kernel(q_ref, kc_ref, kn_ref, vc_ref, vn_ref, bias_ref, o_ref):
    t = q_ref.shape[1]
    wb = kc_ref.shape[1]
    qs = _stack_halves(q_ref[0])
    kc = kc_ref[0].astype(BF16)
    vc = vc_ref[0].astype(BF16)
    kn = _pad_rows(kn_ref[0].astype(BF16), LANES)
    vn = _pad_rows(vn_ref[0].astype(BF16), LANES)
    bias = bias_ref[...].reshape(2 * t, BAND_KW)
    sc = _dot_nt(qs, kc) + bias[:, :wb]
    sn = _dot_nt(qs, kn) + bias[:, wb:wb + LANES]
    col = lax.broadcasted_iota(jnp.int32, sn.shape, 1)
    sn = jnp.where(col < t, sn, NEG)
    m = jnp.maximum(jnp.max(sc, axis=-1, keepdims=True), jnp.max(sn, axis=-1, keepdims=True))
    pc = jnp.exp2(sc - m)
    pn = jnp.exp2(sn - m)
    l = jnp.sum(pc, axis=-1, keepdims=True) + jnp.sum(pn, axis=-1, keepdims=True)
    o = (_dot(pc.astype(BF16), vc) + _dot(pn.astype(BF16), vn)) * (1.0 / l)
    o_ref[0] = _pick_halves(o, t).astype(BF16)


def _band_sample(q, kc, kn, vc, vn, bias):
    b, t, d = q.shape
    wb = kc.shape[1]
    assert wb == B_WINDOW
    new = pl.BlockSpec((1, t, LANES), lambda bi, hp: (bi, 0, hp))
    old = pl.BlockSpec((1, wb, LANES), lambda bi, hp: (bi, 0, hp))
    return pl.pallas_call(
        _band_sample_kernel,
        grid=(b, d // LANES),
        in_specs=[new, old, new, old, new, pl.BlockSpec((2, t, BAND_KW), lambda bi, hp: (hp, 0, 0))],
        out_specs=new,
        out_shape=jax.ShapeDtypeStruct((b, t, d), BF16),
        compiler_params=_params("parallel", "parallel"),
        name="band_attn_sample",
    )(q, kc, kn, vc, vn, bias)


def kernel(x_prompt, x_sample, cache_k_a, cache_v_a, cache_k_b, cache_v_b, g_attn, g_ffn, w_q_a, w_k_a, w_v_a, lam_q1, lam_k1, lam_q2, lam_k2, g_sub_a, w_o_a, g_kv, w_kv_b, w_q_b, rel_bias_b, w_o_b, w_router_g, b_router_g, w_router_e, b_router_e, w_gate, w_up, w_down, g_final):
    bp, sp, d = x_prompt.shape
    bs, ts, _ = x_sample.shape
    past = cache_k_a.shape[2]
    wb = cache_k_b.shape[1]
    n_p, n_s = bp * sp, bs * ts
    assert d == D_MODEL and sp % ROW_TILE == 0 and ROW_TILE % ts == 0 and ts <= LANES
    depth = g_attn.shape[0]
    n_a = w_q_a.shape[0]
    assert depth == 2 and n_a == 1 and w_q_b.shape[0] == 1

    row = lambda v: v.reshape(1, -1).astype(F32)
    bf = lambda w: w.astype(BF16)

    def moe_weights(l):
        wr = jnp.concatenate([w_router_g[l], w_router_e[l].transpose(1, 0, 2).reshape(d, N_EXPERTS)], axis=1)
        wr = jnp.pad(wr, ((0, 0), (0, LANES - wr.shape[1])))
        br = jnp.concatenate([b_router_g[l], b_router_e[l].reshape(-1)])
        br = jnp.pad(br, (0, LANES - br.shape[0]))[None, :]
        wr = wr.astype(F32)
        wr_hi = bf(wr)
        wr_split = jnp.concatenate([wr_hi, bf(wr - wr_hi.astype(F32))], axis=1)
        wd = bf(w_down[l]).reshape(N_GROUPS, EXPERTS_PER_GROUP * EXPERT_FF, d)
        return row(g_ffn[l]), wr_split, br.astype(F32), bf(w_gate[l]), bf(w_up[l]), wd

    xp = x_prompt.reshape(n_p, d)
    xs = x_sample.reshape(n_s, d)

    lam_init = 0.8 - 0.6 * math.exp(-0.3 * 0)
    lamv = jnp.stack([lam_q1[0], lam_k1[0], lam_q2[0], lam_k2[0]]).astype(F32)
    gsub = row(g_sub_a[0])
    wq, wk, wv, wo = bf(w_q_a[0]), bf(w_k_a[0]), bf(w_v_a[0]), bf(w_o_a[0])
    g0 = row(g_attn[0])

    cos_p, sin_p = _rope_table(sp, sp, 0, min(ROW_TILE, sp))
    qp, kfp, kbp, vfp, vbp = _qkv_a(x_prompt, g0, wq, wk, wv, cos_p, sin_p, True, A_DK ** -0.5 * LOG2E)
    ap = _diff_prompt(lamv, gsub, qp, kbp, vbp, lam_init)

    tm_s = min(ROW_TILE, n_s)
    cos_s, sin_s = _rope_table(tm_s, ts, past, tm_s)
    qs, kfs, _, vfs, _ = _qkv_a(xs.reshape(1, n_s, d), g0, wq, wk, wv, cos_s, sin_s, False, A_DK ** -0.5)
    heads_first = lambda a, n: a.reshape(bs, ts, n, d // n).transpose(0, 2, 1, 3)
    a_s = _diff_sample(lamv, gsub, heads_first(qs, 2 * A_HEADS),
                       cache_k_a[0].reshape(bs, past * 2 * A_HEADS, A_DK), heads_first(bf(kfs), 2 * A_HEADS),
                       cache_v_a[0].reshape(bs, past * A_HEADS, A_DV), heads_first(bf(vfs), A_HEADS), lam_init)

    mw = moe_weights(0)
    gfin = row(g_final)
    xp = _moe(ap.reshape(n_p, d), wo, xp, *mw, gfin, final_norm=False)
    xs = _moe(a_s.reshape(n_s, d), wo, xs, *mw, gfin, final_norm=False)

    gkv, g1 = row(g_kv), row(g_attn[1])
    wkv, wqb, wob = bf(w_kv_b), bf(w_q_b[0]), bf(w_o_b[0])
    bias = _band_bias(rel_bias_b[0].astype(F32))

    xp3 = xp.reshape(bp, sp, d)
    kpad, vpad, qb = _kvq_b(xp3, gkv, g1, wkv, wqb, BF16, lead_pad=True)
    kbt, vbt, _ = _kvq_b(xp3, gkv, g1, wkv, wqb, F32, lead_pad=False, row_block=sp // ROW_TILE - 1)
    ab = _band_prompt(qb, kpad, vpad, bias)

    kns, vns, qbs = _kvq_b(xs.reshape(1, n_s, d), gkv, g1, wkv, wqb, F32, lead_pad=False)
    kns3, vns3 = kns.reshape(bs, ts, d), vns.reshape(bs, ts, d)
    abs_ = _band_sample(qbs.reshape(bs, ts, d), cache_k_b.reshape(bs, wb, d), kns3,
                        cache_v_b.reshape(bs, wb, d), vns3, bias)

    mw = moe_weights(1)
    y_prompt = _moe(ab.reshape(n_p, d), wob, xp, *mw, gfin, final_norm=True).reshape(bp, sp, d)
    y_sample = _moe(abs_.reshape(n_s, d), wob, xs, *mw, gfin, final_norm=True).reshape(bs, ts, d)

    wp = min(B_WINDOW, sp)
    assert wp == ROW_TILE
    k_a_prompt = kfp.reshape(1, bp, sp, 2 * A_HEADS, A_DK)
    v_a_prompt = vfp.reshape(1, bp, sp, A_HEADS, A_DV)
    k_a_sample = kfs.reshape(1, bs, ts, 2 * A_HEADS, A_DK)
    v_a_sample = vfs.reshape(1, bs, ts, A_HEADS, A_DV)
    k_b_prompt = kbt.reshape(bp, wp, B_HEADS, B_DH)
    v_b_prompt = vbt.reshape(bp, wp, B_HEADS, B_DH)
    k_b_sample = jnp.concatenate([cache_k_b[:, ts:], kns.reshape(bs, ts, B_HEADS, B_DH)], axis=1)
    v_b_sample = jnp.concatenate([cache_v_b[:, ts:], vns.reshape(bs, ts, B_HEADS, B_DH)], axis=1)
    return (y_prompt, y_sample, k_a_prompt, v_a_prompt, k_b_prompt, v_b_prompt,
            k_a_sample, v_a_sample, k_b_sample, v_b_sample)
```

```python
import functools
import math

import jax
import jax.numpy as jnp
from jax import lax
from jax.experimental import pallas as pl
from jax.experimental.pallas import tpu as pltpu

F32 = jnp.float32
BF16 = jnp.bfloat16

EPS = 1e-6
CHUNK = 64
D_MODEL = 1024
A_HEADS = 8
A_DK = 64
A_DV = 128
B_HEADS = 16
B_DH = 64
B_WINDOW = 512
REL_CLIP = 256
N_GROUPS = 4
EXPERTS_PER_GROUP = 4
N_EXPERTS = 16
EXPERT_FF = 512
ROPE_THETA = 10000.0

LANES = 128
NEG = -1e30
LOG2E = math.log2(math.e)
VMEM_LIMIT = 56 * 1024 * 1024

ROW_TILE = 512
MOE_TILE = 1024
MOE_CHUNK = 128
HEADS_PER_STEP = 4
ONES_ROWS = 16
BAND_TQ = 256
BAND_KW = BAND_TQ + B_WINDOW
BAND_ROWS = 128
SAMPLE_KV_ROWS = 512


def _params(*sem):
    return pltpu.CompilerParams(dimension_semantics=sem, vmem_limit_bytes=VMEM_LIMIT)


def _rms_scale(x):
    return x * lax.rsqrt(jnp.mean(x * x, axis=-1, keepdims=True) + EPS)


def _dot(a, b):
    return jnp.dot(a, b, preferred_element_type=F32)


def _dot_nt(a, b):
    return lax.dot_general(a, b, (((1,), (1,)), ((), ())), preferred_element_type=F32)


def _chunk_of(pos):
    return jnp.right_shift(pos, CHUNK.bit_length() - 1)


def _stack_halves(q):
    lane = lax.broadcasted_iota(jnp.int32, q.shape, 1)
    zero = jnp.zeros_like(q)
    return jnp.concatenate([jnp.where(lane < 64, q, zero), jnp.where(lane >= 64, q, zero)], axis=0)


def _rope_table_kernel(inv_ref, cos_ref, sin_ref, *, period, offset, rows):
    i = pl.program_id(0)
    row = lax.broadcasted_iota(jnp.int32, (rows, LANES), 0) + i * rows
    lane = lax.broadcasted_iota(jnp.int32, (rows, LANES), 1)
    pos = (offset + lax.rem(row, period)).astype(F32)
    ang = pos * inv_ref[...]
    s = jnp.sin(ang)
    cos_ref[...] = jnp.cos(ang)
    sin_ref[...] = jnp.where(lax.rem(lane, 64) < 32, -s, s)


def _rope_table(n_rows, period, offset, rows):
    half = A_DK // 2
    inv = jnp.power(ROPE_THETA, -jnp.arange(half, dtype=F32) / half)
    inv = jnp.tile(inv, LANES // half)[None, :]
    return pl.pallas_call(
        functools.partial(_rope_table_kernel, period=period, offset=offset, rows=rows),
        grid=(n_rows // rows,),
        in_specs=[pl.BlockSpec((1, LANES), lambda i: (0, 0))],
        out_specs=[pl.BlockSpec((rows, LANES), lambda i: (i, 0))] * 2,
        out_shape=[jax.ShapeDtypeStruct((n_rows, LANES), F32)] * 2,
        compiler_params=_params("arbitrary"),
        name="rope_table",
    )(inv)


def _qkv_a_kernel(x_ref, g_ref, wq_ref, wk_ref, wv_ref, cos_ref, sin_ref,
                  q_ref, kf_ref, kb_ref, vf_ref, vb_ref, *, transposed, q_scale):
    h = (_rms_scale(x_ref[0]) * g_ref[...]).astype(BF16)
    cos = cos_ref[...]
    sin = sin_ref[...]
    lane = lax.broadcasted_iota(jnp.int32, cos.shape, 1)
    first = lax.rem(lane, 64) < 32

    yq = _dot(h, wq_ref[...])
    yk = _dot(h, wk_ref[...])
    yv = _dot(h, wv_ref[...])
    for c in range(D_MODEL // LANES):
        sl = slice(c * LANES, (c + 1) * LANES)

        def rope(y):
            yc = y[:, sl]
            rot = jnp.where(first, pltpu.roll(yc, LANES - 32, 1), pltpu.roll(yc, 32, 1))
            return yc * cos + rot * sin

        rq = rope(yq) * q_scale
        rk = rope(yk)
        vc = yv[:, sl]
        kf_ref[0, :, sl] = rk
        vf_ref[0, :, sl] = vc
        if transposed:
            q_ref[0, c] = rq.T.astype(BF16)
            kb_ref[0, c] = rk.astype(BF16)
            vb_ref[0, c, 0, :A_DV] = vc.T.astype(BF16)
            vb_ref[0, c, 0, A_DV:] = jnp.ones((ONES_ROWS, vc.shape[0]), BF16)
        else:
            q_ref[0, :, sl] = rq.astype(BF16)
            kb_ref[0, :, sl] = rk.astype(BF16)
            vb_ref[0, :, sl] = vc.astype(BF16)


def _qkv_a(x3, g, wq, wk, wv, cos, sin, transposed, q_scale):
    b, s, d = x3.shape
    tm = min(ROW_TILE, s)
    n_tab = cos.shape[0] // tm
    nh = d // LANES
    tok = pl.BlockSpec((1, tm, d), lambda bi, i: (bi, i, 0))
    fshape = jax.ShapeDtypeStruct((b, s, d), F32)
    if transposed:
        hspecs = [pl.BlockSpec((1, nh, LANES, tm), lambda bi, i: (bi, 0, 0, i)),
                  pl.BlockSpec((1, nh, tm, LANES), lambda bi, i: (bi, 0, i, 0)),
                  pl.BlockSpec((1, nh, 1, A_DV + ONES_ROWS, tm), lambda bi, i: (bi, 0, i, 0, 0))]
        hshapes = [jax.ShapeDtypeStruct((b, nh, LANES, s), BF16),
                   jax.ShapeDtypeStruct((b, nh, s, LANES), BF16),
                   jax.ShapeDtypeStruct((b, nh, s // tm, A_DV + ONES_ROWS, tm), BF16)]
    else:
        hspecs = [tok] * 3
        hshapes = [jax.ShapeDtypeStruct((b, s, d), BF16)] * 3
    wspec = pl.BlockSpec((d, d), lambda bi, i: (0, 0))
    tab = pl.BlockSpec((tm, LANES), lambda bi, i: (i % n_tab, 0))
    return pl.pallas_call(
        functools.partial(_qkv_a_kernel, transposed=transposed, q_scale=q_scale),
        grid=(b, s // tm),
        in_specs=[tok, pl.BlockSpec((1, d), lambda bi, i: (0, 0)), wspec, wspec, wspec, tab, tab],
        out_specs=[hspecs[0], tok, hspecs[1], tok, hspecs[2]],
        out_shape=[hshapes[0], fshape, hshapes[1], fshape, hshapes[2]],
        compiler_params=_params("parallel", "parallel"),
        name="qkv_a",
    )(x3, g, wq, wk, wv, cos, sin)


def _diff_lambda(lam_ref, lam_init):
    lv = lam_ref[...]
    s1 = jnp.sum(lv[0:1] * lv[1:2], axis=-1, keepdims=True)
    s2 = jnp.sum(lv[2:3] * lv[3:4], axis=-1, keepdims=True)
    return jnp.exp(s1) - jnp.exp(s2) + lam_init


def _diff_finish(o1, o2, lam, gsub, lam_init):
    d = o1 - lam * o2
    return (_rms_scale(d) * gsub * (1.0 - lam_init)).astype(BF16)


def _diff_prompt_kernel(lam_ref, gsub_ref, qt_ref, k_ref, vt_ref, o_ref, s_ref, p_ref, *, t, lam_init):
    qi = pl.program_id(2)
    nh = qt_ref.shape[1]
    row = lax.broadcasted_iota(jnp.int32, (A_DV, t), 0)
    zero = jnp.zeros((A_DV, t), BF16)
    qst = [jnp.concatenate([jnp.where(row < A_DK, qt_ref[0, h], zero), jnp.where(row >= A_DK, qt_ref[0, h], zero)],
                           axis=1) for h in range(nh)]

    def scores(h, tile, m_prev, mask):
        s = _dot(k_ref[0, h, pl.ds(pl.multiple_of(tile * t, t), t), :], qst[h])
        if mask is not None:
            s = jnp.where(mask, s, NEG)
        m = jnp.maximum(m_prev, jnp.max(s, axis=0, keepdims=True))
        s_ref[h] = s
        return m, jnp.exp2(m_prev - m)

    def accumulate(h, tile, alpha, acc):
        return alpha * acc + _dot(vt_ref[0, h, tile], p_ref[h])

    def exponentials(h, m):
        p_ref[h] = jnp.exp2(s_ref[h] - m).astype(BF16)

    kk = lax.broadcasted_iota(jnp.int32, (t, 2 * t), 0)
    qq = lax.broadcasted_iota(jnp.int32, (t, 2 * t), 1)
    mask = _chunk_of(kk) <= _chunk_of(jnp.bitwise_and(qq, t - 1))
    m0 = jnp.full((1, 2 * t), NEG, F32)
    carry = []
    for h in range(nh):
        m, alpha = scores(h, qi, m0, mask)
        p_ref[h] = jnp.zeros((t, 2 * t), BF16)
        carry.append((m, alpha, jnp.ones((1, 2 * t), F32), jnp.zeros((A_DV + ONES_ROWS, 2 * t), F32)))

    def v_tile(n):
        return jnp.where(n == 0, qi, jnp.maximum(n - 1, 0))

    def body(j, carry):
        out = []
        for h in range(nh):
            m_s, alpha_s, alpha_p, acc = carry[h]
            acc = accumulate(h, v_tile(j - 1), alpha_p, acc)
            exponentials(h, m_s)
            m_new, alpha_new = scores(h, j, m_s, None)
            out.append((m_new, alpha_new, alpha_s, acc))
        return tuple(out)

    carry = lax.fori_loop(0, qi // 2, lambda j, c: body(2 * j + 1, body(2 * j, c)), tuple(carry))
    carry = lax.cond(qi % 2 == 1, lambda c: body(qi - 1, c), lambda c: c, carry)
    lam = _diff_lambda(lam_ref, lam_init)
    for h in range(nh):
        m_s, alpha_s, alpha_p, acc = carry[h]
        acc = accumulate(h, v_tile(qi - 1), alpha_p, acc)
        exponentials(h, m_s)
        acc = accumulate(h, v_tile(qi), alpha_s, acc)
        o = (acc[:A_DV] * (1.0 / acc[A_DV:A_DV + 1])).T
        o_ref[0, :, h * A_DV:(h + 1) * A_DV] = _diff_finish(o[:t], o[t:], lam, gsub_ref[...], lam_init)


def _diff_prompt(lamv, gsub, qt, k, vt, lam_init):
    b, nh, _, s = qt.shape
    t = vt.shape[-1]
    hs = HEADS_PER_STEP
    return pl.pallas_call(
        functools.partial(_diff_prompt_kernel, t=t, lam_init=lam_init),
        grid=(b, nh // hs, s // t),
        in_specs=[pl.BlockSpec(lamv.shape, lambda bi, hi, qi: (0, 0)),
                  pl.BlockSpec((1, A_DV), lambda bi, hi, qi: (0, 0)),
                  pl.BlockSpec((1, hs, LANES, t), lambda bi, hi, qi: (bi, hi, 0, qi)),
                  pl.BlockSpec((1, hs, s, LANES), lambda bi, hi, qi: (bi, hi, 0, 0), pipeline_mode=pl.Buffered(1)),
                  pl.BlockSpec((1, hs, s // t, A_DV + ONES_ROWS, t), lambda bi, hi, qi: (bi, hi, 0, 0, 0),
                               pipeline_mode=pl.Buffered(1))],
        out_specs=pl.BlockSpec((1, t, hs * LANES), lambda bi, hi, qi: (bi, qi, hi)),
        out_shape=jax.ShapeDtypeStruct((b, s, nh * LANES), BF16),
        scratch_shapes=[pltpu.VMEM((hs, t, 2 * t), F32), pltpu.VMEM((hs, t, 2 * t), BF16)],
        compiler_params=_params("parallel", "parallel", "arbitrary"),
        name="diff_attn_prompt",
    )(lamv, gsub, qt, k, vt)


def _pad_rows(x, rows):
    return jnp.concatenate([x, jnp.zeros((rows - x.shape[0], x.shape[1]), x.dtype)], axis=0)


def _diff_sample_kernel(lam_ref, gsub_ref, q_ref, kc_ref, kn_ref, vc_ref, vn_ref, o_ref, m_ref, l_ref, acc_ref,
                        *, lam_init, rows):
    c = pl.program_id(1)
    t = q_ref.shape[2]
    n_sub = q_ref.shape[1]

    @pl.when(c == 0)
    def _():
        m_ref[...] = jnp.full(m_ref.shape, NEG, F32)
        l_ref[...] = jnp.zeros(l_ref.shape, F32)
        acc_ref[...] = jnp.zeros(acc_ref.shape, F32)

    def update(keys, values, mask):
        s = jnp.concatenate([_dot_nt(q_ref[0, j], keys(j)) for j in range(n_sub)], axis=0)
        if mask is not None:
            s = jnp.where(mask(s.shape), s, NEG)
        m_old = m_ref[...]
        m = jnp.maximum(m_old, jnp.max(s, axis=-1, keepdims=True))
        p = jnp.exp(s - m)
        alpha = jnp.exp(m_old - m)
        l_ref[...] = alpha * l_ref[...] + jnp.sum(p, axis=-1, keepdims=True)
        p = p.astype(BF16)
        pv = jnp.concatenate([_dot(p[2 * t * h:2 * t * (h + 1)], values(h)) for h in range(A_HEADS)], axis=0)
        acc_ref[...] = alpha * acc_ref[...] + pv
        m_ref[...] = m

    update(lambda j: kc_ref[0, :, j, :].astype(BF16),
           lambda h: vc_ref[0, pl.ds(h, rows, stride=A_HEADS), :].astype(BF16), None)

    @pl.when(c == pl.num_programs(1) - 1)
    def _():
        update(lambda j: _pad_rows(kn_ref[0, j], LANES), lambda h: _pad_rows(vn_ref[0, h], LANES),
               lambda shape: lax.broadcasted_iota(jnp.int32, shape, 1) < t)
        lam = _diff_lambda(lam_ref, lam_init)
        o = acc_ref[...] * (1.0 / l_ref[...])
        for h in range(A_HEADS):
            r = 2 * t * h
            o_ref[0, :, h * A_DV:(h + 1) * A_DV] = _diff_finish(o[r:r + t], o[r + t:r + 2 * t], lam, gsub_ref[...],
                                                                lam_init)


def _diff_sample(lamv, gsub, q, kc, kn, vc, vn, lam_init):
    b, n_sub, t, dk = q.shape
    past = vc.shape[1] // A_HEADS
    rows = min(SAMPLE_KV_ROWS, past)
    fixed = lambda *shape: pl.BlockSpec((1,) + shape, lambda bi, c: (bi,) + (0,) * len(shape))
    return pl.pallas_call(
        functools.partial(_diff_sample_kernel, lam_init=lam_init, rows=rows),
        grid=(b, past // rows),
        in_specs=[pl.BlockSpec(lamv.shape, lambda bi, c: (0, 0)),
                  pl.BlockSpec((1, A_DV), lambda bi, c: (0, 0)),
                  fixed(n_sub, t, dk),
                  pl.BlockSpec((1, rows, n_sub, dk), lambda bi, c: (bi, c, 0, 0)),
                  fixed(n_sub, t, dk),
                  pl.BlockSpec((1, rows * A_HEADS, A_DV), lambda bi, c: (bi, c, 0)),
                  fixed(A_HEADS, t, A_DV)],
        out_specs=fixed(t, A_HEADS * A_DV),
        out_shape=jax.ShapeDtypeStruct((b, t, A_HEADS * A_DV), BF16),
        scratch_shapes=[pltpu.VMEM((A_HEADS * 2 * t, 1), F32), pltpu.VMEM((A_HEADS * 2 * t, 1), F32),
                        pltpu.VMEM((A_HEADS * 2 * t, A_DV), F32)],
        compiler_params=_params("parallel", "arbitrary"),
        name="diff_attn_sample",
    )(lamv, gsub, q, kc, kn, vc, vn)


def _moe_gates(hf, wr_ref, br_ref):
    hi = hf.astype(BF16)
    lo = (hf - hi.astype(F32)).astype(BF16)
    both = _dot(hi, wr_ref[...])
    lg = both[:, :LANES] + both[:, LANES:] + _dot(lo, wr_ref[:, :LANES]) + br_ref[...]
    lane = lax.broadcasted_iota(jnp.int32, lg.shape, 1).astype(F32)
    big = float(LANES)

    def first_argmax(v):
        mx = jnp.max(v, axis=-1, keepdims=True)
        return mx, jnp.min(jnp.where(v == mx, lane, big), axis=-1, keepdims=True)

    is_g = lane < N_GROUPS
    gmax, gidx = first_argmax(jnp.where(is_g, lg, NEG))
    gval = 1.0 / jnp.sum(jnp.where(is_g, jnp.exp(lg - gmax), 0.0), axis=-1, keepdims=True)
    lo = N_GROUPS + EXPERTS_PER_GROUP * gidx
    el = jnp.where((lane >= lo) & (lane < lo + EXPERTS_PER_GROUP), lg, NEG)
    v1, i1 = first_argmax(el)
    v2, i2 = first_argmax(jnp.where(lane == i1, NEG, el))
    e2 = jnp.exp(v2 - v1)
    inv = gval / (1.0 + e2)
    gates = jnp.where(lane == i1 - lo, inv, 0.0) + jnp.where(lane == i2 - lo, inv * e2, 0.0)
    return jnp.where(lane == gidx, 1.0, 0.0), gates


def _moe_kernel(a_ref, wo_ref, x_ref, g_ref, wr_ref, br_ref, tri_ref, wg_ref, wu_ref, wd_ref, gf_ref, o_ref,
                h_ref, gsplit_ref, rank_ref, rank_t_ref, pos_ref, y_ref, base_ref, *, final_norm):
    grp = pl.program_id(1)
    nb = x_ref.shape[0]

    @pl.when(grp == 0)
    def _():
        x = x_ref[...] + _dot(a_ref[...], wo_ref[...])
        o_ref[...] = x
        hf = _rms_scale(x) * g_ref[...]
        h_ref[...] = hf.astype(BF16)
        member, gates = _moe_gates(hf, wr_ref, br_ref)
        rest = gates
        split = jnp.zeros_like(gates)
        for i in range(3):
            part = rest.astype(BF16).astype(F32)
            split = split + pltpu.roll(part, EXPERTS_PER_GROUP * i, 1)
            rest = rest - part
        gsplit_ref[...] = split.astype(BF16)
        rank = jnp.where(member > 0.0, _dot(tri_ref[...], member.astype(BF16)) + 1.0, 0.0)
        rank_ref[...] = rank
        rank_t_ref[...] = rank.T
        pos_ref[...] = jnp.full(pos_ref.shape, -1.0, F32)
        y_ref[...] = jnp.zeros_like(y_ref)
        base_ref[0] = 0

    base = base_ref[0]
    rrow = rank_t_ref[pl.ds(grp, 1), :]
    n_chunks = (jnp.max(rrow).astype(jnp.int32) + MOE_CHUNK - 1) // MOE_CHUNK
    lane = lax.broadcasted_iota(jnp.int32, (nb, LANES), 1)
    rcol = jnp.sum(jnp.where(lane == grp, rank_ref[...], 0.0), axis=-1, keepdims=True)
    pos_ref[...] = jnp.where(rcol > 0.0, base.astype(F32) + rcol - 1.0, pos_ref[...])

    def chunk(c, carry):
        want = (lax.broadcasted_iota(jnp.int32, (MOE_CHUNK, nb), 0) + (c * MOE_CHUNK + 1)).astype(F32)
        onehot = jnp.where(rrow == want, 1.0, 0.0).astype(BF16)
        xg = _dot(onehot, h_ref[...]).astype(BF16)
        gs3 = _dot(onehot, gsplit_ref[...])
        gs = (gs3 + pltpu.roll(gs3, LANES - EXPERTS_PER_GROUP, 1)
              + pltpu.roll(gs3, LANES - 2 * EXPERTS_PER_GROUP, 1))
        parts = []
        for j in range(EXPERTS_PER_GROUP):
            a = _dot(xg, wg_ref[j])
            u = _dot(xg, wu_ref[j])
            parts.append((a * (1.0 / (1.0 + jnp.exp(-a))) * u * gs[:, j:j + 1]).astype(BF16))
        y = _dot(jnp.concatenate(parts, axis=1), wd_ref[0])
        off = pl.multiple_of(base + c * MOE_CHUNK, MOE_CHUNK)
        y_ref[pl.ds(off, MOE_CHUNK), :] = y.astype(BF16)
        return carry

    lax.fori_loop(0, n_chunks, chunk, 0)
    base_ref[0] = base + n_chunks * MOE_CHUNK

    @pl.when(grp == N_GROUPS - 1)
    def _():
        slot = lax.broadcasted_iota(jnp.int32, (nb, y_ref.shape[0]), 1).astype(F32)
        scatter = jnp.where(pos_ref[...] == slot, 1.0, 0.0).astype(BF16)
        out = o_ref[...] + _dot(scatter, y_ref[...])
        if final_norm:
            out = _rms_scale(out) * gf_ref[...]
        o_ref[...] = out


def _moe(a, wo, x, g, wr, br, wg, wu, wd, gf, final_norm):
    n, d = x.shape
    tm = min(MOE_TILE, n)
    ff = EXPERTS_PER_GROUP * EXPERT_FF
    slots = tm + N_GROUPS * MOE_CHUNK
    tri = jnp.tri(tm, tm, -1, dtype=BF16)
    once = pl.Buffered(1)
    tok = pl.BlockSpec((tm, d), lambda i, e: (i, 0))
    tok_once = pl.BlockSpec((tm, d), lambda i, e: (i, 0), pipeline_mode=once)
    vec = pl.BlockSpec((1, d), lambda i, e: (0, 0))
    experts = pl.BlockSpec((EXPERTS_PER_GROUP, d, EXPERT_FF), lambda i, e: (e, 0, 0))
    return pl.pallas_call(
        functools.partial(_moe_kernel, final_norm=final_norm),
        grid=(n // tm, N_GROUPS),
        in_specs=[tok_once, pl.BlockSpec((d, d), lambda i, e: (0, 0), pipeline_mode=once), tok_once, vec,
                  pl.BlockSpec((d, 2 * LANES), lambda i, e: (0, 0), pipeline_mode=once),
                  pl.BlockSpec((1, LANES), lambda i, e: (0, 0)),
                  pl.BlockSpec((tm, tm), lambda i, e: (0, 0), pipeline_mode=once),
                  experts, experts,
                  pl.BlockSpec((1, ff, d), lambda i, e: (e, 0, 0)),
                  vec],
        out_specs=tok,
        out_shape=jax.ShapeDtypeStruct((n, d), F32),
        scratch_shapes=[pltpu.VMEM((tm, d), BF16), pltpu.VMEM((tm, LANES), BF16),
                        pltpu.VMEM((tm, LANES), F32), pltpu.VMEM((LANES, tm), F32),
                        pltpu.VMEM((tm, 1), F32), pltpu.VMEM((slots, d), BF16),
                        pltpu.SMEM((1,), jnp.int32)],
        compiler_params=_params("parallel", "arbitrary"),
        name="hier_moe",
    )(a, wo, x, g, wr, br, tri, wg, wu, wd, gf)


def _kvq_b_kernel(x_ref, gkv_ref, gq_ref, wkv_ref, wq_ref, k_ref, v_ref, q_ref, *, lead_pad):
    def compute():
        xh = _rms_scale(x_ref[0])
        kv = _dot((xh * gkv_ref[...]).astype(BF16), wkv_ref[...])
        k_ref[0] = kv[:, :D_MODEL].astype(k_ref.dtype)
        v_ref[0] = kv[:, D_MODEL:].astype(v_ref.dtype)
        q = _dot((xh * gq_ref[...]).astype(BF16), wq_ref[...])
        q_ref[0] = (q * (B_DH ** -0.5 * LOG2E)).astype(BF16)

    if lead_pad:
        i = pl.program_id(1)

        @pl.when(i == 0)
        def _():
            k_ref[...] = jnp.zeros_like(k_ref)
            v_ref[...] = jnp.zeros_like(v_ref)

        pl.when(i > 0)(compute)
    else:
        compute()


def _kvq_b(x3, gkv, gq, wkv, wq, kv_dtype, lead_pad, row_block=None):
    b, s, d = x3.shape
    tm = min(ROW_TILE, s)
    if lead_pad:
        assert tm == B_WINDOW
        grid = (b, s // tm + 1)
        src = lambda bi, i: (bi, jnp.maximum(i - 1, 0), 0)
        dst = lambda bi, i: (bi, i, 0)
        s_kv, s_q = s + B_WINDOW, s
    elif row_block is not None:
        grid = (b, 1)
        src = lambda bi, i: (bi, row_block, 0)
        dst = lambda bi, i: (bi, 0, 0)
        s_kv = s_q = tm
    else:
        grid = (b, s // tm)
        src = dst = lambda bi, i: (bi, i, 0)
        s_kv = s_q = s
    vec = pl.BlockSpec((1, d), lambda bi, i: (0, 0))
    return pl.pallas_call(
        functools.partial(_kvq_b_kernel, lead_pad=lead_pad),
        grid=grid,
        in_specs=[pl.BlockSpec((1, tm, d), src), vec, vec,
                  pl.BlockSpec((d, 2 * d), lambda bi, i: (0, 0)),
                  pl.BlockSpec((d, d), lambda bi, i: (0, 0))],
        out_specs=[pl.BlockSpec((1, tm, d), dst), pl.BlockSpec((1, tm, d), dst),
                   pl.BlockSpec((1, tm, d), src if lead_pad else dst)],
        out_shape=[jax.ShapeDtypeStruct((b, s_kv, d), kv_dtype), jax.ShapeDtypeStruct((b, s_kv, d), kv_dtype),
                   jax.ShapeDtypeStruct((b, s_q, d), BF16)],
        compiler_params=_params("parallel", "arbitrary"),
        name="kvq_b",
    )(x3, gkv, gq, wkv, wq)


def _band_bias_kernel(u_ref, o_ref):
    x = jnp.broadcast_to(u_ref[0], (BAND_TQ, 2 * B_WINDOW))
    bias = pltpu.roll(x, 0, 1, stride=1, stride_axis=0)[:, :BAND_KW]
    qc = _chunk_of(lax.broadcasted_iota(jnp.int32, bias.shape, 0))
    kc = _chunk_of(lax.broadcasted_iota(jnp.int32, bias.shape, 1))
    ok = (kc >= qc) & (kc <= qc + B_WINDOW // CHUNK)
    o_ref[0] = jnp.where(ok, bias * LOG2E, NEG)


def _band_bias(rel_bias):
    edge = jnp.broadcast_to(rel_bias[:, 2 * REL_CLIP:], (B_HEADS, REL_CLIP))
    u = jnp.concatenate([edge, rel_bias[:, :0:-1], edge], axis=1)[:, None, :]
    assert u.shape[-1] == 2 * B_WINDOW
    return pl.pallas_call(
        _band_bias_kernel,
        grid=(B_HEADS,),
        in_specs=[pl.BlockSpec((1, 1, 2 * B_WINDOW), lambda h: (h, 0, 0))],
        out_specs=pl.BlockSpec((1, BAND_TQ, BAND_KW), lambda h: (h, 0, 0)),
        out_shape=jax.ShapeDtypeStruct((B_HEADS, BAND_TQ, BAND_KW), F32),
        compiler_params=_params("parallel"),
        name="band_bias",
    )(u)


def _pick_halves(o, t):
    lane = lax.broadcasted_iota(jnp.int32, (t, LANES), 1)
    return jnp.where(lane < 64, o[:t], o[t:])


def _band_prompt_kernel(q_ref, k_ref, v_ref, bias_ref, o_ref, p_ref, *, n_tiles):
    t = BAND_TQ

    def probabilities(off, lead):
        kw = k_ref[0, pl.ds(off, BAND_KW), :]
        qs = _stack_halves(q_ref[0, pl.ds(off, t), :])
        for g in range(2 * t // BAND_ROWS):
            r0 = g * BAND_ROWS
            s = _dot_nt(qs[r0:r0 + BAND_ROWS], kw) + bias_ref[r0 // t, r0 % t:r0 % t + BAND_ROWS, :]
            if lead:
                c = lax.broadcasted_iota(jnp.int32, s.shape, 1)
                s = jnp.where(c >= B_WINDOW - off, s, NEG)
            m = jnp.max(s, axis=-1, keepdims=True)
            p_ref[r0:r0 + BAND_ROWS, :] = jnp.exp2(s - m).astype(BF16)

    def outputs(off):
        vx = jnp.concatenate([v_ref[0, pl.ds(off, BAND_KW), :], jnp.ones((BAND_KW, LANES), BF16)], axis=1)
        o = _dot(p_ref[...], vx)
        o = o[:, :LANES] * (1.0 / o[:, LANES:])
        o_ref[0, pl.ds(off, t), :] = _pick_halves(o, t).astype(BF16)

    n_lead = B_WINDOW // t
    for ti in range(n_lead):
        if ti > 0:
            outputs((ti - 1) * t)
        probabilities(ti * t, True)

    def body(ti, carry):
        outputs(pl.multiple_of((ti - 1) * t, t))
        probabilities(pl.multiple_of(ti * t, t), False)
        return carry

    lax.fori_loop(n_lead, n_tiles, body, 0)
    outputs((n_tiles - 1) * t)


def _band_prompt(q, kpad, vpad, bias):
    b, s, d = q.shape
    t = BAND_TQ
    nhp = d // LANES
    win = pl.BlockSpec((1, s + B_WINDOW, LANES), lambda hp, bi: (bi, 0, hp))
    seq = pl.BlockSpec((1, s, LANES), lambda hp, bi: (bi, 0, hp))
    return pl.pallas_call(
        functools.partial(_band_prompt_kernel, n_tiles=s // t),
        grid=(nhp, b),
        in_specs=[seq, win, win, pl.BlockSpec((2, t, BAND_KW), lambda hp, bi: (hp, 0, 0))],
        out_specs=seq,
        out_shape=jax.ShapeDtypeStruct((b, s, d), BF16),
        scratch_shapes=[pltpu.VMEM((2 * t, BAND_KW), BF16)],
        compiler_params=_params("parallel", "parallel"),
        name="band_attn_prompt",
    )(q, kpad, vpad, bias)


def _band_sample_kernel(q_ref, kc_ref, kn_ref, vc_ref, vn_ref, bias_ref, o_ref):
    t = q_ref.shape[1]
    wb = kc_ref.shape[1]
    col = lax.broadcasted_iota(jnp.int32, (2 * t, LANES), 1)
    for hp in range(q_ref.shape[2] // LANES):
        lanes = slice(hp * LANES, (hp + 1) * LANES)
        qs = _stack_halves(q_ref[0, :, lanes])
        kn = _pad_rows(kn_ref[0, :, lanes].astype(BF16), LANES)
        vn = _pad_rows(vn_ref[0, :, lanes].astype(BF16), LANES)
        bias = bias_ref[2 * hp:2 * hp + 2].reshape(2 * t, BAND_KW)
        sc = _dot_nt(qs, kc_ref[0, :, lanes].astype(BF16)) + bias[:, :wb]
        sn = jnp.where(col < t, _dot_nt(qs, kn) + bias[:, wb:wb + LANES], NEG)
        m = jnp.maximum(jnp.max(sc, axis=-1, keepdims=True), jnp.max(sn, axis=-1, keepdims=True))
        pc = jnp.exp2(sc - m)
        pn = jnp.exp2(sn - m)
        l = jnp.sum(pc, axis=-1, keepdims=True) + jnp.sum(pn, axis=-1, keepdims=True)
        o = (_dot(pc.astype(BF16), vc_ref[0, :, lanes].astype(BF16)) + _dot(pn.astype(BF16), vn)) * (1.0 / l)
        o_ref[0, :, lanes] = _pick_halves(o, t).astype(BF16)


def _band_sample(q, kc, kn, vc, vn, bias):
    b, t, d = q.shape
    wb = kc.shape[1]
    assert wb == B_WINDOW
    new = pl.BlockSpec((1, t, d), lambda bi: (bi, 0, 0))
    old = pl.BlockSpec((1, wb, d), lambda bi: (bi, 0, 0))
    return pl.pallas_call(
        _band_sample_kernel,
        grid=(b,),
        in_specs=[new, old, new, old, new, pl.BlockSpec((B_HEADS, t, BAND_KW), lambda bi: (0, 0, 0))],
        out_specs=new,
        out_shape=jax.ShapeDtypeStruct((b, t, d), BF16),
        compiler_params=_params("parallel"),
        name="band_attn_sample",
    )(q, kc, kn, vc, vn, bias)


def _roll_buffers_kernel(ck_ref, nk_ref, cv_ref, nv_ref, ok_ref, ov_ref):
    t = nk_ref.shape[1]
    wb = ck_ref.shape[1]
    for c_ref, n_ref, o_ref in ((ck_ref, nk_ref, ok_ref), (cv_ref, nv_ref, ov_ref)):
        o_ref[0, :wb - t] = c_ref[0, t:]
        o_ref[0, wb - t:] = n_ref[0]


def _roll_buffers(ck, nk, cv, nv):
    b, wb, nh, dh = ck.shape
    t = nk.shape[1]
    old = pl.BlockSpec((1, wb, nh, dh), lambda bi: (bi, 0, 0, 0))
    new = pl.BlockSpec((1, t, nh, dh), lambda bi: (bi, 0, 0, 0))
    return pl.pallas_call(
        _roll_buffers_kernel,
        grid=(b,),
        in_specs=[old, new, old, new],
        out_specs=[old, old],
        out_shape=[jax.ShapeDtypeStruct(ck.shape, ck.dtype)] * 2,
        compiler_params=_params("parallel"),
        name="roll_buffers",
    )(ck, nk, cv, nv)


def kernel(x_prompt, x_sample, cache_k_a, cache_v_a, cache_k_b, cache_v_b, g_attn, g_ffn, w_q_a, w_k_a, w_v_a, lam_q1, lam_k1, lam_q2, lam_k2, g_sub_a, w_o_a, g_kv, w_kv_b, w_q_b, rel_bias_b, w_o_b, w_router_g, b_router_g, w_router_e, b_router_e, w_gate, w_up, w_down, g_final):
    bp, sp, d = x_prompt.shape
    bs, ts, _ = x_sample.shape
    past = cache_k_a.shape[2]
    wb = cache_k_b.shape[1]
    n_p, n_s = bp * sp, bs * ts
    assert d == D_MODEL and sp % ROW_TILE == 0 and ROW_TILE % ts == 0 and ts <= LANES
    depth = g_attn.shape[0]
    n_a = w_q_a.shape[0]
    assert depth == 2 and n_a == 1 and w_q_b.shape[0] == 1

    row = lambda v: v.reshape(1, -1).astype(F32)
    bf = lambda w: w.astype(BF16)

    def moe_weights(l):
        wr = jnp.concatenate([w_router_g[l], w_router_e[l].transpose(1, 0, 2).reshape(d, N_EXPERTS)], axis=1)
        wr = jnp.pad(wr, ((0, 0), (0, LANES - wr.shape[1])))
        br = jnp.concatenate([b_router_g[l], b_router_e[l].reshape(-1)])
        br = jnp.pad(br, (0, LANES - br.shape[0]))[None, :]
        wr = wr.astype(F32)
        wr_hi = bf(wr)
        wr_split = jnp.concatenate([wr_hi, bf(wr - wr_hi.astype(F32))], axis=1)
        wd = bf(w_down[l]).reshape(N_GROUPS, EXPERTS_PER_GROUP * EXPERT_FF, d)
        return row(g_ffn[l]), wr_split, br.astype(F32), bf(w_gate[l]), bf(w_up[l]), wd

    xp = x_prompt.reshape(n_p, d)
    xs = x_sample.reshape(n_s, d)

    lam_init = 0.8 - 0.6 * math.exp(-0.3 * 0)
    lamv = jnp.stack([lam_q1[0], lam_k1[0], lam_q2[0], lam_k2[0]]).astype(F32)
    gsub = row(g_sub_a[0])
    wq, wk, wv, wo = bf(w_q_a[0]), bf(w_k_a[0]), bf(w_v_a[0]), bf(w_o_a[0])
    g0 = row(g_attn[0])

    cos_p, sin_p = _rope_table(sp, sp, 0, min(ROW_TILE, sp))
    qp, kfp, kbp, vfp, vbp = _qkv_a(x_prompt, g0, wq, wk, wv, cos_p, sin_p, True, A_DK ** -0.5 * LOG2E)
    ap = _diff_prompt(lamv, gsub, qp, kbp, vbp, lam_init)

    tm_s = min(ROW_TILE, n_s)
    cos_s, sin_s = _rope_table(tm_s, ts, past, tm_s)
    qs, kfs, _, vfs, _ = _qkv_a(xs.reshape(1, n_s, d), g0, wq, wk, wv, cos_s, sin_s, False, A_DK ** -0.5)
    heads_first = lambda a, n: a.reshape(bs, ts, n, d // n).transpose(0, 2, 1, 3)
    a_s = _diff_sample(lamv, gsub, heads_first(qs, 2 * A_HEADS),
                       cache_k_a[0], heads_first(bf(kfs), 2 * A_HEADS),
                       cache_v_a[0].reshape(bs, past * A_HEADS, A_DV), heads_first(bf(vfs), A_HEADS), lam_init)

    mw = moe_weights(0)
    gfin = row(g_final)
    xp = _moe(ap.reshape(n_p, d), wo, xp, *mw, gfin, final_norm=False)
    xs = _moe(a_s.reshape(n_s, d), wo, xs, *mw, gfin, final_norm=False)

    gkv, g1 = row(g_kv), row(g_attn[1])
    wkv, wqb, wob = bf(w_kv_b), bf(w_q_b[0]), bf(w_o_b[0])
    bias = _band_bias(rel_bias_b[0].astype(F32))

    xp3 = xp.reshape(bp, sp, d)
    kpad, vpad, qb = _kvq_b(xp3, gkv, g1, wkv, wqb, BF16, lead_pad=True)
    kbt, vbt, _ = _kvq_b(xp3, gkv, g1, wkv, wqb, F32, lead_pad=False, row_block=sp // ROW_TILE - 1)
    ab = _band_prompt(qb, kpad, vpad, bias)

    kns, vns, qbs = _kvq_b(xs.reshape(1, n_s, d), gkv, g1, wkv, wqb, F32, lead_pad=False)
    kns3, vns3 = kns.reshape(bs, ts, d), vns.reshape(bs, ts, d)
    abs_ = _band_sample(qbs.reshape(bs, ts, d), cache_k_b.reshape(bs, wb, d), kns3,
                        cache_v_b.reshape(bs, wb, d), vns3, bias)

    mw = moe_weights(1)
    y_prompt = _moe(ab.reshape(n_p, d), wob, xp, *mw, gfin, final_norm=True).reshape(bp, sp, d)
    y_sample = _moe(abs_.reshape(n_s, d), wob, xs, *mw, gfin, final_norm=True).reshape(bs, ts, d)

    wp = min(B_WINDOW, sp)
    assert wp == ROW_TILE
    k_a_prompt = kfp.reshape(1, bp, sp, 2 * A_HEADS, A_DK)
    v_a_prompt = vfp.reshape(1, bp, sp, A_HEADS, A_DV)
    k_a_sample = kfs.reshape(1, bs, ts, 2 * A_HEADS, A_DK)
    v_a_sample = vfs.reshape(1, bs, ts, A_HEADS, A_DV)
    k_b_prompt = kbt.reshape(bp, wp, B_HEADS, B_DH)
    v_b_prompt = vbt.reshape(bp, wp, B_HEADS, B_DH)
    k_b_sample, v_b_sample = _roll_buffers(cache_k_b, kns.reshape(bs, ts, B_HEADS, B_DH),
                                           cache_v_b, vns.reshape(bs, ts, B_HEADS, B_DH))
    return (y_prompt, y_sample, k_a_prompt, v_a_prompt, k_b_prompt, v_b_prompt,
            k_a_sample, v_a_sample, k_b_sample, v_b_sample)
```

```python
import functools
import math

import jax
import jax.numpy as jnp
from jax import lax
from jax.experimental import pallas as pl
from jax.experimental.pallas import tpu as pltpu

F32 = jnp.float32
BF16 = jnp.bfloat16

EPS = 1e-6
CHUNK = 64
D_MODEL = 1024
A_HEADS = 8
A_DK = 64
A_DV = 128
B_HEADS = 16
B_DH = 64
B_WINDOW = 512
REL_CLIP = 256
N_GROUPS = 4
EXPERTS_PER_GROUP = 4
N_EXPERTS = 16
EXPERT_FF = 512
ROPE_THETA = 10000.0

LANES = 128
HEAD_W = LANES // 2
assert A_DK == HEAD_W and B_DH == HEAD_W and A_DV == LANES
NEG = -1e30
LOG2E = math.log2(math.e)
VMEM_LIMIT = 56 * 1024 * 1024

ROW_TILE = 512
MOE_TILE = 1024
MOE_CHUNK = 128
HEADS_PER_STEP = 4
ONES_ROWS = 16
BAND_TQ = 256
BAND_KW = BAND_TQ + B_WINDOW
BAND_ROWS = 128


def _params(*sem):
    return pltpu.CompilerParams(dimension_semantics=sem, vmem_limit_bytes=VMEM_LIMIT)


def _rms_scale(x):
    return x * lax.rsqrt(jnp.mean(x * x, axis=-1, keepdims=True) + EPS)


def _dot(a, b):
    return jnp.dot(a, b, preferred_element_type=F32)


def _dot_nt(a, b):
    return lax.dot_general(a, b, (((1,), (1,)), ((), ())), preferred_element_type=F32)


def _chunk_of(pos):
    return jnp.right_shift(pos, CHUNK.bit_length() - 1)


def _stack_halves(q):
    lane = lax.broadcasted_iota(jnp.int32, q.shape, 1)
    zero = jnp.zeros_like(q)
    return jnp.concatenate([jnp.where(lane < HEAD_W, q, zero), jnp.where(lane >= HEAD_W, q, zero)], axis=0)


def _rope_table_kernel(inv_ref, cos_ref, sin_ref, *, period, offset, rows):
    i = pl.program_id(0)
    row = lax.broadcasted_iota(jnp.int32, (rows, LANES), 0) + i * rows
    lane = lax.broadcasted_iota(jnp.int32, (rows, LANES), 1)
    pos = (offset + lax.rem(row, period)).astype(F32)
    ang = pos * inv_ref[...]
    s = jnp.sin(ang)
    cos_ref[...] = jnp.cos(ang)
    sin_ref[...] = jnp.where(lax.rem(lane, A_DK) < A_DK // 2, -s, s)


def _rope_table(n_rows, period, offset, rows):
    half = A_DK // 2
    inv = jnp.power(ROPE_THETA, -jnp.arange(half, dtype=F32) / half)
    inv = jnp.tile(inv, LANES // half)[None, :]
    return pl.pallas_call(
        functools.partial(_rope_table_kernel, period=period, offset=offset, rows=rows),
        grid=(n_rows // rows,),
        in_specs=[pl.BlockSpec((1, LANES), lambda i: (0, 0))],
        out_specs=[pl.BlockSpec((rows, LANES), lambda i: (i, 0))] * 2,
        out_shape=[jax.ShapeDtypeStruct((n_rows, LANES), F32)] * 2,
        compiler_params=_params("arbitrary"),
        name="rope_table",
    )(inv)


def _qkv_a_kernel(x_ref, g_ref, wq_ref, wk_ref, wv_ref, cos_ref, sin_ref,
                  q_ref, kf_ref, kb_ref, vf_ref, vb_ref, *, transposed, q_scale):
    h = (_rms_scale(x_ref[0]) * g_ref[...]).astype(BF16)
    cos = cos_ref[...]
    sin = sin_ref[...]
    lane = lax.broadcasted_iota(jnp.int32, cos.shape, 1)
    half = A_DK // 2
    first = lax.rem(lane, A_DK) < half

    yq = _dot(h, wq_ref[...])
    yk = _dot(h, wk_ref[...])
    yv = _dot(h, wv_ref[...])
    for c in range(D_MODEL // LANES):
        sl = slice(c * LANES, (c + 1) * LANES)

        def rope(y):
            yc = y[:, sl]
            rot = jnp.where(first, pltpu.roll(yc, LANES - half, 1), pltpu.roll(yc, half, 1))
            return yc * cos + rot * sin

        rq = rope(yq) * q_scale
        rk = rope(yk)
        vc = yv[:, sl]
        kf_ref[0, :, sl] = rk
        vf_ref[0, :, sl] = vc
        if transposed:
            q_ref[0, c] = rq.T.astype(BF16)
            kb_ref[0, c] = rk.astype(BF16)
            vb_ref[0, c, 0, :A_DV] = vc.T.astype(BF16)
            vb_ref[0, c, 0, A_DV:] = jnp.ones((ONES_ROWS, vc.shape[0]), BF16)
        else:
            q_ref[0, :, sl] = rq.astype(BF16)
            kb_ref[0, :, sl] = rk.astype(BF16)
            vb_ref[0, :, sl] = vc.astype(BF16)


def _qkv_a(x3, g, wq, wk, wv, cos, sin, transposed, q_scale):
    b, s, d = x3.shape
    tm = min(ROW_TILE, s)
    n_tab = cos.shape[0] // tm
    nh = d // LANES
    tok = pl.BlockSpec((1, tm, d), lambda bi, i: (bi, i, 0))
    fshape = jax.ShapeDtypeStruct((b, s, d), F32)
    if transposed:
        hspecs = [pl.BlockSpec((1, nh, LANES, tm), lambda bi, i: (bi, 0, 0, i)),
                  pl.BlockSpec((1, nh, tm, LANES), lambda bi, i: (bi, 0, i, 0)),
                  pl.BlockSpec((1, nh, 1, A_DV + ONES_ROWS, tm), lambda bi, i: (bi, 0, i, 0, 0))]
        hshapes = [jax.ShapeDtypeStruct((b, nh, LANES, s), BF16),
                   jax.ShapeDtypeStruct((b, nh, s, LANES), BF16),
                   jax.ShapeDtypeStruct((b, nh, s // tm, A_DV + ONES_ROWS, tm), BF16)]
    else:
        hspecs = [tok] * 3
        hshapes = [jax.ShapeDtypeStruct((b, s, d), BF16)] * 3
    wspec = pl.BlockSpec((d, d), lambda bi, i: (0, 0))
    tab = pl.BlockSpec((tm, LANES), lambda bi, i: (i % n_tab, 0))
    return pl.pallas_call(
        functools.partial(_qkv_a_kernel, transposed=transposed, q_scale=q_scale),
        grid=(b, s // tm),
        in_specs=[tok, pl.BlockSpec((1, d), lambda bi, i: (0, 0)), wspec, wspec, wspec, tab, tab],
        out_specs=[hspecs[0], tok, hspecs[1], tok, hspecs[2]],
        out_shape=[hshapes[0], fshape, hshapes[1], fshape, hshapes[2]],
        compiler_params=_params("parallel", "parallel"),
        name="qkv_a",
    )(x3, g, wq, wk, wv, cos, sin)


def _diff_lambda(lam_ref, lam_init):
    lv = lam_ref[...]
    s1 = jnp.sum(lv[0:1] * lv[1:2], axis=-1, keepdims=True)
    s2 = jnp.sum(lv[2:3] * lv[3:4], axis=-1, keepdims=True)
    return jnp.exp(s1) - jnp.exp(s2) + lam_init


def _diff_finish(o1, o2, lam, gsub, lam_init):
    d = o1 - lam * o2
    return (_rms_scale(d) * gsub * (1.0 - lam_init)).astype(BF16)


def _diff_prompt_kernel(lam_ref, gsub_ref, qt_ref, k_ref, vt_ref, o_ref, s_ref, p_ref, *, t, lam_init):
    qi = pl.program_id(2)
    nh = qt_ref.shape[1]
    row = lax.broadcasted_iota(jnp.int32, (A_DV, t), 0)
    zero = jnp.zeros((A_DV, t), BF16)
    qst = [jnp.concatenate([jnp.where(row < A_DK, qt_ref[0, h], zero), jnp.where(row >= A_DK, qt_ref[0, h], zero)],
                           axis=1) for h in range(nh)]

    def scores(h, tile, m_prev, mask):
        s = _dot(k_ref[0, h, pl.ds(pl.multiple_of(tile * t, t), t), :], qst[h])
        if mask is not None:
            s = jnp.where(mask, s, NEG)
        m = jnp.maximum(m_prev, jnp.max(s, axis=0, keepdims=True))
        s_ref[h] = s
        return m, jnp.exp2(m_prev - m)

    def accumulate(h, tile, alpha, acc):
        return alpha * acc + _dot(vt_ref[0, h, tile], p_ref[h])

    def exponentials(h, m):
        p_ref[h] = jnp.exp2(s_ref[h] - m).astype(BF16)

    kk = lax.broadcasted_iota(jnp.int32, (t, 2 * t), 0)
    qq = lax.broadcasted_iota(jnp.int32, (t, 2 * t), 1)
    mask = _chunk_of(kk) <= _chunk_of(jnp.bitwise_and(qq, t - 1))
    m0 = jnp.full((1, 2 * t), NEG, F32)
    carry = []
    for h in range(nh):
        m, alpha = scores(h, qi, m0, mask)
        p_ref[h] = jnp.zeros((t, 2 * t), BF16)
        carry.append((m, alpha, jnp.ones((1, 2 * t), F32), jnp.zeros((A_DV + ONES_ROWS, 2 * t), F32)))

    def v_tile(n):
        return jnp.where(n == 0, qi, jnp.maximum(n - 1, 0))

    def body(j, carry):
        out = []
        for h in range(nh):
            m_s, alpha_s, alpha_p, acc = carry[h]
            acc = accumulate(h, v_tile(j - 1), alpha_p, acc)
            exponentials(h, m_s)
            m_new, alpha_new = scores(h, j, m_s, None)
            out.append((m_new, alpha_new, alpha_s, acc))
        return tuple(out)

    carry = lax.fori_loop(0, qi, body, tuple(carry))
    lam = _diff_lambda(lam_ref, lam_init)
    for h in range(nh):
        m_s, alpha_s, alpha_p, acc = carry[h]
        acc = accumulate(h, v_tile(qi - 1), alpha_p, acc)
        exponentials(h, m_s)
        acc = accumulate(h, v_tile(qi), alpha_s, acc)
        o = (acc[:A_DV] * (1.0 / acc[A_DV:A_DV + 1])).T
        o_ref[0, :, h * A_DV:(h + 1) * A_DV] = _diff_finish(o[:t], o[t:], lam, gsub_ref[...], lam_init)


def _diff_prompt(lamv, gsub, qt, k, vt, lam_init):
    b, nh, _, s = qt.shape
    t = vt.shape[-1]
    hs = HEADS_PER_STEP
    return pl.pallas_call(
        functools.partial(_diff_prompt_kernel, t=t, lam_init=lam_init),
        grid=(b, nh // hs, s // t),
        in_specs=[pl.BlockSpec(lamv.shape, lambda bi, hi, qi: (0, 0)),
                  pl.BlockSpec((1, A_DV), lambda bi, hi, qi: (0, 0)),
                  pl.BlockSpec((1, hs, LANES, t), lambda bi, hi, qi: (bi, hi, 0, qi)),
                  pl.BlockSpec((1, hs, s, LANES), lambda bi, hi, qi: (bi, hi, 0, 0), pipeline_mode=pl.Buffered(1)),
                  pl.BlockSpec((1, hs, s // t, A_DV + ONES_ROWS, t), lambda bi, hi, qi: (bi, hi, 0, 0, 0),
                               pipeline_mode=pl.Buffered(1))],
        out_specs=pl.BlockSpec((1, t, hs * LANES), lambda bi, hi, qi: (bi, qi, hi)),
        out_shape=jax.ShapeDtypeStruct((b, s, nh * LANES), BF16),
        scratch_shapes=[pltpu.VMEM((hs, t, 2 * t), F32), pltpu.VMEM((hs, t, 2 * t), BF16)],
        compiler_params=_params("parallel", "parallel", "arbitrary"),
        name="diff_attn_prompt",
    )(lamv, gsub, qt, k, vt)


def _pad_rows(x, rows):
    return jnp.concatenate([x, jnp.zeros((rows - x.shape[0], x.shape[1]), x.dtype)], axis=0)


def _diff_sample_kernel(lam_ref, gsub_ref, q_ref, kc_ref, kn_ref, vc_ref, vn_ref, o_ref, *, lam_init):
    t = q_ref.shape[1]
    qs = _stack_halves(q_ref[0])
    kc = kc_ref[0].astype(BF16)
    vc = vc_ref[0].astype(BF16)
    kn = _pad_rows(kn_ref[0], LANES)
    vn = _pad_rows(vn_ref[0], LANES)
    sc = _dot_nt(qs, kc)
    sn = _dot_nt(qs, kn)
    col = lax.broadcasted_iota(jnp.int32, sn.shape, 1)
    sn = jnp.where(col < t, sn, NEG)
    m = jnp.maximum(jnp.max(sc, axis=-1, keepdims=True), jnp.max(sn, axis=-1, keepdims=True))
    pc = jnp.exp(sc - m)
    pn = jnp.exp(sn - m)
    l = jnp.sum(pc, axis=-1, keepdims=True) + jnp.sum(pn, axis=-1, keepdims=True)
    acc = _dot(pc.astype(BF16), vc) + _dot(pn.astype(BF16), vn)
    o = acc * (1.0 / l)
    o_ref[0] = _diff_finish(o[:t], o[t:], _diff_lambda(lam_ref, lam_init), gsub_ref[...], lam_init)


def _diff_sample(lamv, gsub, q, kc, kn, vc, vn, lam_init):
    b, t, d = q.shape
    past = kc.shape[1]
    new = pl.BlockSpec((1, t, LANES), lambda bi, hi: (bi, 0, hi))
    old = pl.BlockSpec((1, past, LANES), lambda bi, hi: (bi, 0, hi))
    return pl.pallas_call(
        functools.partial(_diff_sample_kernel, lam_init=lam_init),
        grid=(b, d // LANES),
        in_specs=[pl.BlockSpec(lamv.shape, lambda bi, hi: (0, 0)),
                  pl.BlockSpec((1, A_DV), lambda bi, hi: (0, 0)), new, old, new, old, new],
        out_specs=new,
        out_shape=jax.ShapeDtypeStruct((b, t, d), BF16),
        compiler_params=_params("parallel", "parallel"),
        name="diff_attn_sample",
    )(lamv, gsub, q, kc, kn, vc, vn)


def _moe_gates(hf, wr_ref, br_ref):
    hi = hf.astype(BF16)
    lo = (hf - hi.astype(F32)).astype(BF16)
    both = _dot(hi, wr_ref[...])
    lg = both[:, :LANES] + both[:, LANES:] + _dot(lo, wr_ref[:, :LANES]) + br_ref[...]
    lane = lax.broadcasted_iota(jnp.int32, lg.shape, 1).astype(F32)
    big = float(LANES)

    def first_argmax(v):
        mx = jnp.max(v, axis=-1, keepdims=True)
        return mx, jnp.min(jnp.where(v == mx, lane, big), axis=-1, keepdims=True)

    is_g = lane < N_GROUPS
    gmax, gidx = first_argmax(jnp.where(is_g, lg, NEG))
    gval = 1.0 / jnp.sum(jnp.where(is_g, jnp.exp(lg - gmax), 0.0), axis=-1, keepdims=True)
    lo = N_GROUPS + EXPERTS_PER_GROUP * gidx
    el = jnp.where((lane >= lo) & (lane < lo + EXPERTS_PER_GROUP), lg, NEG)
    v1, i1 = first_argmax(el)
    v2, i2 = first_argmax(jnp.where(lane == i1, NEG, el))
    e2 = jnp.exp(v2 - v1)
    inv = gval / (1.0 + e2)
    gates = jnp.where(lane == i1 - lo, inv, 0.0) + jnp.where(lane == i2 - lo, inv * e2, 0.0)
    return jnp.where(lane == gidx, 1.0, 0.0), gates


def _moe_kernel(a_ref, wo_ref, x_ref, g_ref, wr_ref, br_ref, tri_ref, wg_ref, wu_ref, wd_ref, gf_ref, o_ref,
                h_ref, gsplit_ref, rank_ref, rank_t_ref, pos_ref, y_ref, base_ref, *, final_norm):
    grp = pl.program_id(1)
    nb = x_ref.shape[0]

    @pl.when(grp == 0)
    def _():
        x = x_ref[...] + _dot(a_ref[...], wo_ref[...])
        o_ref[...] = x
        hf = _rms_scale(x) * g_ref[...]
        h_ref[...] = hf.astype(BF16)
        member, gates = _moe_gates(hf, wr_ref, br_ref)
        rest = gates
        split = jnp.zeros_like(gates)
        for i in range(3):
            part = rest.astype(BF16).astype(F32)
            split = split + pltpu.roll(part, EXPERTS_PER_GROUP * i, 1)
            rest = rest - part
        gsplit_ref[...] = split.astype(BF16)
        rank = jnp.where(member > 0.0, _dot(tri_ref[...], member.astype(BF16)) + 1.0, 0.0)
        rank_ref[...] = rank
        rank_t_ref[...] = rank.T
        pos_ref[...] = jnp.full(pos_ref.shape, -1.0, F32)
        y_ref[...] = jnp.zeros_like(y_ref)
        base_ref[0] = 0

    base = base_ref[0]
    rrow = rank_t_ref[pl.ds(grp, 1), :]
    n_chunks = (jnp.max(rrow).astype(jnp.int32) + MOE_CHUNK - 1) // MOE_CHUNK
    lane = lax.broadcasted_iota(jnp.int32, (nb, LANES), 1)
    rcol = jnp.sum(jnp.where(lane == grp, rank_ref[...], 0.0), axis=-1, keepdims=True)
    pos_ref[...] = jnp.where(rcol > 0.0, base.astype(F32) + rcol - 1.0, pos_ref[...])

    def chunk(c, carry):
        want = (lax.broadcasted_iota(jnp.int32, (MOE_CHUNK, nb), 0) + (c * MOE_CHUNK + 1)).astype(F32)
        onehot = jnp.where(rrow == want, 1.0, 0.0).astype(BF16)
        xg = _dot(onehot, h_ref[...]).astype(BF16)
        gs3 = _dot(onehot, gsplit_ref[...])
        gs = (gs3 + pltpu.roll(gs3, LANES - EXPERTS_PER_GROUP, 1)
              + pltpu.roll(gs3, LANES - 2 * EXPERTS_PER_GROUP, 1))
        parts = []
        for j in range(EXPERTS_PER_GROUP):
            a = _dot(xg, wg_ref[j])
            u = _dot(xg, wu_ref[j])
            parts.append((a * (1.0 / (1.0 + jnp.exp(-a))) * u * gs[:, j:j + 1]).astype(BF16))
        y = _dot(jnp.concatenate(parts, axis=1), wd_ref[0])
        off = pl.multiple_of(base + c * MOE_CHUNK, MOE_CHUNK)
        y_ref[pl.ds(off, MOE_CHUNK), :] = y.astype(BF16)
        return carry

    lax.fori_loop(0, n_chunks, chunk, 0)
    base_ref[0] = base + n_chunks * MOE_CHUNK

    @pl.when(grp == N_GROUPS - 1)
    def _():
        slot = lax.broadcasted_iota(jnp.int32, (nb, y_ref.shape[0]), 1).astype(F32)
        scatter = jnp.where(pos_ref[...] == slot, 1.0, 0.0).astype(BF16)
        out = o_ref[...] + _dot(scatter, y_ref[...])
        if final_norm:
            out = _rms_scale(out) * gf_ref[...]
        o_ref[...] = out


def _moe(a, wo, x, g, wr, br, wg, wu, wd, gf, final_norm):
    n, d = x.shape
    tm = min(MOE_TILE, n)
    ff = EXPERTS_PER_GROUP * EXPERT_FF
    slots = tm + N_GROUPS * MOE_CHUNK
    tri = jnp.tri(tm, tm, -1, dtype=BF16)
    once = pl.Buffered(1)
    tok = pl.BlockSpec((tm, d), lambda i, e: (i, 0))
    tok_once = pl.BlockSpec((tm, d), lambda i, e: (i, 0), pipeline_mode=once)
    vec = pl.BlockSpec((1, d), lambda i, e: (0, 0))
    experts = pl.BlockSpec((EXPERTS_PER_GROUP, d, EXPERT_FF), lambda i, e: (e, 0, 0))
    return pl.pallas_call(
        functools.partial(_moe_kernel, final_norm=final_norm),
        grid=(n // tm, N_GROUPS),
        in_specs=[tok_once, pl.BlockSpec((d, d), lambda i, e: (0, 0), pipeline_mode=once), tok_once, vec,
                  pl.BlockSpec((d, 2 * LANES), lambda i, e: (0, 0), pipeline_mode=once),
                  pl.BlockSpec((1, LANES), lambda i, e: (0, 0)),
                  pl.BlockSpec((tm, tm), lambda i, e: (0, 0), pipeline_mode=once),
                  experts, experts,
                  pl.BlockSpec((1, ff, d), lambda i, e: (e, 0, 0)),
                  vec],
        out_specs=tok,
        out_shape=jax.ShapeDtypeStruct((n, d), F32),
        scratch_shapes=[pltpu.VMEM((tm, d), BF16), pltpu.VMEM((tm, LANES), BF16),
                        pltpu.VMEM((tm, LANES), F32), pltpu.VMEM((LANES, tm), F32),
                        pltpu.VMEM((tm, 1), F32), pltpu.VMEM((slots, d), BF16),
                        pltpu.SMEM((1,), jnp.int32)],
        compiler_params=_params("parallel", "arbitrary"),
        name="hier_moe",
    )(a, wo, x, g, wr, br, tri, wg, wu, wd, gf)


def _kvq_b_kernel(x_ref, gkv_ref, gq_ref, wkv_ref, wq_ref, k_ref, v_ref, q_ref, *, lead_pad):
    def compute():
        xh = _rms_scale(x_ref[0])
        kv = _dot((xh * gkv_ref[...]).astype(BF16), wkv_ref[...])
        k_ref[0] = kv[:, :D_MODEL].astype(k_ref.dtype)
        v_ref[0] = kv[:, D_MODEL:].astype(v_ref.dtype)
        q = _dot((xh * gq_ref[...]).astype(BF16), wq_ref[...])
        q_ref[0] = (q * (B_DH ** -0.5 * LOG2E)).astype(BF16)

    if lead_pad:
        i = pl.program_id(1)

        @pl.when(i == 0)
        def _():
            k_ref[...] = jnp.zeros_like(k_ref)
            v_ref[...] = jnp.zeros_like(v_ref)

        pl.when(i > 0)(compute)
    else:
        compute()


def _kvq_b(x3, gkv, gq, wkv, wq, kv_dtype, lead_pad, row_block=None):
    b, s, d = x3.shape
    tm = min(ROW_TILE, s)
    if lead_pad:
        assert tm == B_WINDOW
        grid = (b, s // tm + 1)
        src = lambda bi, i: (bi, jnp.maximum(i - 1, 0), 0)
        dst = lambda bi, i: (bi, i, 0)
        s_kv, s_q = s + B_WINDOW, s
    elif row_block is not None:
        grid = (b, 1)
        src = lambda bi, i: (bi, row_block, 0)
        dst = lambda bi, i: (bi, 0, 0)
        s_kv = s_q = tm
    else:
        grid = (b, s // tm)
        src = dst = lambda bi, i: (bi, i, 0)
        s_kv = s_q = s
    vec = pl.BlockSpec((1, d), lambda bi, i: (0, 0))
    return pl.pallas_call(
        functools.partial(_kvq_b_kernel, lead_pad=lead_pad),
        grid=grid,
        in_specs=[pl.BlockSpec((1, tm, d), src), vec, vec,
                  pl.BlockSpec((d, 2 * d), lambda bi, i: (0, 0)),
                  pl.BlockSpec((d, d), lambda bi, i: (0, 0))],
        out_specs=[pl.BlockSpec((1, tm, d), dst), pl.BlockSpec((1, tm, d), dst),
                   pl.BlockSpec((1, tm, d), src if lead_pad else dst)],
        out_shape=[jax.ShapeDtypeStruct((b, s_kv, d), kv_dtype), jax.ShapeDtypeStruct((b, s_kv, d), kv_dtype),
                   jax.ShapeDtypeStruct((b, s_q, d), BF16)],
        compiler_params=_params("parallel", "arbitrary"),
        name="kvq_b",
    )(x3, gkv, gq, wkv, wq)


def _band_bias_kernel(u_ref, o_ref):
    x = jnp.broadcast_to(u_ref[0], (BAND_TQ, 2 * B_WINDOW))
    bias = pltpu.roll(x, 0, 1, stride=1, stride_axis=0)[:, :BAND_KW]
    qc = _chunk_of(lax.broadcasted_iota(jnp.int32, bias.shape, 0))
    kc = _chunk_of(lax.broadcasted_iota(jnp.int32, bias.shape, 1))
    ok = (kc >= qc) & (kc <= qc + B_WINDOW // CHUNK)
    o_ref[0] = jnp.where(ok, bias * LOG2E, NEG)


def _band_bias(rel_bias):
    edge = jnp.broadcast_to(rel_bias[:, 2 * REL_CLIP:], (B_HEADS, REL_CLIP))
    u = jnp.concatenate([edge, rel_bias[:, :0:-1], edge], axis=1)[:, None, :]
    assert u.shape[-1] == 2 * B_WINDOW
    return pl.pallas_call(
        _band_bias_kernel,
        grid=(B_HEADS,),
        in_specs=[pl.BlockSpec((1, 1, 2 * B_WINDOW), lambda h: (h, 0, 0))],
        out_specs=pl.BlockSpec((1, BAND_TQ, BAND_KW), lambda h: (h, 0, 0)),
        out_shape=jax.ShapeDtypeStruct((B_HEADS, BAND_TQ, BAND_KW), F32),
        compiler_params=_params("parallel"),
        name="band_bias",
    )(u)


def _pick_halves(o, t):
    lane = lax.broadcasted_iota(jnp.int32, (t, LANES), 1)
    return jnp.where(lane < HEAD_W, o[:t], o[t:])


def _band_prompt_kernel(q_ref, k_ref, v_ref, bias_ref, o_ref, p_ref, *, n_tiles):
    t = BAND_TQ

    def probabilities(off, lead):
        kw = k_ref[0, pl.ds(off, BAND_KW), :]
        qs = _stack_halves(q_ref[0, pl.ds(off, t), :])
        for g in range(2 * t // BAND_ROWS):
            r0 = g * BAND_ROWS
            s = _dot_nt(qs[r0:r0 + BAND_ROWS], kw) + bias_ref[r0 // t, r0 % t:r0 % t + BAND_ROWS, :]
            if lead:
                c = lax.broadcasted_iota(jnp.int32, s.shape, 1)
                s = jnp.where(c >= B_WINDOW - off, s, NEG)
            m = jnp.max(s, axis=-1, keepdims=True)
            p_ref[r0:r0 + BAND_ROWS, :] = jnp.exp2(s - m).astype(BF16)

    def outputs(off):
        vx = jnp.concatenate([v_ref[0, pl.ds(off, BAND_KW), :], jnp.ones((BAND_KW, LANES), BF16)], axis=1)
        o = _dot(p_ref[...], vx)
        o = o[:, :LANES] * (1.0 / o[:, LANES:])
        o_ref[0, pl.ds(off, t), :] = _pick_halves(o, t).astype(BF16)

    n_lead = B_WINDOW // t
    for ti in range(n_lead):
        if ti > 0:
            outputs((ti - 1) * t)
        probabilities(ti * t, True)

    def body(ti, carry):
        outputs(pl.multiple_of((ti - 1) * t, t))
        probabilities(pl.multiple_of(ti * t, t), False)
        return carry

    lax.fori_loop(n_lead, n_tiles, body, 0)
    outputs((n_tiles - 1) * t)


def _band_prompt(q, kpad, vpad, bias):
    b, s, d = q.shape
    t = BAND_TQ
    nhp = d // LANES
    win = pl.BlockSpec((1, s + B_WINDOW, LANES), lambda hp, bi: (bi, 0, hp))
    seq = pl.BlockSpec((1, s, LANES), lambda hp, bi: (bi, 0, hp))
    return pl.pallas_call(
        functools.partial(_band_prompt_kernel, n_tiles=s // t),
        grid=(nhp, b),
        in_specs=[seq, win, win, pl.BlockSpec((2, t, BAND_KW), lambda hp, bi: (hp, 0, 0))],
        out_specs=seq,
        out_shape=jax.ShapeDtypeStruct((b, s, d), BF16),
        scratch_shapes=[pltpu.VMEM((2 * t, BAND_KW), BF16)],
        compiler_params=_params("parallel", "parallel"),
        name="band_attn_prompt",
    )(q, kpad, vpad, bias)


def _band_sample_kernel(q_ref, kc_ref, kn_ref, vc_ref, vn_ref, bias_ref, o_ref):
    t = q_ref.shape[1]
    wb = kc_ref.shape[1]
    col = lax.broadcasted_iota(jnp.int32, (2 * t, LANES), 1)
    for hp in range(q_ref.shape[2] // LANES):
        lanes = slice(hp * LANES, (hp + 1) * LANES)
        qs = _stack_halves(q_ref[0, :, lanes])
        kn = _pad_rows(kn_ref[0, :, lanes].astype(BF16), LANES)
        vn = _pad_rows(vn_ref[0, :, lanes].astype(BF16), LANES)
        bias = bias_ref[2 * hp:2 * hp + 2].reshape(2 * t, BAND_KW)
        sc = _dot_nt(qs, kc_ref[0, :, lanes].astype(BF16)) + bias[:, :wb]
        sn = jnp.where(col < t, _dot_nt(qs, kn) + bias[:, wb:wb + LANES], NEG)
        m = jnp.maximum(jnp.max(sc, axis=-1, keepdims=True), jnp.max(sn, axis=-1, keepdims=True))
        pc = jnp.exp2(sc - m)
        pn = jnp.exp2(sn - m)
        l = jnp.sum(pc, axis=-1, keepdims=True) + jnp.sum(pn, axis=-1, keepdims=True)
        o = (_dot(pc.astype(BF16), vc_ref[0, :, lanes].astype(BF16)) + _dot(pn.astype(BF16), vn)) * (1.0 / l)
        o_ref[0, :, lanes] = _pick_halves(o, t).astype(BF16)


def _band_sample(q, kc, kn, vc, vn, bias):
    b, t, d = q.shape
    wb = kc.shape[1]
    assert wb == B_WINDOW
    new = pl.BlockSpec((1, t, d), lambda bi: (bi, 0, 0))
    old = pl.BlockSpec((1, wb, d), lambda bi: (bi, 0, 0))
    return pl.pallas_call(
        _band_sample_kernel,
        grid=(b,),
        in_specs=[new, old, new, old, new, pl.BlockSpec((B_HEADS, t, BAND_KW), lambda bi: (0, 0, 0))],
        out_specs=new,
        out_shape=jax.ShapeDtypeStruct((b, t, d), BF16),
        compiler_params=_params("parallel"),
        name="band_attn_sample",
    )(q, kc, kn, vc, vn, bias)


def kernel(x_prompt, x_sample, cache_k_a, cache_v_a, cache_k_b, cache_v_b, g_attn, g_ffn, w_q_a, w_k_a, w_v_a, lam_q1, lam_k1, lam_q2, lam_k2, g_sub_a, w_o_a, g_kv, w_kv_b, w_q_b, rel_bias_b, w_o_b, w_router_g, b_router_g, w_router_e, b_router_e, w_gate, w_up, w_down, g_final):
    bp, sp, d = x_prompt.shape
    bs, ts, _ = x_sample.shape
    past = cache_k_a.shape[2]
    wb = cache_k_b.shape[1]
    n_p, n_s = bp * sp, bs * ts
    assert d == D_MODEL and sp % ROW_TILE == 0 and ROW_TILE % ts == 0 and ts <= LANES
    depth = g_attn.shape[0]
    n_a = w_q_a.shape[0]
    assert depth == 2 and n_a == 1 and w_q_b.shape[0] == 1

    row = lambda v: v.reshape(1, -1).astype(F32)
    bf = lambda w: w.astype(BF16)

    def moe_weights(l):
        wr = jnp.concatenate([w_router_g[l], w_router_e[l].transpose(1, 0, 2).reshape(d, N_EXPERTS)], axis=1)
        wr = jnp.pad(wr, ((0, 0), (0, LANES - wr.shape[1])))
        br = jnp.concatenate([b_router_g[l], b_router_e[l].reshape(-1)])
        br = jnp.pad(br, (0, LANES - br.shape[0]))[None, :]
        wr = wr.astype(F32)
        wr_hi = bf(wr)
        wr_split = jnp.concatenate([wr_hi, bf(wr - wr_hi.astype(F32))], axis=1)
        wd = bf(w_down[l]).reshape(N_GROUPS, EXPERTS_PER_GROUP * EXPERT_FF, d)
        return row(g_ffn[l]), wr_split, br.astype(F32), bf(w_gate[l]), bf(w_up[l]), wd

    xp = x_prompt.reshape(n_p, d)
    xs = x_sample.reshape(n_s, d)

    lam_init = 0.8 - 0.6 * math.exp(-0.3 * 0)
    lamv = jnp.stack([lam_q1[0], lam_k1[0], lam_q2[0], lam_k2[0]]).astype(F32)
    gsub = row(g_sub_a[0])
    wq, wk, wv, wo = bf(w_q_a[0]), bf(w_k_a[0]), bf(w_v_a[0]), bf(w_o_a[0])
    g0 = row(g_attn[0])

    cos_p, sin_p = _rope_table(sp, sp, 0, min(ROW_TILE, sp))
    qp, kfp, kbp, vfp, vbp = _qkv_a(x_prompt, g0, wq, wk, wv, cos_p, sin_p, True, A_DK ** -0.5 * LOG2E)
    ap = _diff_prompt(lamv, gsub, qp, kbp, vbp, lam_init)

    tm_s = min(ROW_TILE, n_s)
    cos_s, sin_s = _rope_table(tm_s, ts, past, tm_s)
    qs, kfs, _, vfs, _ = _qkv_a(xs.reshape(1, n_s, d), g0, wq, wk, wv, cos_s, sin_s, False, A_DK ** -0.5)
    kfs3, vfs3 = kfs.reshape(bs, ts, d), vfs.reshape(bs, ts, d)
    a_s = _diff_sample(lamv, gsub, qs.reshape(bs, ts, d), cache_k_a[0].reshape(bs, past, d), bf(kfs3),
                       cache_v_a[0].reshape(bs, past, d), bf(vfs3), lam_init)

    mw = moe_weights(0)
    gfin = row(g_final)
    xp = _moe(ap.reshape(n_p, d), wo, xp, *mw, gfin, final_norm=False)
    xs = _moe(a_s.reshape(n_s, d), wo, xs, *mw, gfin, final_norm=False)

    gkv, g1 = row(g_kv), row(g_attn[1])
    wkv, wqb, wob = bf(w_kv_b), bf(w_q_b[0]), bf(w_o_b[0])
    bias = _band_bias(rel_bias_b[0].astype(F32))

    xp3 = xp.reshape(bp, sp, d)
    kpad, vpad, qb = _kvq_b(xp3, gkv, g1, wkv, wqb, BF16, lead_pad=True)
    kbt, vbt, _ = _kvq_b(xp3, gkv, g1, wkv, wqb, F32, lead_pad=False, row_block=sp // ROW_TILE - 1)
    ab = _band_prompt(qb, kpad, vpad, bias)

    kns, vns, qbs = _kvq_b(xs.reshape(1, n_s, d), gkv, g1, wkv, wqb, F32, lead_pad=False)
    kns3, vns3 = kns.reshape(bs, ts, d), vns.reshape(bs, ts, d)
    abs_ = _band_sample(qbs.reshape(bs, ts, d), cache_k_b.reshape(bs, wb, d), kns3,
                        cache_v_b.reshape(bs, wb, d), vns3, bias)

    mw = moe_weights(1)
    y_prompt = _moe(ab.reshape(n_p, d), wob, xp, *mw, gfin, final_norm=True).reshape(bp, sp, d)
    y_sample = _moe(abs_.reshape(n_s, d), wob, xs, *mw, gfin, final_norm=True).reshape(bs, ts, d)

    wp = min(B_WINDOW, sp)
    assert wp == ROW_TILE
    k_a_prompt = kfp.reshape(1, bp, sp, 2 * A_HEADS, A_DK)
    v_a_prompt = vfp.reshape(1, bp, sp, A_HEADS, A_DV)
    k_a_sample = kfs.reshape(1, bs, ts, 2 * A_HEADS, A_DK)
    v_a_sample = vfs.reshape(1, bs, ts, A_HEADS, A_DV)
    k_b_prompt = kbt.reshape(bp, wp, B_HEADS, B_DH)
    v_b_prompt = vbt.reshape(bp, wp, B_HEADS, B_DH)
    k_b_sample = jnp.concatenate([cache_k_b[:, ts:], kns.reshape(bs, ts, B_HEADS, B_DH)], axis=1)
    v_b_sample = jnp.concatenate([cache_v_b[:, ts:], vns.reshape(bs, ts, B_HEADS, B_DH)], axis=1)
    return (y_prompt, y_sample, k_a_prompt, v_a_prompt, k_b_prompt, v_b_prompt,
            k_a_sample, v_a_sample, k_b_sample, v_b_sample)
```

```python
import functools
import math

import jax
import jax.numpy as jnp
from jax import lax
from jax.experimental import pallas as pl
from jax.experimental.pallas import tpu as pltpu

F32 = jnp.float32
BF16 = jnp.bfloat16

EPS = 1e-6
CHUNK = 64
D_MODEL = 1024
A_HEADS = 8
A_DK = 64
A_DV = 128
B_HEADS = 16
B_DH = 64
B_WINDOW = 512
REL_CLIP = 256
N_GROUPS = 4
EXPERTS_PER_GROUP = 4
N_EXPERTS = 16
EXPERT_FF = 512
ROPE_THETA = 10000.0

LANES = 128
HEAD_W = LANES // 2
assert A_DK == HEAD_W and B_DH == HEAD_W and A_DV == LANES
NEG = -1e30
LOG2E = math.log2(math.e)
VMEM_LIMIT = 58 * 1024 * 1024

ROW_TILE = 512
MOE_TILE = 1024
MOE_CHUNK = 128
HEADS_PER_STEP = 4
ONES_ROWS = 16
BAND_TQ = 256
BAND_KW = BAND_TQ + B_WINDOW
BAND_ROWS = 128


def _params(*sem):
    return pltpu.CompilerParams(dimension_semantics=sem, vmem_limit_bytes=VMEM_LIMIT)


def _rms_scale(x):
    return x * lax.rsqrt(jnp.mean(x * x, axis=-1, keepdims=True) + EPS)


def _dot(a, b):
    return jnp.dot(a, b, preferred_element_type=F32)


def _dot_nt(a, b):
    return lax.dot_general(a, b, (((1,), (1,)), ((), ())), preferred_element_type=F32)


def _chunk_of(pos):
    return jnp.right_shift(pos, CHUNK.bit_length() - 1)


def _stack_halves(q):
    lane = lax.broadcasted_iota(jnp.int32, q.shape, 1)
    zero = jnp.zeros_like(q)
    return jnp.concatenate([jnp.where(lane < HEAD_W, q, zero), jnp.where(lane >= HEAD_W, q, zero)], axis=0)


def _rope_table_kernel(inv_ref, cos_ref, sin_ref, *, period, offset, rows):
    i = pl.program_id(0)
    row = lax.broadcasted_iota(jnp.int32, (rows, LANES), 0) + i * rows
    lane = lax.broadcasted_iota(jnp.int32, (rows, LANES), 1)
    pos = (offset + lax.rem(row, period)).astype(F32)
    ang = pos * inv_ref[...]
    s = jnp.sin(ang)
    cos_ref[...] = jnp.cos(ang)
    sin_ref[...] = jnp.where(lax.rem(lane, A_DK) < A_DK // 2, -s, s)


def _rope_table(n_rows, period, offset, rows):
    half = A_DK // 2
    inv = jnp.power(ROPE_THETA, -jnp.arange(half, dtype=F32) / half)
    inv = jnp.tile(inv, LANES // half)[None, :]
    return pl.pallas_call(
        functools.partial(_rope_table_kernel, period=period, offset=offset, rows=rows),
        grid=(n_rows // rows,),
        in_specs=[pl.BlockSpec((1, LANES), lambda i: (0, 0))],
        out_specs=[pl.BlockSpec((rows, LANES), lambda i: (i, 0))] * 2,
        out_shape=[jax.ShapeDtypeStruct((n_rows, LANES), F32)] * 2,
        compiler_params=_params("arbitrary"),
        name="rope_table",
    )(inv)


def _qkv_a_kernel(x_ref, g_ref, wq_ref, wk_ref, wv_ref, cos_ref, sin_ref,
                  q_ref, kf_ref, kb_ref, vf_ref, vb_ref, *, transposed, q_scale):
    h = (_rms_scale(x_ref[0]) * g_ref[...]).astype(BF16)
    cos = cos_ref[...]
    sin = sin_ref[...]
    lane = lax.broadcasted_iota(jnp.int32, cos.shape, 1)
    half = A_DK // 2
    first = lax.rem(lane, A_DK) < half

    yq = _dot(h, wq_ref[...])
    yk = _dot(h, wk_ref[...])
    yv = _dot(h, wv_ref[...])
    for c in range(D_MODEL // LANES):
        sl = slice(c * LANES, (c + 1) * LANES)

        def rope(y):
            yc = y[:, sl]
            rot = jnp.where(first, pltpu.roll(yc, LANES - half, 1), pltpu.roll(yc, half, 1))
            return yc * cos + rot * sin

        rq = rope(yq) * q_scale
        rk = rope(yk)
        vc = yv[:, sl]
        kf_ref[0, :, sl] = rk
        vf_ref[0, :, sl] = vc
        if transposed:
            q_ref[0, c] = rq.T.astype(BF16)
            kb_ref[0, c] = rk.astype(BF16)
            vb_ref[0, c, 0, :A_DV] = vc.T.astype(BF16)
            vb_ref[0, c, 0, A_DV:] = jnp.ones((ONES_ROWS, vc.shape[0]), BF16)
        else:
            q_ref[0, :, sl] = rq.astype(BF16)
            kb_ref[0, :, sl] = rk.astype(BF16)
            vb_ref[0, :, sl] = vc.astype(BF16)


def _qkv_a(x3, g, wq, wk, wv, cos, sin, transposed, q_scale):
    b, s, d = x3.shape
    tm = min(ROW_TILE, s)
    n_tab = cos.shape[0] // tm
    nh = d // LANES
    tok = pl.BlockSpec((1, tm, d), lambda bi, i: (bi, i, 0))
    fshape = jax.ShapeDtypeStruct((b, s, d), F32)
    if transposed:
        hspecs = [pl.BlockSpec((1, nh, LANES, tm), lambda bi, i: (bi, 0, 0, i)),
                  pl.BlockSpec((1, nh, tm, LANES), lambda bi, i: (bi, 0, i, 0)),
                  pl.BlockSpec((1, nh, 1, A_DV + ONES_ROWS, tm), lambda bi, i: (bi, 0, i, 0, 0))]
        hshapes = [jax.ShapeDtypeStruct((b, nh, LANES, s), BF16),
                   jax.ShapeDtypeStruct((b, nh, s, LANES), BF16),
                   jax.ShapeDtypeStruct((b, nh, s // tm, A_DV + ONES_ROWS, tm), BF16)]
    else:
        hspecs = [tok] * 3
        hshapes = [jax.ShapeDtypeStruct((b, s, d), BF16)] * 3
    wspec = pl.BlockSpec((d, d), lambda bi, i: (0, 0))
    tab = pl.BlockSpec((tm, LANES), lambda bi, i: (i % n_tab, 0))
    return pl.pallas_call(
        functools.partial(_qkv_a_kernel, transposed=transposed, q_scale=q_scale),
        grid=(b, s // tm),
        in_specs=[tok, pl.BlockSpec((1, d), lambda bi, i: (0, 0)), wspec, wspec, wspec, tab, tab],
        out_specs=[hspecs[0], tok, hspecs[1], tok, hspecs[2]],
        out_shape=[hshapes[0], fshape, hshapes[1], fshape, hshapes[2]],
        compiler_params=_params("parallel", "parallel"),
        name="qkv_a",
    )(x3, g, wq, wk, wv, cos, sin)


def _diff_lambda(lam_ref, lam_init):
    lv = lam_ref[...]
    s1 = jnp.sum(lv[0:1] * lv[1:2], axis=-1, keepdims=True)
    s2 = jnp.sum(lv[2:3] * lv[3:4], axis=-1, keepdims=True)
    return jnp.exp(s1) - jnp.exp(s2) + lam_init


def _diff_finish(o1, o2, lam, gsub, lam_init):
    d = o1 - lam * o2
    return (_rms_scale(d) * gsub * (1.0 - lam_init)).astype(BF16)


def _diff_prompt_kernel(lam_ref, gsub_ref, qt_ref, k_ref, vt_ref, o_ref, s_ref, p_ref, acc_ref, *, t, lam_init):
    qi = pl.program_id(2)
    nh = qt_ref.shape[1]
    row = lax.broadcasted_iota(jnp.int32, (A_DV, t), 0)
    zero = jnp.zeros((A_DV, t), BF16)
    qst = [jnp.concatenate([jnp.where(row < A_DK, qt_ref[0, h], zero), jnp.where(row >= A_DK, qt_ref[0, h], zero)],
                           axis=1) for h in range(nh)]

    def scores(h, tile, m_prev, mask):
        s = _dot(k_ref[0, h, pl.ds(pl.multiple_of(tile * t, t), t), :], qst[h])
        if mask is not None:
            s = jnp.where(mask, s, NEG)
        m = jnp.maximum(m_prev, jnp.max(s, axis=0, keepdims=True))
        s_ref[h] = s
        return m, jnp.exp2(m_prev - m)

    def accumulate(h, tile, alpha):
        acc_ref[h] = alpha * acc_ref[h] + _dot(vt_ref[0, h, tile], p_ref[h])

    def exponentials(h, m):
        p_ref[h] = jnp.exp2(s_ref[h] - m).astype(BF16)

    kk = lax.broadcasted_iota(jnp.int32, (t, 2 * t), 0)
    qq = lax.broadcasted_iota(jnp.int32, (t, 2 * t), 1)
    mask = _chunk_of(kk) <= _chunk_of(jnp.bitwise_and(qq, t - 1))
    m0 = jnp.full((1, 2 * t), NEG, F32)
    carry = []
    for h in range(nh):
        m, alpha = scores(h, qi, m0, mask)
        p_ref[h] = jnp.zeros((t, 2 * t), BF16)
        acc_ref[h] = jnp.zeros((A_DV + ONES_ROWS, 2 * t), F32)
        carry.append((m, alpha, jnp.ones((1, 2 * t), F32)))

    def v_tile(n):
        return jnp.where(n == 0, qi, jnp.maximum(n - 1, 0))

    def body(j, carry):
        out = []
        for h in range(nh):
            m_s, alpha_s, alpha_p = carry[h]
            accumulate(h, v_tile(j - 1), alpha_p)
            exponentials(h, m_s)
            m_new, alpha_new = scores(h, j, m_s, None)
            out.append((m_new, alpha_new, alpha_s))
        return tuple(out)

    carry = lax.fori_loop(0, qi, body, tuple(carry))
    lam = _diff_lambda(lam_ref, lam_init)
    for h in range(nh):
        m_s, alpha_s, alpha_p = carry[h]
        accumulate(h, v_tile(qi - 1), alpha_p)
        exponentials(h, m_s)
        accumulate(h, v_tile(qi), alpha_s)
        acc = acc_ref[h]
        o = (acc[:A_DV] * (1.0 / acc[A_DV:A_DV + 1])).T
        o_ref[0, :, h * A_DV:(h + 1) * A_DV] = _diff_finish(o[:t], o[t:], lam, gsub_ref[...], lam_init)


def _diff_prompt(lamv, gsub, qt, k, vt, lam_init):
    b, nh, _, s = qt.shape
    t = vt.shape[-1]
    hs = HEADS_PER_STEP
    return pl.pallas_call(
        functools.partial(_diff_prompt_kernel, t=t, lam_init=lam_init),
        grid=(b, nh // hs, s // t),
        in_specs=[pl.BlockSpec(lamv.shape, lambda bi, hi, qi: (0, 0)),
                  pl.BlockSpec((1, A_DV), lambda bi, hi, qi: (0, 0)),
                  pl.BlockSpec((1, hs, LANES, t), lambda bi, hi, qi: (bi, hi, 0, qi)),
                  pl.BlockSpec((1, hs, s, LANES), lambda bi, hi, qi: (bi, hi, 0, 0)),
                  pl.BlockSpec((1, hs, s // t, A_DV + ONES_ROWS, t), lambda bi, hi, qi: (bi, hi, 0, 0, 0))],
        out_specs=pl.BlockSpec((1, t, hs * LANES), lambda bi, hi, qi: (bi, qi, hi)),
        out_shape=jax.ShapeDtypeStruct((b, s, nh * LANES), BF16),
        scratch_shapes=[pltpu.VMEM((hs, t, 2 * t), F32), pltpu.VMEM((hs, t, 2 * t), BF16),
                        pltpu.VMEM((hs, A_DV + ONES_ROWS, 2 * t), F32)],
        compiler_params=_params("parallel", "parallel", "arbitrary"),
        name="diff_attn_prompt",
    )(lamv, gsub, qt, k, vt)


def _pad_rows(x, rows):
    return jnp.concatenate([x, jnp.zeros((rows - x.shape[0], x.shape[1]), x.dtype)], axis=0)


def _diff_sample_kernel(lam_ref, gsub_ref, q_ref, kc_ref, kn_ref, vc_ref, vn_ref, o_ref, *, lam_init):
    t = q_ref.shape[1]
    qs = _stack_halves(q_ref[0])
    kc = kc_ref[0].astype(BF16)
    vc = vc_ref[0].astype(BF16)
    kn = _pad_rows(kn_ref[0], LANES)
    vn = _pad_rows(vn_ref[0], LANES)
    sc = _dot_nt(qs, kc)
    sn = _dot_nt(qs, kn)
    col = lax.broadcasted_iota(jnp.int32, sn.shape, 1)
    sn = jnp.where(col < t, sn, NEG)
    m = jnp.maximum(jnp.max(sc, axis=-1, keepdims=True), jnp.max(sn, axis=-1, keepdims=True))
    pc = jnp.exp(sc - m)
    pn = jnp.exp(sn - m)
    l = jnp.sum(pc, axis=-1, keepdims=True) + jnp.sum(pn, axis=-1, keepdims=True)
    acc = _dot(pc.astype(BF16), vc) + _dot(pn.astype(BF16), vn)
    o = acc * (1.0 / l)
    o_ref[0] = _diff_finish(o[:t], o[t:], _diff_lambda(lam_ref, lam_init), gsub_ref[...], lam_init)


def _diff_sample(lamv, gsub, q, kc, kn, vc, vn, lam_init):
    b, t, d = q.shape
    past = kc.shape[1]
    new = pl.BlockSpec((1, t, LANES), lambda bi, hi: (bi, 0, hi))
    old = pl.BlockSpec((1, past, LANES), lambda bi, hi: (bi, 0, hi))
    return pl.pallas_call(
        functools.partial(_diff_sample_kernel, lam_init=lam_init),
        grid=(b, d // LANES),
        in_specs=[pl.BlockSpec(lamv.shape, lambda bi, hi: (0, 0)),
                  pl.BlockSpec((1, A_DV), lambda bi, hi: (0, 0)), new, old, new, old, new],
        out_specs=new,
        out_shape=jax.ShapeDtypeStruct((b, t, d), BF16),
        compiler_params=_params("parallel", "parallel"),
        name="diff_attn_sample",
    )(lamv, gsub, q, kc, kn, vc, vn)


def _moe_gates(hf, wr_ref, br_ref):
    hi = hf.astype(BF16)
    lo = (hf - hi.astype(F32)).astype(BF16)
    both = _dot(hi, wr_ref[...])
    lg = both[:, :LANES] + both[:, LANES:] + _dot(lo, wr_ref[:, :LANES]) + br_ref[...]
    lane = lax.broadcasted_iota(jnp.int32, lg.shape, 1).astype(F32)
    big = float(LANES)

    def first_argmax(v):
        mx = jnp.max(v, axis=-1, keepdims=True)
        return mx, jnp.min(jnp.where(v == mx, lane, big), axis=-1, keepdims=True)

    is_g = lane < N_GROUPS
    gmax, gidx = first_argmax(jnp.where(is_g, lg, NEG))
    gval = 1.0 / jnp.sum(jnp.where(is_g, jnp.exp(lg - gmax), 0.0), axis=-1, keepdims=True)
    lo = N_GROUPS + EXPERTS_PER_GROUP * gidx
    el = jnp.where((lane >= lo) & (lane < lo + EXPERTS_PER_GROUP), lg, NEG)
    v1, i1 = first_argmax(el)
    v2, i2 = first_argmax(jnp.where(lane == i1, NEG, el))
    e2 = jnp.exp(v2 - v1)
    inv = gval / (1.0 + e2)
    gates = jnp.where(lane == i1 - lo, inv, 0.0) + jnp.where(lane == i2 - lo, inv * e2, 0.0)
    return jnp.where(lane == gidx, 1.0, 0.0), gates


def _moe_kernel(a_ref, wo_ref, x_ref, g_ref, wr_ref, br_ref, wg_ref, wu_ref, wd_ref, gf_ref, o_ref,
                h_ref, gsplit_ref, rank_ref, rank_t_ref, pos_ref, y_ref, base_ref, *, final_norm):
    grp = pl.program_id(1)
    nb = x_ref.shape[0]

    @pl.when(grp == 0)
    def _():
        x = x_ref[...] + _dot(a_ref[...], wo_ref[...])
        o_ref[...] = x
        hf = _rms_scale(x) * g_ref[...]
        h_ref[...] = hf.astype(BF16)
        member, gates = _moe_gates(hf, wr_ref, br_ref)
        rest = gates
        split = jnp.zeros_like(gates)
        for i in range(3):
            part = rest.astype(BF16).astype(F32)
            split = split + pltpu.roll(part, EXPERTS_PER_GROUP * i, 1)
            rest = rest - part
        gsplit_ref[...] = split.astype(BF16)
        earlier = (lax.broadcasted_iota(jnp.int32, (nb, nb), 1) < lax.broadcasted_iota(jnp.int32, (nb, nb), 0))
        rank = jnp.where(member > 0.0, _dot(jnp.where(earlier, 1.0, 0.0).astype(BF16), member.astype(BF16)) + 1.0, 0.0)
        rank_ref[...] = rank
        rank_t_ref[...] = rank.T
        pos_ref[...] = jnp.full(pos_ref.shape, -1.0, F32)
        y_ref[...] = jnp.zeros_like(y_ref)
        base_ref[0] = 0

    base = base_ref[0]
    rrow = rank_t_ref[pl.ds(grp, 1), :]
    n_chunks = (jnp.max(rrow).astype(jnp.int32) + MOE_CHUNK - 1) // MOE_CHUNK
    lane = lax.broadcasted_iota(jnp.int32, (nb, LANES), 1)
    rcol = jnp.sum(jnp.where(lane == grp, rank_ref[...], 0.0), axis=-1, keepdims=True)
    pos_ref[...] = jnp.where(rcol > 0.0, base.astype(F32) + rcol - 1.0, pos_ref[...])

    def chunk(c, carry):
        want = (lax.broadcasted_iota(jnp.int32, (MOE_CHUNK, nb), 0) + (c * MOE_CHUNK + 1)).astype(F32)
        onehot = jnp.where(rrow == want, 1.0, 0.0).astype(BF16)
        xg = _dot(onehot, h_ref[...]).astype(BF16)
        gs3 = _dot(onehot, gsplit_ref[...])
        gs = (gs3 + pltpu.roll(gs3, LANES - EXPERTS_PER_GROUP, 1)
              + pltpu.roll(gs3, LANES - 2 * EXPERTS_PER_GROUP, 1))
        parts = []
        for j in range(EXPERTS_PER_GROUP):
            a = _dot(xg, wg_ref[j])
            u = _dot(xg, wu_ref[j])
            parts.append((a * (1.0 / (1.0 + jnp.exp(-a))) * u * gs[:, j:j + 1]).astype(BF16))
        y = _dot(jnp.concatenate(parts, axis=1), wd_ref[0])
        off = pl.multiple_of(base + c * MOE_CHUNK, MOE_CHUNK)
        y_ref[pl.ds(off, MOE_CHUNK), :] = y.astype(BF16)
        return carry

    lax.fori_loop(0, n_chunks, chunk, 0)
    base_ref[0] = base + n_chunks * MOE_CHUNK

    @pl.when(grp == N_GROUPS - 1)
    def _():
        slot = lax.broadcasted_iota(jnp.int32, (nb, y_ref.shape[0]), 1).astype(F32)
        scatter = jnp.where(pos_ref[...] == slot, 1.0, 0.0).astype(BF16)
        out = o_ref[...] + _dot(scatter, y_ref[...])
        if final_norm:
            out = _rms_scale(out) * gf_ref[...]
        o_ref[...] = out


def _moe(a, wo, x, g, wr, br, wg, wu, wd, gf, final_norm):
    n, d = x.shape
    tm = min(MOE_TILE, n)
    ff = EXPERTS_PER_GROUP * EXPERT_FF
    slots = tm + N_GROUPS * MOE_CHUNK
    once = pl.Buffered(1)
    tok = pl.BlockSpec((tm, d), lambda i, e: (i, 0))
    vec = pl.BlockSpec((1, d), lambda i, e: (0, 0))
    experts = pl.BlockSpec((EXPERTS_PER_GROUP, d, EXPERT_FF), lambda i, e: (e, 0, 0))
    return pl.pallas_call(
        functools.partial(_moe_kernel, final_norm=final_norm),
        grid=(n // tm, N_GROUPS),
        in_specs=[tok, pl.BlockSpec((d, d), lambda i, e: (0, 0), pipeline_mode=once), tok, vec,
                  pl.BlockSpec((d, 2 * LANES), lambda i, e: (0, 0), pipeline_mode=once),
                  pl.BlockSpec((1, LANES), lambda i, e: (0, 0)),
                  experts, experts,
                  pl.BlockSpec((1, ff, d), lambda i, e: (e, 0, 0)),
                  vec],
        out_specs=tok,
        out_shape=jax.ShapeDtypeStruct((n, d), F32),
        scratch_shapes=[pltpu.VMEM((tm, d), BF16), pltpu.VMEM((tm, LANES), BF16),
                        pltpu.VMEM((tm, LANES), F32), pltpu.VMEM((LANES, tm), F32),
                        pltpu.VMEM((tm, 1), F32), pltpu.VMEM((slots, d), BF16),
                        pltpu.SMEM((1,), jnp.int32)],
        compiler_params=_params("parallel", "arbitrary"),
        name="hier_moe",
    )(a, wo, x, g, wr, br, wg, wu, wd, gf)


def _kvq_b_kernel(x_ref, gkv_ref, gq_ref, wkv_ref, wq_ref, k_ref, v_ref, q_ref, *, lead_pad):
    def compute():
        xh = _rms_scale(x_ref[0])
        kv = _dot((xh * gkv_ref[...]).astype(BF16), wkv_ref[...])
        k_ref[0] = kv[:, :D_MODEL].astype(k_ref.dtype)
        v_ref[0] = kv[:, D_MODEL:].astype(v_ref.dtype)
        q = _dot((xh * gq_ref[...]).astype(BF16), wq_ref[...])
        q_ref[0] = (q * (B_DH ** -0.5 * LOG2E)).astype(BF16)

    if lead_pad:
        i = pl.program_id(1)

        @pl.when(i == 0)
        def _():
            k_ref[...] = jnp.zeros_like(k_ref)
            v_ref[...] = jnp.zeros_like(v_ref)

        pl.when(i > 0)(compute)
    else:
        compute()


def _kvq_b(x3, gkv, gq, wkv, wq, kv_dtype, lead_pad, row_block=None):
    b, s, d = x3.shape
    tm = min(ROW_TILE, s)
    if lead_pad:
        assert tm == B_WINDOW
        grid = (b, s // tm + 1)
        src = lambda bi, i: (bi, jnp.maximum(i - 1, 0), 0)
        dst = lambda bi, i: (bi, i, 0)
        s_kv, s_q = s + B_WINDOW, s
    elif row_block is not None:
        grid = (b, 1)
        src = lambda bi, i: (bi, row_block, 0)
        dst = lambda bi, i: (bi, 0, 0)
        s_kv = s_q = tm
    else:
        grid = (b, s // tm)
        src = dst = lambda bi, i: (bi, i, 0)
        s_kv = s_q = s
    vec = pl.BlockSpec((1, d), lambda bi, i: (0, 0))
    return pl.pallas_call(
        functools.partial(_kvq_b_kernel, lead_pad=lead_pad),
        grid=grid,
        in_specs=[pl.BlockSpec((1, tm, d), src), vec, vec,
                  pl.BlockSpec((d, 2 * d), lambda bi, i: (0, 0)),
                  pl.BlockSpec((d, d), lambda bi, i: (0, 0))],
        out_specs=[pl.BlockSpec((1, tm, d), dst), pl.BlockSpec((1, tm, d), dst),
                   pl.BlockSpec((1, tm, d), src if lead_pad else dst)],
        out_shape=[jax.ShapeDtypeStruct((b, s_kv, d), kv_dtype), jax.ShapeDtypeStruct((b, s_kv, d), kv_dtype),
                   jax.ShapeDtypeStruct((b, s_q, d), BF16)],
        compiler_params=_params("parallel", "arbitrary"),
        name="kvq_b",
    )(x3, gkv, gq, wkv, wq)


def _band_bias_kernel(u_ref, o_ref):
    x = jnp.broadcast_to(u_ref[0], (BAND_TQ, 2 * B_WINDOW))
    bias = pltpu.roll(x, 0, 1, stride=1, stride_axis=0)[:, :BAND_KW]
    qc = _chunk_of(lax.broadcasted_iota(jnp.int32, bias.shape, 0))
    kc = _chunk_of(lax.broadcasted_iota(jnp.int32, bias.shape, 1))
    ok = (kc >= qc) & (kc <= qc + B_WINDOW // CHUNK)
    o_ref[0] = jnp.where(ok, bias * LOG2E, NEG)


def _band_bias(rel_bias):
    edge = jnp.broadcast_to(rel_bias[:, 2 * REL_CLIP:], (B_HEADS, REL_CLIP))
    u = jnp.concatenate([edge, rel_bias[:, :0:-1], edge], axis=1)[:, None, :]
    assert u.shape[-1] == 2 * B_WINDOW
    return pl.pallas_call(
        _band_bias_kernel,
        grid=(B_HEADS,),
        in_specs=[pl.BlockSpec((1, 1, 2 * B_WINDOW), lambda h: (h, 0, 0))],
        out_specs=pl.BlockSpec((1, BAND_TQ, BAND_KW), lambda h: (h, 0, 0)),
        out_shape=jax.ShapeDtypeStruct((B_HEADS, BAND_TQ, BAND_KW), F32),
        compiler_params=_params("parallel"),
        name="band_bias",
    )(u)


def _pick_halves(o, t):
    lane = lax.broadcasted_iota(jnp.int32, (t, LANES), 1)
    return jnp.where(lane < HEAD_W, o[:t], o[t:])


def _band_prompt_kernel(q_ref, k_ref, v_ref, bias_ref, o_ref, p_ref, *, n_tiles):
    t = BAND_TQ

    def probabilities(off, lead):
        kw = k_ref[0, pl.ds(off, BAND_KW), :]
        qs = _stack_halves(q_ref[0, pl.ds(off, t), :])
        for g in range(2 * t // BAND_ROWS):
            r0 = g * BAND_ROWS
            s = _dot_nt(qs[r0:r0 + BAND_ROWS], kw) + bias_ref[r0 // t, r0 % t:r0 % t + BAND_ROWS, :]
            if lead:
                c = lax.broadcasted_iota(jnp.int32, s.shape, 1)
                s = jnp.where(c >= B_WINDOW - off, s, NEG)
            m = jnp.max(s, axis=-1, keepdims=True)
            p_ref[r0:r0 + BAND_ROWS, :] = jnp.exp2(s - m).astype(BF16)

    def outputs(off):
        vx = jnp.concatenate([v_ref[0, pl.ds(off, BAND_KW), :], jnp.ones((BAND_KW, LANES), BF16)], axis=1)
        o = _dot(p_ref[...], vx)
        o = o[:, :LANES] * (1.0 / o[:, LANES:])
        o_ref[0, pl.ds(off, t), :] = _pick_halves(o, t).astype(BF16)

    n_lead = B_WINDOW // t
    for ti in range(n_lead):
        if ti > 0:
            outputs((ti - 1) * t)
        probabilities(ti * t, True)

    def body(ti, carry):
        outputs(pl.multiple_of((ti - 1) * t, t))
        probabilities(pl.multiple_of(ti * t, t), False)
        return carry

    lax.fori_loop(n_lead, n_tiles, body, 0)
    outputs((n_tiles - 1) * t)


def _band_prompt(q, kpad, vpad, bias):
    b, s, d = q.shape
    t = BAND_TQ
    nhp = d // LANES
    win = pl.BlockSpec((1, s + B_WINDOW, LANES), lambda hp, bi: (bi, 0, hp))
    seq = pl.BlockSpec((1, s, LANES), lambda hp, bi: (bi, 0, hp))
    return pl.pallas_call(
        functools.partial(_band_prompt_kernel, n_tiles=s // t),
        grid=(nhp, b),
        in_specs=[seq, win, win, pl.BlockSpec((2, t, BAND_KW), lambda hp, bi: (hp, 0, 0))],
        out_specs=seq,
        out_shape=jax.ShapeDtypeStruct((b, s, d), BF16),
        scratch_shapes=[pltpu.VMEM((2 * t, BAND_KW), BF16)],
        compiler_params=_params("parallel", "parallel"),
        name="band_attn_prompt",
    )(q, kpad, vpad, bias)


def _band_sample_kernel(q_ref, kc_ref, kn_ref, vc_ref, vn_ref, bias_ref, o_ref):
    t = q_ref.shape[1]
    wb = kc_ref.shape[1]
    col = lax.broadcasted_iota(jnp.int32, (2 * t, LANES), 1)
    for hp in range(q_ref.shape[2] // LANES):
        lanes = slice(hp * LANES, (hp + 1) * LANES)
        qs = _stack_halves(q_ref[0, :, lanes])
        kn = _pad_rows(kn_ref[0, :, lanes].astype(BF16), LANES)
        vn = _pad_rows(vn_ref[0, :, lanes].astype(BF16), LANES)
        bias = bias_ref[2 * hp:2 * hp + 2].reshape(2 * t, BAND_KW)
        sc = _dot_nt(qs, kc_ref[0, :, lanes].astype(BF16)) + bias[:, :wb]
        sn = jnp.where(col < t, _dot_nt(qs, kn) + bias[:, wb:wb + LANES], NEG)
        m = jnp.maximum(jnp.max(sc, axis=-1, keepdims=True), jnp.max(sn, axis=-1, keepdims=True))
        pc = jnp.exp2(sc - m)
        pn = jnp.exp2(sn - m)
        l = jnp.sum(pc, axis=-1, keepdims=True) + jnp.sum(pn, axis=-1, keepdims=True)
        o = (_dot(pc.astype(BF16), vc_ref[0, :, lanes].astype(BF16)) + _dot(pn.astype(BF16), vn)) * (1.0 / l)
        o_ref[0, :, lanes] = _pick_halves(o, t).astype(BF16)


def _band_sample(q, kc, kn, vc, vn, bias):
    b, t, d = q.shape
    wb = kc.shape[1]
    assert wb == B_WINDOW
    new = pl.BlockSpec((1, t, d), lambda bi: (bi, 0, 0))
    old = pl.BlockSpec((1, wb, d), lambda bi: (bi, 0, 0))
    return pl.pallas_call(
        _band_sample_kernel,
        grid=(b,),
        in_specs=[new, old, new, old, new, pl.BlockSpec((B_HEADS, t, BAND_KW), lambda bi: (0, 0, 0))],
        out_specs=new,
        out_shape=jax.ShapeDtypeStruct((b, t, d), BF16),
        compiler_params=_params("parallel"),
        name="band_attn_sample",
    )(q, kc, kn, vc, vn, bias)


def kernel(x_prompt, x_sample, cache_k_a, cache_v_a, cache_k_b, cache_v_b, g_attn, g_ffn, w_q_a, w_k_a, w_v_a, lam_q1, lam_k1, lam_q2, lam_k2, g_sub_a, w_o_a, g_kv, w_kv_b, w_q_b, rel_bias_b, w_o_b, w_router_g, b_router_g, w_router_e, b_router_e, w_gate, w_up, w_down, g_final):
    bp, sp, d = x_prompt.shape
    bs, ts, _ = x_sample.shape
    past = cache_k_a.shape[2]
    wb = cache_k_b.shape[1]
    n_p, n_s = bp * sp, bs * ts
    assert d == D_MODEL and sp % ROW_TILE == 0 and ROW_TILE % ts == 0 and ts <= LANES
    depth = g_attn.shape[0]
    n_a = w_q_a.shape[0]
    assert depth == 2 and n_a == 1 and w_q_b.shape[0] == 1

    row = lambda v: v.reshape(1, -1).astype(F32)
    bf = lambda w: w.astype(BF16)

    def moe_weights(l):
        wr = jnp.concatenate([w_router_g[l], w_router_e[l].transpose(1, 0, 2).reshape(d, N_EXPERTS)], axis=1)
        wr = jnp.pad(wr, ((0, 0), (0, LANES - wr.shape[1])))
        br = jnp.concatenate([b_router_g[l], b_router_e[l].reshape(-1)])
        br = jnp.pad(br, (0, LANES - br.shape[0]))[None, :]
        wr = wr.astype(F32)
        wr_hi = bf(wr)
        wr_split = jnp.concatenate([wr_hi, bf(wr - wr_hi.astype(F32))], axis=1)
        wd = bf(w_down[l]).reshape(N_GROUPS, EXPERTS_PER_GROUP * EXPERT_FF, d)
        return row(g_ffn[l]), wr_split, br.astype(F32), bf(w_gate[l]), bf(w_up[l]), wd

    xp = x_prompt.reshape(n_p, d)
    xs = x_sample.reshape(n_s, d)

    lam_init = 0.8 - 0.6 * math.exp(-0.3 * 0)
    lamv = jnp.stack([lam_q1[0], lam_k1[0], lam_q2[0], lam_k2[0]]).astype(F32)
    gsub = row(g_sub_a[0])
    wq, wk, wv, wo = bf(w_q_a[0]), bf(w_k_a[0]), bf(w_v_a[0]), bf(w_o_a[0])
    g0 = row(g_attn[0])

    cos_p, sin_p = _rope_table(sp, sp, 0, min(ROW_TILE, sp))
    qp, kfp, kbp, vfp, vbp = _qkv_a(x_prompt, g0, wq, wk, wv, cos_p, sin_p, True, A_DK ** -0.5 * LOG2E)
    ap = _diff_prompt(lamv, gsub, qp, kbp, vbp, lam_init)

    tm_s = min(ROW_TILE, n_s)
    cos_s, sin_s = _rope_table(tm_s, ts, past, tm_s)
    qs, kfs, _, vfs, _ = _qkv_a(xs.reshape(1, n_s, d), g0, wq, wk, wv, cos_s, sin_s, False, A_DK ** -0.5)
    kfs3, vfs3 = kfs.reshape(bs, ts, d), vfs.reshape(bs, ts, d)
    a_s = _diff_sample(lamv, gsub, qs.reshape(bs, ts, d), cache_k_a[0].reshape(bs, past, d), bf(kfs3),
                       cache_v_a[0].reshape(bs, past, d), bf(vfs3), lam_init)

    mw = moe_weights(0)
    gfin = row(g_final)
    xp = _moe(ap.reshape(n_p, d), wo, xp, *mw, gfin, final_norm=False)
    xs = _moe(a_s.reshape(n_s, d), wo, xs, *mw, gfin, final_norm=False)

    gkv, g1 = row(g_kv), row(g_attn[1])
    wkv, wqb, wob = bf(w_kv_b), bf(w_q_b[0]), bf(w_o_b[0])
    bias = _band_bias(rel_bias_b[0].astype(F32))

    xp3 = xp.reshape(bp, sp, d)
    kpad, vpad, qb = _kvq_b(xp3, gkv, g1, wkv, wqb, BF16, lead_pad=True)
    kbt, vbt, _ = _kvq_b(xp3, gkv, g1, wkv, wqb, F32, lead_pad=False, row_block=sp // ROW_TILE - 1)
    ab = _band_prompt(qb, kpad, vpad, bias)

    kns, vns, qbs = _kvq_b(xs.reshape(1, n_s, d), gkv, g1, wkv, wqb, F32, lead_pad=False)
    kns3, vns3 = kns.reshape(bs, ts, d), vns.reshape(bs, ts, d)
    abs_ = _band_sample(qbs.reshape(bs, ts, d), cache_k_b.reshape(bs, wb, d), kns3,
                        cache_v_b.reshape(bs, wb, d), vns3, bias)

    mw = moe_weights(1)
    y_prompt = _moe(ab.reshape(n_p, d), wob, xp, *mw, gfin, final_norm=True).reshape(bp, sp, d)
    y_sample = _moe(abs_.reshape(n_s, d), wob, xs, *mw, gfin, final_norm=True).reshape(bs, ts, d)

    wp = min(B_WINDOW, sp)
    assert wp == ROW_TILE
    k_a_prompt = kfp.reshape(1, bp, sp, 2 * A_HEADS, A_DK)
    v_a_prompt = vfp.reshape(1, bp, sp, A_HEADS, A_DV)
    k_a_sample = kfs.reshape(1, bs, ts, 2 * A_HEADS, A_DK)
    v_a_sample = vfs.reshape(1, bs, ts, A_HEADS, A_DV)
    k_b_prompt = kbt.reshape(bp, wp, B_HEADS, B_DH)
    v_b_prompt = vbt.reshape(bp, wp, B_HEADS, B_DH)
    k_b_sample = jnp.concatenate([cache_k_b[:, ts:], kns.reshape(bs, ts, B_HEADS, B_DH)], axis=1)
    v_b_sample = jnp.concatenate([cache_v_b[:, ts:], vns.reshape(bs, ts, B_HEADS, B_DH)], axis=1)
    return (y_prompt, y_sample, k_a_prompt, v_a_prompt, k_b_prompt, v_b_prompt,
            k_a_sample, v_a_sample, k_b_sample, v_b_sample)
```

```python
import functools
import math

import jax
import jax.numpy as jnp
from jax import lax
from jax.experimental import pallas as pl
from jax.experimental.pallas import tpu as pltpu

F32 = jnp.float32
BF16 = jnp.bfloat16

EPS = 1e-6
CHUNK = 64
D_MODEL = 1024
A_HEADS = 8
A_DK = 64
A_DV = 128
B_HEADS = 16
B_DH = 64
B_WINDOW = 512
REL_CLIP = 256
N_GROUPS = 4
EXPERTS_PER_GROUP = 4
N_EXPERTS = 16
EXPERT_FF = 512
ROPE_THETA = 10000.0

LANES = 128
HEAD_W = LANES // 2
assert A_DK == HEAD_W and B_DH == HEAD_W and A_DV == LANES
NEG = -1e30
LOG2E = math.log2(math.e)
VMEM_LIMIT = 58 * 1024 * 1024

ROW_TILE = 512
MOE_TILE = 1024
MOE_CHUNK = 128
HEADS_PER_STEP = 4
ONES_ROWS = 16
BAND_TQ = 256
BAND_KW = BAND_TQ + B_WINDOW
BAND_ROWS = 128
SAMPLE_HEADS = 2


def _params(*sem):
    return pltpu.CompilerParams(dimension_semantics=sem, vmem_limit_bytes=VMEM_LIMIT)


def _rms_scale(x):
    return x * lax.rsqrt(jnp.mean(x * x, axis=-1, keepdims=True) + EPS)


def _dot(a, b):
    return jnp.dot(a, b, preferred_element_type=F32)


def _dot_nt(a, b):
    return lax.dot_general(a, b, (((1,), (1,)), ((), ())), preferred_element_type=F32)


def _chunk_of(pos):
    return jnp.right_shift(pos, CHUNK.bit_length() - 1)


def _stack_halves(q):
    lane = lax.broadcasted_iota(jnp.int32, q.shape, 1)
    zero = jnp.zeros_like(q)
    return jnp.concatenate([jnp.where(lane < HEAD_W, q, zero), jnp.where(lane >= HEAD_W, q, zero)], axis=0)


def _rope_table_kernel(inv_ref, cos_ref, sin_ref, *, period, offset, rows):
    i = pl.program_id(0)
    row = lax.broadcasted_iota(jnp.int32, (rows, LANES), 0) + i * rows
    lane = lax.broadcasted_iota(jnp.int32, (rows, LANES), 1)
    pos = (offset + lax.rem(row, period)).astype(F32)
    ang = pos * inv_ref[...]
    s = jnp.sin(ang)
    cos_ref[...] = jnp.cos(ang)
    sin_ref[...] = jnp.where(lax.rem(lane, A_DK) < A_DK // 2, -s, s)


def _rope_table(n_rows, period, offset, rows):
    half = A_DK // 2
    inv = jnp.power(ROPE_THETA, -jnp.arange(half, dtype=F32) / half)
    inv = jnp.tile(inv, LANES // half)[None, :]
    return pl.pallas_call(
        functools.partial(_rope_table_kernel, period=period, offset=offset, rows=rows),
        grid=(n_rows // rows,),
        in_specs=[pl.BlockSpec((1, LANES), lambda i: (0, 0))],
        out_specs=[pl.BlockSpec((rows, LANES), lambda i: (i, 0))] * 2,
        out_shape=[jax.ShapeDtypeStruct((n_rows, LANES), F32)] * 2,
        compiler_params=_params("arbitrary"),
        name="rope_table",
    )(inv)


def _qkv_a_kernel(x_ref, g_ref, wq_ref, wk_ref, wv_ref, cos_ref, sin_ref,
                  q_ref, kf_ref, kb_ref, vf_ref, vb_ref, *, transposed, q_scale):
    h = (_rms_scale(x_ref[0]) * g_ref[...]).astype(BF16)
    cos = cos_ref[...]
    sin = sin_ref[...]
    lane = lax.broadcasted_iota(jnp.int32, cos.shape, 1)
    half = A_DK // 2
    first = lax.rem(lane, A_DK) < half

    yq = _dot(h, wq_ref[...])
    yk = _dot(h, wk_ref[...])
    yv = _dot(h, wv_ref[...])
    for c in range(D_MODEL // LANES):
        sl = slice(c * LANES, (c + 1) * LANES)

        def rope(y):
            yc = y[:, sl]
            rot = jnp.where(first, pltpu.roll(yc, LANES - half, 1), pltpu.roll(yc, half, 1))
            return yc * cos + rot * sin

        rq = rope(yq) * q_scale
        rk = rope(yk)
        vc = yv[:, sl]
        kf_ref[0, :, sl] = rk
        vf_ref[0, :, sl] = vc
        if transposed:
            q_ref[0, c] = rq.T.astype(BF16)
            kb_ref[0, c] = rk.astype(BF16)
            vb_ref[0, c, 0, :A_DV] = vc.T.astype(BF16)
            vb_ref[0, c, 0, A_DV:] = jnp.ones((ONES_ROWS, vc.shape[0]), BF16)
        else:
            q_ref[0, :, sl] = rq.astype(BF16)
            kb_ref[0, :, sl] = rk.astype(BF16)
            vb_ref[0, :, sl] = vc.astype(BF16)


def _qkv_a(x3, g, wq, wk, wv, cos, sin, transposed, q_scale):
    b, s, d = x3.shape
    tm = min(ROW_TILE, s)
    n_tab = cos.shape[0] // tm
    nh = d // LANES
    tok = pl.BlockSpec((1, tm, d), lambda bi, i: (bi, i, 0))
    fshape = jax.ShapeDtypeStruct((b, s, d), F32)
    if transposed:
        hspecs = [pl.BlockSpec((1, nh, LANES, tm), lambda bi, i: (bi, 0, 0, i)),
                  pl.BlockSpec((1, nh, tm, LANES), lambda bi, i: (bi, 0, i, 0)),
                  pl.BlockSpec((1, nh, 1, A_DV + ONES_ROWS, tm), lambda bi, i: (bi, 0, i, 0, 0))]
        hshapes = [jax.ShapeDtypeStruct((b, nh, LANES, s), BF16),
                   jax.ShapeDtypeStruct((b, nh, s, LANES), BF16),
                   jax.ShapeDtypeStruct((b, nh, s // tm, A_DV + ONES_ROWS, tm), BF16)]
    else:
        hspecs = [tok] * 3
        hshapes = [jax.ShapeDtypeStruct((b, s, d), BF16)] * 3
    wspec = pl.BlockSpec((d, d), lambda bi, i: (0, 0))
    tab = pl.BlockSpec((tm, LANES), lambda bi, i: (i % n_tab, 0))
    return pl.pallas_call(
        functools.partial(_qkv_a_kernel, transposed=transposed, q_scale=q_scale),
        grid=(b, s // tm),
        in_specs=[tok, pl.BlockSpec((1, d), lambda bi, i: (0, 0)), wspec, wspec, wspec, tab, tab],
        out_specs=[hspecs[0], tok, hspecs[1], tok, hspecs[2]],
        out_shape=[hshapes[0], fshape, hshapes[1], fshape, hshapes[2]],
        compiler_params=_params("parallel", "parallel"),
        name="qkv_a",
    )(x3, g, wq, wk, wv, cos, sin)


def _diff_lambda(lam_ref, lam_init):
    lv = lam_ref[...]
    s1 = jnp.sum(lv[0:1] * lv[1:2], axis=-1, keepdims=True)
    s2 = jnp.sum(lv[2:3] * lv[3:4], axis=-1, keepdims=True)
    return jnp.exp(s1) - jnp.exp(s2) + lam_init


def _diff_finish(o1, o2, lam, gsub, lam_init):
    d = o1 - lam * o2
    return (_rms_scale(d) * gsub * (1.0 - lam_init)).astype(BF16)


def _diff_prompt_kernel(lam_ref, gsub_ref, qt_ref, k_ref, vt_ref, o_ref, s_ref, p_ref, acc_ref, *, t, lam_init):
    qi = pl.program_id(2)
    nh = qt_ref.shape[1]
    row = lax.broadcasted_iota(jnp.int32, (A_DV, t), 0)
    zero = jnp.zeros((A_DV, t), BF16)
    qst = [jnp.concatenate([jnp.where(row < A_DK, qt_ref[0, h], zero), jnp.where(row >= A_DK, qt_ref[0, h], zero)],
                           axis=1) for h in range(nh)]

    def scores(h, tile, m_prev, mask):
        s = _dot(k_ref[0, h, pl.ds(pl.multiple_of(tile * t, t), t), :], qst[h])
        if mask is not None:
            s = jnp.where(mask, s, NEG)
        m = jnp.maximum(m_prev, jnp.max(s, axis=0, keepdims=True))
        s_ref[h] = s
        return m, jnp.exp2(m_prev - m)

    def accumulate(h, tile, alpha):
        acc_ref[h] = alpha * acc_ref[h] + _dot(vt_ref[0, h, tile], p_ref[h])

    def exponentials(h, m):
        p_ref[h] = jnp.exp2(s_ref[h] - m).astype(BF16)

    kk = lax.broadcasted_iota(jnp.int32, (t, 2 * t), 0)
    qq = lax.broadcasted_iota(jnp.int32, (t, 2 * t), 1)
    mask = _chunk_of(kk) <= _chunk_of(jnp.bitwise_and(qq, t - 1))
    m0 = jnp.full((1, 2 * t), NEG, F32)
    carry = []
    for h in range(nh):
        m, alpha = scores(h, qi, m0, mask)
        p_ref[h] = jnp.zeros((t, 2 * t), BF16)
        acc_ref[h] = jnp.zeros((A_DV + ONES_ROWS, 2 * t), F32)
        carry.append((m, alpha, jnp.ones((1, 2 * t), F32)))

    def v_tile(n):
        return jnp.where(n == 0, qi, jnp.maximum(n - 1, 0))

    def body(j, carry):
        out = []
        for h in range(nh):
            m_s, alpha_s, alpha_p = carry[h]
            accumulate(h, v_tile(j - 1), alpha_p)
            exponentials(h, m_s)
            m_new, alpha_new = scores(h, j, m_s, None)
            out.append((m_new, alpha_new, alpha_s))
        return tuple(out)

    carry = lax.fori_loop(0, qi, body, tuple(carry))
    lam = _diff_lambda(lam_ref, lam_init)
    for h in range(nh):
        m_s, alpha_s, alpha_p = carry[h]
        accumulate(h, v_tile(qi - 1), alpha_p)
        exponentials(h, m_s)
        accumulate(h, v_tile(qi), alpha_s)
        acc = acc_ref[h]
        o = (acc[:A_DV] * (1.0 / acc[A_DV:A_DV + 1])).T
        o_ref[0, :, h * A_DV:(h + 1) * A_DV] = _diff_finish(o[:t], o[t:], lam, gsub_ref[...], lam_init)


def _diff_prompt(lamv, gsub, qt, k, vt, lam_init):
    b, nh, _, s = qt.shape
    t = vt.shape[-1]
    hs = HEADS_PER_STEP
    return pl.pallas_call(
        functools.partial(_diff_prompt_kernel, t=t, lam_init=lam_init),
        grid=(b, nh // hs, s // t),
        in_specs=[pl.BlockSpec(lamv.shape, lambda bi, hi, qi: (0, 0)),
                  pl.BlockSpec((1, A_DV), lambda bi, hi, qi: (0, 0)),
                  pl.BlockSpec((1, hs, LANES, t), lambda bi, hi, qi: (bi, hi, 0, qi)),
                  pl.BlockSpec((1, hs, s, LANES), lambda bi, hi, qi: (bi, hi, 0, 0)),
                  pl.BlockSpec((1, hs, s // t, A_DV + ONES_ROWS, t), lambda bi, hi, qi: (bi, hi, 0, 0, 0))],
        out_specs=pl.BlockSpec((1, t, hs * LANES), lambda bi, hi, qi: (bi, qi, hi)),
        out_shape=jax.ShapeDtypeStruct((b, s, nh * LANES), BF16),
        scratch_shapes=[pltpu.VMEM((hs, t, 2 * t), F32), pltpu.VMEM((hs, t, 2 * t), BF16),
                        pltpu.VMEM((hs, A_DV + ONES_ROWS, 2 * t), F32)],
        compiler_params=_params("parallel", "parallel", "arbitrary"),
        name="diff_attn_prompt",
    )(lamv, gsub, qt, k, vt)


def _pad_rows(x, rows):
    return jnp.concatenate([x, jnp.zeros((rows - x.shape[0], x.shape[1]), x.dtype)], axis=0)


def _diff_sample_kernel(lam_ref, gsub_ref, q_ref, kc_ref, kn_ref, vc_ref, vn_ref, o_ref, *, lam_init):
    t = q_ref.shape[1]
    lam = _diff_lambda(lam_ref, lam_init)
    for h in range(q_ref.shape[2] // LANES):
        lanes = slice(h * LANES, (h + 1) * LANES)
        qs = _stack_halves(q_ref[0, :, lanes])
        kn = _pad_rows(kn_ref[0, :, lanes], LANES)
        vn = _pad_rows(vn_ref[0, :, lanes], LANES)
        sc = _dot_nt(qs, kc_ref[0, :, lanes].astype(BF16))
        sn = _dot_nt(qs, kn)
        col = lax.broadcasted_iota(jnp.int32, sn.shape, 1)
        sn = jnp.where(col < t, sn, NEG)
        m = jnp.maximum(jnp.max(sc, axis=-1, keepdims=True), jnp.max(sn, axis=-1, keepdims=True))
        pc = jnp.exp(sc - m)
        pn = jnp.exp(sn - m)
        l = jnp.sum(pc, axis=-1, keepdims=True) + jnp.sum(pn, axis=-1, keepdims=True)
        acc = _dot(pc.astype(BF16), vc_ref[0, :, lanes].astype(BF16)) + _dot(pn.astype(BF16), vn)
        o = acc * (1.0 / l)
        o_ref[0, :, lanes] = _diff_finish(o[:t], o[t:], lam, gsub_ref[...], lam_init)


def _diff_sample(lamv, gsub, q, kc, kn, vc, vn, lam_init):
    b, t, d = q.shape
    past = kc.shape[1]
    w = SAMPLE_HEADS * LANES
    new = pl.BlockSpec((1, t, w), lambda bi, hi: (bi, 0, hi))
    old = pl.BlockSpec((1, past, w), lambda bi, hi: (bi, 0, hi))
    return pl.pallas_call(
        functools.partial(_diff_sample_kernel, lam_init=lam_init),
        grid=(b, d // w),
        in_specs=[pl.BlockSpec(lamv.shape, lambda bi, hi: (0, 0)),
                  pl.BlockSpec((1, A_DV), lambda bi, hi: (0, 0)), new, old, new, old, new],
        out_specs=new,
        out_shape=jax.ShapeDtypeStruct((b, t, d), BF16),
        compiler_params=_params("parallel", "parallel"),
        name="diff_attn_sample",
    )(lamv, gsub, q, kc, kn, vc, vn)


def _moe_gates(hf, wr_ref, br_ref):
    hi = hf.astype(BF16)
    lo = (hf - hi.astype(F32)).astype(BF16)
    both = _dot(hi, wr_ref[...])
    lg = both[:, :LANES] + both[:, LANES:] + _dot(lo, wr_ref[:, :LANES]) + br_ref[...]
    lane = lax.broadcasted_iota(jnp.int32, lg.shape, 1).astype(F32)
    big = float(LANES)

    def first_argmax(v):
        mx = jnp.max(v, axis=-1, keepdims=True)
        return mx, jnp.min(jnp.where(v == mx, lane, big), axis=-1, keepdims=True)

    is_g = lane < N_GROUPS
    gmax, gidx = first_argmax(jnp.where(is_g, lg, NEG))
    gval = 1.0 / jnp.sum(jnp.where(is_g, jnp.exp(lg - gmax), 0.0), axis=-1, keepdims=True)
    lo = N_GROUPS + EXPERTS_PER_GROUP * gidx
    el = jnp.where((lane >= lo) & (lane < lo + EXPERTS_PER_GROUP), lg, NEG)
    v1, i1 = first_argmax(el)
    v2, i2 = first_argmax(jnp.where(lane == i1, NEG, el))
    e2 = jnp.exp(v2 - v1)
    inv = gval / (1.0 + e2)
    gates = jnp.where(lane == i1 - lo, inv, 0.0) + jnp.where(lane == i2 - lo, inv * e2, 0.0)
    return jnp.where(lane == gidx, 1.0, 0.0), gates


def _moe_kernel(a_ref, wo_ref, x_ref, g_ref, wr_ref, br_ref, wg_ref, wu_ref, wd_ref, gf_ref, o_ref,
                h_ref, gsplit_ref, rank_ref, rank_t_ref, pos_ref, y_ref, base_ref, *, final_norm):
    grp = pl.program_id(1)
    nb = x_ref.shape[0]

    @pl.when(grp == 0)
    def _():
        x = x_ref[...] + _dot(a_ref[...], wo_ref[...])
        o_ref[...] = x
        hf = _rms_scale(x) * g_ref[...]
        h_ref[...] = hf.astype(BF16)
        member, gates = _moe_gates(hf, wr_ref, br_ref)
        rest = gates
        split = jnp.zeros_like(gates)
        for i in range(3):
            part = rest.astype(BF16).astype(F32)
            split = split + pltpu.roll(part, EXPERTS_PER_GROUP * i, 1)
            rest = rest - part
        gsplit_ref[...] = split.astype(BF16)
        earlier = (lax.broadcasted_iota(jnp.int32, (nb, nb), 1) < lax.broadcasted_iota(jnp.int32, (nb, nb), 0))
        rank = jnp.where(member > 0.0, _dot(jnp.where(earlier, 1.0, 0.0).astype(BF16), member.astype(BF16)) + 1.0, 0.0)
        rank_ref[...] = rank
        rank_t_ref[...] = rank.T
        pos_ref[...] = jnp.full(pos_ref.shape, -1.0, F32)
        y_ref[...] = jnp.zeros_like(y_ref)
        base_ref[0] = 0

    base = base_ref[0]
    rrow = rank_t_ref[pl.ds(grp, 1), :]
    n_chunks = (jnp.max(rrow).astype(jnp.int32) + MOE_CHUNK - 1) // MOE_CHUNK
    lane = lax.broadcasted_iota(jnp.int32, (nb, LANES), 1)
    rcol = jnp.sum(jnp.where(lane == grp, rank_ref[...], 0.0), axis=-1, keepdims=True)
    pos_ref[...] = jnp.where(rcol > 0.0, base.astype(F32) + rcol - 1.0, pos_ref[...])

    def chunk(c, carry):
        want = (lax.broadcasted_iota(jnp.int32, (MOE_CHUNK, nb), 0) + (c * MOE_CHUNK + 1)).astype(F32)
        onehot = jnp.where(rrow == want, 1.0, 0.0).astype(BF16)
        xg = _dot(onehot, h_ref[...]).astype(BF16)
        gs3 = _dot(onehot, gsplit_ref[...])
        gs = (gs3 + pltpu.roll(gs3, LANES - EXPERTS_PER_GROUP, 1)
              + pltpu.roll(gs3, LANES - 2 * EXPERTS_PER_GROUP, 1))
        parts = []
        for j in range(EXPERTS_PER_GROUP):
            a = _dot(xg, wg_ref[j])
            u = _dot(xg, wu_ref[j])
            parts.append((a * (1.0 / (1.0 + jnp.exp(-a))) * u * gs[:, j:j + 1]).astype(BF16))
        y = _dot(jnp.concatenate(parts, axis=1), wd_ref[0])
        off = pl.multiple_of(base + c * MOE_CHUNK, MOE_CHUNK)
        y_ref[pl.ds(off, MOE_CHUNK), :] = y.astype(BF16)
        return carry

    lax.fori_loop(0, n_chunks, chunk, 0)
    base_ref[0] = base + n_chunks * MOE_CHUNK

    @pl.when(grp == N_GROUPS - 1)
    def _():
        slot = lax.broadcasted_iota(jnp.int32, (nb, y_ref.shape[0]), 1).astype(F32)
        scatter = jnp.where(pos_ref[...] == slot, 1.0, 0.0).astype(BF16)
        out = o_ref[...] + _dot(scatter, y_ref[...])
        if final_norm:
            out = _rms_scale(out) * gf_ref[...]
        o_ref[...] = out


def _moe(a, wo, x, g, wr, br, wg, wu, wd, gf, final_norm):
    n, d = x.shape
    tm = min(MOE_TILE, n)
    ff = EXPERTS_PER_GROUP * EXPERT_FF
    slots = tm + N_GROUPS * MOE_CHUNK
    once = pl.Buffered(1)
    tok = pl.BlockSpec((tm, d), lambda i, e: (i, 0))
    vec = pl.BlockSpec((1, d), lambda i, e: (0, 0))
    experts = pl.BlockSpec((EXPERTS_PER_GROUP, d, EXPERT_FF), lambda i, e: (e, 0, 0))
    return pl.pallas_call(
        functools.partial(_moe_kernel, final_norm=final_norm),
        grid=(n // tm, N_GROUPS),
        in_specs=[tok, pl.BlockSpec((d, d), lambda i, e: (0, 0), pipeline_mode=once), tok, vec,
                  pl.BlockSpec((d, 2 * LANES), lambda i, e: (0, 0), pipeline_mode=once),
                  pl.BlockSpec((1, LANES), lambda i, e: (0, 0)),
                  experts, experts,
                  pl.BlockSpec((1, ff, d), lambda i, e: (e, 0, 0)),
                  vec],
        out_specs=tok,
        out_shape=jax.ShapeDtypeStruct((n, d), F32),
        scratch_shapes=[pltpu.VMEM((tm, d), BF16), pltpu.VMEM((tm, LANES), BF16),
                        pltpu.VMEM((tm, LANES), F32), pltpu.VMEM((LANES, tm), F32),
                        pltpu.VMEM((tm, 1), F32), pltpu.VMEM((slots, d), BF16),
                        pltpu.SMEM((1,), jnp.int32)],
        compiler_params=_params("parallel", "arbitrary"),
        name="hier_moe",
    )(a, wo, x, g, wr, br, wg, wu, wd, gf)


def _kvq_b_kernel(x_ref, gkv_ref, gq_ref, wkv_ref, wq_ref, k_ref, v_ref, q_ref, *, lead_pad):
    def compute():
        xh = _rms_scale(x_ref[0])
        kv = _dot((xh * gkv_ref[...]).astype(BF16), wkv_ref[...])
        k_ref[0] = kv[:, :D_MODEL].astype(k_ref.dtype)
        v_ref[0] = kv[:, D_MODEL:].astype(v_ref.dtype)
        q = _dot((xh * gq_ref[...]).astype(BF16), wq_ref[...])
        q_ref[0] = (q * (B_DH ** -0.5 * LOG2E)).astype(BF16)

    if lead_pad:
        i = pl.program_id(1)

        @pl.when(i == 0)
        def _():
            k_ref[...] = jnp.zeros_like(k_ref)
            v_ref[...] = jnp.zeros_like(v_ref)

        pl.when(i > 0)(compute)
    else:
        compute()


def _kvq_b(x3, gkv, gq, wkv, wq, kv_dtype, lead_pad, row_block=None):
    b, s, d = x3.shape
    tm = min(ROW_TILE, s)
    if lead_pad:
        assert tm == B_WINDOW
        grid = (b, s // tm + 1)
        src = lambda bi, i: (bi, jnp.maximum(i - 1, 0), 0)
        dst = lambda bi, i: (bi, i, 0)
        s_kv, s_q = s + B_WINDOW, s
    elif row_block is not None:
        grid = (b, 1)
        src = lambda bi, i: (bi, row_block, 0)
        dst = lambda bi, i: (bi, 0, 0)
        s_kv = s_q = tm
    else:
        grid = (b, s // tm)
        src = dst = lambda bi, i: (bi, i, 0)
        s_kv = s_q = s
    vec = pl.BlockSpec((1, d), lambda bi, i: (0, 0))
    return pl.pallas_call(
        functools.partial(_kvq_b_kernel, lead_pad=lead_pad),
        grid=grid,
        in_specs=[pl.BlockSpec((1, tm, d), src), vec, vec,
                  pl.BlockSpec((d, 2 * d), lambda bi, i: (0, 0)),
                  pl.BlockSpec((d, d), lambda bi, i: (0, 0))],
        out_specs=[pl.BlockSpec((1, tm, d), dst), pl.BlockSpec((1, tm, d), dst),
                   pl.BlockSpec((1, tm, d), src if lead_pad else dst)],
        out_shape=[jax.ShapeDtypeStruct((b, s_kv, d), kv_dtype), jax.ShapeDtypeStruct((b, s_kv, d), kv_dtype),
                   jax.ShapeDtypeStruct((b, s_q, d), BF16)],
        compiler_params=_params("parallel", "arbitrary"),
        name="kvq_b",
    )(x3, gkv, gq, wkv, wq)


def _band_bias_kernel(u_ref, o_ref):
    x = jnp.broadcast_to(u_ref[0], (BAND_TQ, 2 * B_WINDOW))
    bias = pltpu.roll(x, 0, 1, stride=1, stride_axis=0)[:, :BAND_KW]
    qc = _chunk_of(lax.broadcasted_iota(jnp.int32, bias.shape, 0))
    kc = _chunk_of(lax.broadcasted_iota(jnp.int32, bias.shape, 1))
    ok = (kc >= qc) & (kc <= qc + B_WINDOW // CHUNK)
    o_ref[0] = jnp.where(ok, bias * LOG2E, NEG)


def _band_bias(rel_bias):
    edge = jnp.broadcast_to(rel_bias[:, 2 * REL_CLIP:], (B_HEADS, REL_CLIP))
    u = jnp.concatenate([edge, rel_bias[:, :0:-1], edge], axis=1)[:, None, :]
    assert u.shape[-1] == 2 * B_WINDOW
    return pl.pallas_call(
        _band_bias_kernel,
        grid=(B_HEADS,),
        in_specs=[pl.BlockSpec((1, 1, 2 * B_WINDOW), lambda h: (h, 0, 0))],
        out_specs=pl.BlockSpec((1, BAND_TQ, BAND_KW), lambda h: (h, 0, 0)),
        out_shape=jax.ShapeDtypeStruct((B_HEADS, BAND_TQ, BAND_KW), F32),
        compiler_params=_params("parallel"),
        name="band_bias",
    )(u)


def _pick_halves(o, t):
    lane = lax.broadcasted_iota(jnp.int32, (t, LANES), 1)
    return jnp.where(lane < HEAD_W, o[:t], o[t:])


def _band_prompt_kernel(q_ref, k_ref, v_ref, bias_ref, o_ref, p_ref, *, n_tiles):
    t = BAND_TQ

    def probabilities(off, lead):
        kw = k_ref[0, pl.ds(off, BAND_KW), :]
        qs = _stack_halves(q_ref[0, pl.ds(off, t), :])
        for g in range(2 * t // BAND_ROWS):
            r0 = g * BAND_ROWS
            s = _dot_nt(qs[r0:r0 + BAND_ROWS], kw) + bias_ref[r0 // t, r0 % t:r0 % t + BAND_ROWS, :]
            if lead:
                c = lax.broadcasted_iota(jnp.int32, s.shape, 1)
                s = jnp.where(c >= B_WINDOW - off, s, NEG)
            m = jnp.max(s, axis=-1, keepdims=True)
            p_ref[r0:r0 + BAND_ROWS, :] = jnp.exp2(s - m).astype(BF16)

    def outputs(off):
        vx = jnp.concatenate([v_ref[0, pl.ds(off, BAND_KW), :], jnp.ones((BAND_KW, LANES), BF16)], axis=1)
        o = _dot(p_ref[...], vx)
        o = o[:, :LANES] * (1.0 / o[:, LANES:])
        o_ref[0, pl.ds(off, t), :] = _pick_halves(o, t).astype(BF16)

    n_lead = B_WINDOW // t
    for ti in range(n_lead):
        if ti > 0:
            outputs((ti - 1) * t)
        probabilities(ti * t, True)

    def body(ti, carry):
        outputs(pl.multiple_of((ti - 1) * t, t))
        probabilities(pl.multiple_of(ti * t, t), False)
        return carry

    lax.fori_loop(n_lead, n_tiles, body, 0)
    outputs((n_tiles - 1) * t)


def _band_prompt(q, kpad, vpad, bias):
    b, s, d = q.shape
    t = BAND_TQ
    nhp = d // LANES
    win = pl.BlockSpec((1, s + B_WINDOW, LANES), lambda hp, bi: (bi, 0, hp))
    seq = pl.BlockSpec((1, s, LANES), lambda hp, bi: (bi, 0, hp))
    return pl.pallas_call(
        functools.partial(_band_prompt_kernel, n_tiles=s // t),
        grid=(nhp, b),
        in_specs=[seq, win, win, pl.BlockSpec((2, t, BAND_KW), lambda hp, bi: (hp, 0, 0))],
        out_specs=seq,
        out_shape=jax.ShapeDtypeStruct((b, s, d), BF16),
        scratch_shapes=[pltpu.VMEM((2 * t, BAND_KW), BF16)],
        compiler_params=_params("parallel", "parallel"),
        name="band_attn_prompt",
    )(q, kpad, vpad, bias)


def _band_sample_kernel(q_ref, kc_ref, kn_ref, vc_ref, vn_ref, bias_ref, o_ref):
    t = q_ref.shape[1]
    wb = kc_ref.shape[1]
    col = lax.broadcasted_iota(jnp.int32, (2 * t, LANES), 1)
    for hp in range(q_ref.shape[2] // LANES):
        lanes = slice(hp * LANES, (hp + 1) * LANES)
        qs = _stack_halves(q_ref[0, :, lanes])
        kn = _pad_rows(kn_ref[0, :, lanes].astype(BF16), LANES)
        vn = _pad_rows(vn_ref[0, :, lanes].astype(BF16), LANES)
        bias = bias_ref[2 * hp:2 * hp + 2].reshape(2 * t, BAND_KW)
        sc = _dot_nt(qs, kc_ref[0, :, lanes].astype(BF16)) + bias[:, :wb]
        sn = jnp.where(col < t, _dot_nt(qs, kn) + bias[:, wb:wb + LANES], NEG)
        m = jnp.maximum(jnp.max(sc, axis=-1, keepdims=True), jnp.max(sn, axis=-1, keepdims=True))
        pc = jnp.exp2(sc - m)
        pn = jnp.exp2(sn - m)
        l = jnp.sum(pc, axis=-1, keepdims=True) + jnp.sum(pn, axis=-1, keepdims=True)
        o = (_dot(pc.astype(BF16), vc_ref[0, :, lanes].astype(BF16)) + _dot(pn.astype(BF16), vn)) * (1.0 / l)
        o_ref[0, :, lanes] = _pick_halves(o, t).astype(BF16)


def _band_sample(q, kc, kn, vc, vn, bias):
    b, t, d = q.shape
    wb = kc.shape[1]
    assert wb == B_WINDOW
    new = pl.BlockSpec((1, t, d), lambda bi: (bi, 0, 0))
    old = pl.BlockSpec((1, wb, d), lambda bi: (bi, 0, 0))
    return pl.pallas_call(
        _band_sample_kernel,
        grid=(b,),
        in_specs=[new, old, new, old, new, pl.BlockSpec((B_HEADS, t, BAND_KW), lambda bi: (0, 0, 0))],
        out_specs=new,
        out_shape=jax.ShapeDtypeStruct((b, t, d), BF16),
        compiler_params=_params("parallel"),
        name="band_attn_sample",
    )(q, kc, kn, vc, vn, bias)


def kernel(x_prompt, x_sample, cache_k_a, cache_v_a, cache_k_b, cache_v_b, g_attn, g_ffn, w_q_a, w_k_a, w_v_a, lam_q1, lam_k1, lam_q2, lam_k2, g_sub_a, w_o_a, g_kv, w_kv_b, w_q_b, rel_bias_b, w_o_b, w_router_g, b_router_g, w_router_e, b_router_e, w_gate, w_up, w_down, g_final):
    bp, sp, d = x_prompt.shape
    bs, ts, _ = x_sample.shape
    past = cache_k_a.shape[2]
    wb = cache_k_b.shape[1]
    n_p, n_s = bp * sp, bs * ts
    assert d == D_MODEL and sp % ROW_TILE == 0 and ROW_TILE % ts == 0 and ts <= LANES
    depth = g_attn.shape[0]
    n_a = w_q_a.shape[0]
    assert depth == 2 and n_a == 1 and w_q_b.shape[0] == 1

    row = lambda v: v.reshape(1, -1).astype(F32)
    bf = lambda w: w.astype(BF16)

    def moe_weights(l):
        wr = jnp.concatenate([w_router_g[l], w_router_e[l].transpose(1, 0, 2).reshape(d, N_EXPERTS)], axis=1)
        wr = jnp.pad(wr, ((0, 0), (0, LANES - wr.shape[1])))
        br = jnp.concatenate([b_router_g[l], b_router_e[l].reshape(-1)])
        br = jnp.pad(br, (0, LANES - br.shape[0]))[None, :]
        wr = wr.astype(F32)
        wr_hi = bf(wr)
        wr_split = jnp.concatenate([wr_hi, bf(wr - wr_hi.astype(F32))], axis=1)
        wd = bf(w_down[l]).reshape(N_GROUPS, EXPERTS_PER_GROUP * EXPERT_FF, d)
        return row(g_ffn[l]), wr_split, br.astype(F32), bf(w_gate[l]), bf(w_up[l]), wd

    xp = x_prompt.reshape(n_p, d)
    xs = x_sample.reshape(n_s, d)

    lam_init = 0.8 - 0.6 * math.exp(-0.3 * 0)
    lamv = jnp.stack([lam_q1[0], lam_k1[0], lam_q2[0], lam_k2[0]]).astype(F32)
    gsub = row(g_sub_a[0])
    wq, wk, wv, wo = bf(w_q_a[0]), bf(w_k_a[0]), bf(w_v_a[0]), bf(w_o_a[0])
    g0 = row(g_attn[0])

    cos_p, sin_p = _rope_table(sp, sp, 0, min(ROW_TILE, sp))
    qp, kfp, kbp, vfp, vbp = _qkv_a(x_prompt, g0, wq, wk, wv, cos_p, sin_p, True, A_DK ** -0.5 * LOG2E)
    ap = _diff_prompt(lamv, gsub, qp, kbp, vbp, lam_init)

    tm_s = min(ROW_TILE, n_s)
    cos_s, sin_s = _rope_table(tm_s, ts, past, tm_s)
    qs, kfs, _, vfs, _ = _qkv_a(xs.reshape(1, n_s, d), g0, wq, wk, wv, cos_s, sin_s, False, A_DK ** -0.5)
    kfs3, vfs3 = kfs.reshape(bs, ts, d), vfs.reshape(bs, ts, d)
    a_s = _diff_sample(lamv, gsub, qs.reshape(bs, ts, d), cache_k_a[0].reshape(bs, past, d), bf(kfs3),
                       cache_v_a[0].reshape(bs, past, d), bf(vfs3), lam_init)

    mw = moe_weights(0)
    gfin = row(g_final)
    xp = _moe(ap.reshape(n_p, d), wo, xp, *mw, gfin, final_norm=False)
    xs = _moe(a_s.reshape(n_s, d), wo, xs, *mw, gfin, final_norm=False)

    gkv, g1 = row(g_kv), row(g_attn[1])
    wkv, wqb, wob = bf(w_kv_b), bf(w_q_b[0]), bf(w_o_b[0])
    bias = _band_bias(rel_bias_b[0].astype(F32))

    xp3 = xp.reshape(bp, sp, d)
    kpad, vpad, qb = _kvq_b(xp3, gkv, g1, wkv, wqb, BF16, lead_pad=True)
    kbt, vbt, _ = _kvq_b(xp3, gkv, g1, wkv, wqb, F32, lead_pad=False, row_block=sp // ROW_TILE - 1)
    ab = _band_prompt(qb, kpad, vpad, bias)

    kns, vns, qbs = _kvq_b(xs.reshape(1, n_s, d), gkv, g1, wkv, wqb, F32, lead_pad=False)
    kns3, vns3 = kns.reshape(bs, ts, d), vns.reshape(bs, ts, d)
    abs_ = _band_sample(qbs.reshape(bs, ts, d), cache_k_b.reshape(bs, wb, d), kns3,
                        cache_v_b.reshape(bs, wb, d), vns3, bias)

    mw = moe_weights(1)
    y_prompt = _moe(ab.reshape(n_p, d), wob, xp, *mw, gfin, final_norm=True).reshape(bp, sp, d)
    y_sample = _moe(abs_.reshape(n_s, d), wob, xs, *mw, gfin, final_norm=True).reshape(bs, ts, d)

    wp = min(B_WINDOW, sp)
    assert wp == ROW_TILE
    k_a_prompt = kfp.reshape(1, bp, sp, 2 * A_HEADS, A_DK)
    v_a_prompt = vfp.reshape(1, bp, sp, A_HEADS, A_DV)
    k_a_sample = kfs.reshape(1, bs, ts, 2 * A_HEADS, A_DK)
    v_a_sample = vfs.reshape(1, bs, ts, A_HEADS, A_DV)
    k_b_prompt = kbt.reshape(bp, wp, B_HEADS, B_DH)
    v_b_prompt = vbt.reshape(bp, wp, B_HEADS, B_DH)
    k_b_sample = jnp.concatenate([cache_k_b[:, ts:], kns.reshape(bs, ts, B_HEADS, B_DH)], axis=1)
    v_b_sample = jnp.concatenate([cache_v_b[:, ts:], vns.reshape(bs, ts, B_HEADS, B_DH)], axis=1)
    return (y_prompt, y_sample, k_a_prompt, v_a_prompt, k_b_prompt, v_b_prompt,
            k_a_sample, v_a_sample, k_b_sample, v_b_sample)
```

```python
import functools
import math

import jax
import jax.numpy as jnp
from jax import lax
from jax.experimental import pallas as pl
from jax.experimental.pallas import tpu as pltpu

F32 = jnp.float32
BF16 = jnp.bfloat16

EPS = 1e-6
CHUNK = 64
D_MODEL = 1024
A_HEADS = 8
A_DK = 64
A_DV = 128
B_HEADS = 16
B_DH = 64
B_WINDOW = 512
REL_CLIP = 256
N_GROUPS = 4
EXPERTS_PER_GROUP = 4
N_EXPERTS = 16
EXPERT_FF = 512
ROPE_THETA = 10000.0

LANES = 128
HEAD_W = LANES // 2
assert A_DK == HEAD_W and B_DH == HEAD_W and A_DV == LANES
NEG = -1e30
LOG2E = math.log2(math.e)
VMEM_LIMIT = 58 * 1024 * 1024

ROW_TILE = 512
MOE_TILE = 1024
MOE_CHUNK = 128
HEADS_PER_STEP = 4
ONES_ROWS = 16
BAND_TQ = 256
BAND_KW = BAND_TQ + B_WINDOW
BAND_ROWS = 128
SAMPLE_HEADS = 2


def _params(*sem, fuse_inputs=None):
    return pltpu.CompilerParams(dimension_semantics=sem, vmem_limit_bytes=VMEM_LIMIT, allow_input_fusion=fuse_inputs)


def _rms_scale(x):
    return x * lax.rsqrt(jnp.mean(x * x, axis=-1, keepdims=True) + EPS)


def _dot(a, b):
    return jnp.dot(a, b, preferred_element_type=F32)


def _dot_nt(a, b):
    return lax.dot_general(a, b, (((1,), (1,)), ((), ())), preferred_element_type=F32)


def _chunk_of(pos):
    return jnp.right_shift(pos, CHUNK.bit_length() - 1)


def _stack_halves(q):
    lane = lax.broadcasted_iota(jnp.int32, q.shape, 1)
    zero = jnp.zeros_like(q)
    return jnp.concatenate([jnp.where(lane < HEAD_W, q, zero), jnp.where(lane >= HEAD_W, q, zero)], axis=0)


def _rope_table_kernel(inv_ref, cos_ref, sin_ref, *, period, offset, rows):
    i = pl.program_id(0)
    row = lax.broadcasted_iota(jnp.int32, (rows, LANES), 0) + i * rows
    lane = lax.broadcasted_iota(jnp.int32, (rows, LANES), 1)
    pos = (offset + lax.rem(row, period)).astype(F32)
    ang = pos * inv_ref[...]
    s = jnp.sin(ang)
    cos_ref[...] = jnp.cos(ang)
    sin_ref[...] = jnp.where(lax.rem(lane, A_DK) < A_DK // 2, -s, s)


def _rope_table(n_rows, period, offset, rows):
    half = A_DK // 2
    inv = jnp.power(ROPE_THETA, -jnp.arange(half, dtype=F32) / half)
    inv = jnp.tile(inv, LANES // half)[None, :]
    return pl.pallas_call(
        functools.partial(_rope_table_kernel, period=period, offset=offset, rows=rows),
        grid=(n_rows // rows,),
        in_specs=[pl.BlockSpec((1, LANES), lambda i: (0, 0))],
        out_specs=[pl.BlockSpec((rows, LANES), lambda i: (i, 0))] * 2,
        out_shape=[jax.ShapeDtypeStruct((n_rows, LANES), F32)] * 2,
        compiler_params=_params("arbitrary"),
        name="rope_table",
    )(inv)


def _qkv_a_kernel(x_ref, g_ref, wq_ref, wk_ref, wv_ref, cos_ref, sin_ref,
                  q_ref, kf_ref, kb_ref, vf_ref, vb_ref, *, transposed, q_scale):
    h = (_rms_scale(x_ref[0]) * g_ref[...]).astype(BF16)
    cos = cos_ref[...]
    sin = sin_ref[...]
    lane = lax.broadcasted_iota(jnp.int32, cos.shape, 1)
    half = A_DK // 2
    first = lax.rem(lane, A_DK) < half

    yq = _dot(h, wq_ref[...])
    yk = _dot(h, wk_ref[...])
    yv = _dot(h, wv_ref[...])
    for c in range(D_MODEL // LANES):
        sl = slice(c * LANES, (c + 1) * LANES)

        def rope(y):
            yc = y[:, sl]
            rot = jnp.where(first, pltpu.roll(yc, LANES - half, 1), pltpu.roll(yc, half, 1))
            return yc * cos + rot * sin

        rq = rope(yq) * q_scale
        rk = rope(yk)
        vc = yv[:, sl]
        kf_ref[0, :, sl] = rk
        vf_ref[0, :, sl] = vc
        if transposed:
            q_ref[0, c] = rq.T.astype(BF16)
            kb_ref[0, c] = rk.astype(BF16)
            vb_ref[0, c, 0, :A_DV] = vc.T.astype(BF16)
            vb_ref[0, c, 0, A_DV:] = jnp.ones((ONES_ROWS, vc.shape[0]), BF16)
        else:
            q_ref[0, :, sl] = rq.astype(BF16)
            kb_ref[0, :, sl] = rk.astype(BF16)
            vb_ref[0, :, sl] = vc.astype(BF16)


def _qkv_a(x3, g, wq, wk, wv, cos, sin, transposed, q_scale):
    b, s, d = x3.shape
    tm = min(ROW_TILE, s)
    n_tab = cos.shape[0] // tm
    nh = d // LANES
    tok = pl.BlockSpec((1, tm, d), lambda bi, i: (bi, i, 0))
    fshape = jax.ShapeDtypeStruct((b, s, d), F32)
    if transposed:
        hspecs = [pl.BlockSpec((1, nh, LANES, tm), lambda bi, i: (bi, 0, 0, i)),
                  pl.BlockSpec((1, nh, tm, LANES), lambda bi, i: (bi, 0, i, 0)),
                  pl.BlockSpec((1, nh, 1, A_DV + ONES_ROWS, tm), lambda bi, i: (bi, 0, i, 0, 0))]
        hshapes = [jax.ShapeDtypeStruct((b, nh, LANES, s), BF16),
                   jax.ShapeDtypeStruct((b, nh, s, LANES), BF16),
                   jax.ShapeDtypeStruct((b, nh, s // tm, A_DV + ONES_ROWS, tm), BF16)]
    else:
        hspecs = [tok] * 3
        hshapes = [jax.ShapeDtypeStruct((b, s, d), BF16)] * 3
    wspec = pl.BlockSpec((d, d), lambda bi, i: (0, 0))
    tab = pl.BlockSpec((tm, LANES), lambda bi, i: (i % n_tab, 0))
    return pl.pallas_call(
        functools.partial(_qkv_a_kernel, transposed=transposed, q_scale=q_scale),
        grid=(b, s // tm),
        in_specs=[tok, pl.BlockSpec((1, d), lambda bi, i: (0, 0)), wspec, wspec, wspec, tab, tab],
        out_specs=[hspecs[0], tok, hspecs[1], tok, hspecs[2]],
        out_shape=[hshapes[0], fshape, hshapes[1], fshape, hshapes[2]],
        compiler_params=_params("parallel", "parallel"),
        name="qkv_a",
    )(x3, g, wq, wk, wv, cos, sin)


def _diff_lambda(lam_ref, lam_init):
    lv = lam_ref[...]
    s1 = jnp.sum(lv[0:1] * lv[1:2], axis=-1, keepdims=True)
    s2 = jnp.sum(lv[2:3] * lv[3:4], axis=-1, keepdims=True)
    return jnp.exp(s1) - jnp.exp(s2) + lam_init


def _diff_finish(o1, o2, lam, gsub, lam_init):
    d = o1 - lam * o2
    return (_rms_scale(d) * gsub * (1.0 - lam_init)).astype(BF16)


def _diff_prompt_kernel(lam_ref, gsub_ref, qt_ref, k_ref, vt_ref, o_ref, s_ref, p_ref, acc_ref, *, t, lam_init):
    qi = pl.program_id(2)
    nh = qt_ref.shape[1]
    row = lax.broadcasted_iota(jnp.int32, (A_DV, t), 0)
    zero = jnp.zeros((A_DV, t), BF16)
    qst = [jnp.concatenate([jnp.where(row < A_DK, qt_ref[0, h], zero), jnp.where(row >= A_DK, qt_ref[0, h], zero)],
                           axis=1) for h in range(nh)]

    def scores(h, tile, m_prev, mask):
        s = _dot(k_ref[0, h, pl.ds(pl.multiple_of(tile * t, t), t), :], qst[h])
        if mask is not None:
            s = jnp.where(mask, s, NEG)
        m = jnp.maximum(m_prev, jnp.max(s, axis=0, keepdims=True))
        s_ref[h] = s
        return m, jnp.exp2(m_prev - m)

    def accumulate(h, tile, alpha):
        acc_ref[h] = alpha * acc_ref[h] + _dot(vt_ref[0, h, tile], p_ref[h])

    def exponentials(h, m):
        p_ref[h] = jnp.exp2(s_ref[h] - m).astype(BF16)

    kk = lax.broadcasted_iota(jnp.int32, (t, 2 * t), 0)
    qq = lax.broadcasted_iota(jnp.int32, (t, 2 * t), 1)
    mask = _chunk_of(kk) <= _chunk_of(jnp.bitwise_and(qq, t - 1))
    m0 = jnp.full((1, 2 * t), NEG, F32)
    carry = []
    for h in range(nh):
        m, alpha = scores(h, qi, m0, mask)
        p_ref[h] = jnp.zeros((t, 2 * t), BF16)
        acc_ref[h] = jnp.zeros((A_DV + ONES_ROWS, 2 * t), F32)
        carry.append((m, alpha, jnp.ones((1, 2 * t), F32)))

    def v_tile(n):
        return jnp.where(n == 0, qi, jnp.maximum(n - 1, 0))

    def body(j, carry):
        out = []
        for h in range(nh):
            m_s, alpha_s, alpha_p = carry[h]
            accumulate(h, v_tile(j - 1), alpha_p)
            exponentials(h, m_s)
            m_new, alpha_new = scores(h, j, m_s, None)
            out.append((m_new, alpha_new, alpha_s))
        return tuple(out)

    carry = lax.fori_loop(0, qi, body, tuple(carry))
    lam = _diff_lambda(lam_ref, lam_init)
    for h in range(nh):
        m_s, alpha_s, alpha_p = carry[h]
        accumulate(h, v_tile(qi - 1), alpha_p)
        exponentials(h, m_s)
        accumulate(h, v_tile(qi), alpha_s)
        acc = acc_ref[h]
        o = (acc[:A_DV] * (1.0 / acc[A_DV:A_DV + 1])).T
        o_ref[0, :, h * A_DV:(h + 1) * A_DV] = _diff_finish(o[:t], o[t:], lam, gsub_ref[...], lam_init)


def _diff_prompt(lamv, gsub, qt, k, vt, lam_init):
    b, nh, _, s = qt.shape
    t = vt.shape[-1]
    hs = HEADS_PER_STEP
    return pl.pallas_call(
        functools.partial(_diff_prompt_kernel, t=t, lam_init=lam_init),
        grid=(b, nh // hs, s // t),
        in_specs=[pl.BlockSpec(lamv.shape, lambda bi, hi, qi: (0, 0)),
                  pl.BlockSpec((1, A_DV), lambda bi, hi, qi: (0, 0)),
                  pl.BlockSpec((1, hs, LANES, t), lambda bi, hi, qi: (bi, hi, 0, qi)),
                  pl.BlockSpec((1, hs, s, LANES), lambda bi, hi, qi: (bi, hi, 0, 0)),
                  pl.BlockSpec((1, hs, s // t, A_DV + ONES_ROWS, t), lambda bi, hi, qi: (bi, hi, 0, 0, 0))],
        out_specs=pl.BlockSpec((1, t, hs * LANES), lambda bi, hi, qi: (bi, qi, hi)),
        out_shape=jax.ShapeDtypeStruct((b, s, nh * LANES), BF16),
        scratch_shapes=[pltpu.VMEM((hs, t, 2 * t), F32), pltpu.VMEM((hs, t, 2 * t), BF16),
                        pltpu.VMEM((hs, A_DV + ONES_ROWS, 2 * t), F32)],
        compiler_params=_params("parallel", "parallel", "arbitrary"),
        name="diff_attn_prompt",
    )(lamv, gsub, qt, k, vt)


def _pad_rows(x, rows):
    return jnp.concatenate([x, jnp.zeros((rows - x.shape[0], x.shape[1]), x.dtype)], axis=0)


def _diff_sample_kernel(lam_ref, gsub_ref, q_ref, kc_ref, kn_ref, vc_ref, vn_ref, o_ref, *, lam_init):
    t = q_ref.shape[1]
    lam = _diff_lambda(lam_ref, lam_init)
    for h in range(q_ref.shape[2] // LANES):
        lanes = slice(h * LANES, (h + 1) * LANES)
        qs = _stack_halves(q_ref[0, :, lanes])
        kn = _pad_rows(kn_ref[0, :, lanes], LANES)
        vn = _pad_rows(vn_ref[0, :, lanes], LANES)
        sc = _dot_nt(qs, kc_ref[0, :, lanes].astype(BF16))
        sn = _dot_nt(qs, kn)
        col = lax.broadcasted_iota(jnp.int32, sn.shape, 1)
        sn = jnp.where(col < t, sn, NEG)
        m = jnp.maximum(jnp.max(sc, axis=-1, keepdims=True), jnp.max(sn, axis=-1, keepdims=True))
        pc = jnp.exp(sc - m)
        pn = jnp.exp(sn - m)
        l = jnp.sum(pc, axis=-1, keepdims=True) + jnp.sum(pn, axis=-1, keepdims=True)
        acc = _dot(pc.astype(BF16), vc_ref[0, :, lanes].astype(BF16)) + _dot(pn.astype(BF16), vn)
        o = acc * (1.0 / l)
        o_ref[0, :, lanes] = _diff_finish(o[:t], o[t:], lam, gsub_ref[...], lam_init)


def _diff_sample(lamv, gsub, q, kc, kn, vc, vn, lam_init):
    b, t, d = q.shape
    past = kc.shape[1]
    w = SAMPLE_HEADS * LANES
    new = pl.BlockSpec((1, t, w), lambda bi, hi: (bi, 0, hi))
    old = pl.BlockSpec((1, past, w), lambda bi, hi: (bi, 0, hi))
    return pl.pallas_call(
        functools.partial(_diff_sample_kernel, lam_init=lam_init),
        grid=(b, d // w),
        in_specs=[pl.BlockSpec(lamv.shape, lambda bi, hi: (0, 0)),
                  pl.BlockSpec((1, A_DV), lambda bi, hi: (0, 0)), new, old, new, old, new],
        out_specs=new,
        out_shape=jax.ShapeDtypeStruct((b, t, d), BF16),
        compiler_params=_params("parallel", "parallel", fuse_inputs=[False, False, False, True, True, True, True]),
        name="diff_attn_sample",
    )(lamv, gsub, q, kc, kn, vc, vn)


def _moe_gates(hf, wr_ref, br_ref):
    hi = hf.astype(BF16)
    lo = (hf - hi.astype(F32)).astype(BF16)
    both = _dot(hi, wr_ref[...])
    lg = both[:, :LANES] + both[:, LANES:] + _dot(lo, wr_ref[:, :LANES]) + br_ref[...]
    lane = lax.broadcasted_iota(jnp.int32, lg.shape, 1).astype(F32)
    big = float(LANES)

    def first_argmax(v):
        mx = jnp.max(v, axis=-1, keepdims=True)
        return mx, jnp.min(jnp.where(v == mx, lane, big), axis=-1, keepdims=True)

    is_g = lane < N_GROUPS
    gmax, gidx = first_argmax(jnp.where(is_g, lg, NEG))
    gval = 1.0 / jnp.sum(jnp.where(is_g, jnp.exp(lg - gmax), 0.0), axis=-1, keepdims=True)
    lo = N_GROUPS + EXPERTS_PER_GROUP * gidx
    el = jnp.where((lane >= lo) & (lane < lo + EXPERTS_PER_GROUP), lg, NEG)
    v1, i1 = first_argmax(el)
    v2, i2 = first_argmax(jnp.where(lane == i1, NEG, el))
    e2 = jnp.exp(v2 - v1)
    inv = gval / (1.0 + e2)
    gates = jnp.where(lane == i1 - lo, inv, 0.0) + jnp.where(lane == i2 - lo, inv * e2, 0.0)
    return jnp.where(lane == gidx, 1.0, 0.0), gates


def _moe_kernel(a_ref, wo_ref, x_ref, g_ref, wr_ref, br_ref, wg_ref, wu_ref, wd_ref, gf_ref, o_ref,
                h_ref, gsplit_ref, rank_ref, rank_t_ref, pos_ref, y_ref, base_ref, *, final_norm):
    grp = pl.program_id(1)
    nb = x_ref.shape[0]

    @pl.when(grp == 0)
    def _():
        x = x_ref[...] + _dot(a_ref[...], wo_ref[...])
        o_ref[...] = x
        hf = _rms_scale(x) * g_ref[...]
        h_ref[...] = hf.astype(BF16)
        member, gates = _moe_gates(hf, wr_ref, br_ref)
        rest = gates
        split = jnp.zeros_like(gates)
        for i in range(3):
            part = rest.astype(BF16).astype(F32)
            split = split + pltpu.roll(part, EXPERTS_PER_GROUP * i, 1)
            rest = rest - part
        gsplit_ref[...] = split.astype(BF16)
        earlier = (lax.broadcasted_iota(jnp.int32, (nb, nb), 1) < lax.broadcasted_iota(jnp.int32, (nb, nb), 0))
        rank = jnp.where(member > 0.0, _dot(jnp.where(earlier, 1.0, 0.0).astype(BF16), member.astype(BF16)) + 1.0, 0.0)
        rank_ref[...] = rank
        rank_t_ref[...] = rank.T
        pos_ref[...] = jnp.full(pos_ref.shape, -1.0, F32)
        y_ref[...] = jnp.zeros_like(y_ref)
        base_ref[0] = 0

    base = base_ref[0]
    rrow = rank_t_ref[pl.ds(grp, 1), :]
    n_chunks = (jnp.max(rrow).astype(jnp.int32) + MOE_CHUNK - 1) // MOE_CHUNK
    lane = lax.broadcasted_iota(jnp.int32, (nb, LANES), 1)
    rcol = jnp.sum(jnp.where(lane == grp, rank_ref[...], 0.0), axis=-1, keepdims=True)
    pos_ref[...] = jnp.where(rcol > 0.0, base.astype(F32) + rcol - 1.0, pos_ref[...])

    def chunk(c, carry):
        want = (lax.broadcasted_iota(jnp.int32, (MOE_CHUNK, nb), 0) + (c * MOE_CHUNK + 1)).astype(F32)
        onehot = jnp.where(rrow == want, 1.0, 0.0).astype(BF16)
        xg = _dot(onehot, h_ref[...]).astype(BF16)
        gs3 = _dot(onehot, gsplit_ref[...])
        gs = (gs3 + pltpu.roll(gs3, LANES - EXPERTS_PER_GROUP, 1)
              + pltpu.roll(gs3, LANES - 2 * EXPERTS_PER_GROUP, 1))
        parts = []
        for j in range(EXPERTS_PER_GROUP):
            a = _dot(xg, wg_ref[j])
            u = _dot(xg, wu_ref[j])
            parts.append((a * (1.0 / (1.0 + jnp.exp(-a))) * u * gs[:, j:j + 1]).astype(BF16))
        y = _dot(jnp.concatenate(parts, axis=1), wd_ref[0])
        off = pl.multiple_of(base + c * MOE_CHUNK, MOE_CHUNK)
        y_ref[pl.ds(off, MOE_CHUNK), :] = y.astype(BF16)
        return carry

    lax.fori_loop(0, n_chunks, chunk, 0)
    base_ref[0] = base + n_chunks * MOE_CHUNK

    @pl.when(grp == N_GROUPS - 1)
    def _():
        slot = lax.broadcasted_iota(jnp.int32, (nb, y_ref.shape[0]), 1).astype(F32)
        scatter = jnp.where(pos_ref[...] == slot, 1.0, 0.0).astype(BF16)
        out = o_ref[...] + _dot(scatter, y_ref[...])
        if final_norm:
            out = _rms_scale(out) * gf_ref[...]
        o_ref[...] = out


def _moe(a, wo, x, g, wr, br, wg, wu, wd, gf, final_norm):
    n, d = x.shape
    tm = min(MOE_TILE, n)
    ff = EXPERTS_PER_GROUP * EXPERT_FF
    slots = tm + N_GROUPS * MOE_CHUNK
    once = pl.Buffered(1)
    tok = pl.BlockSpec((tm, d), lambda i, e: (i, 0))
    vec = pl.BlockSpec((1, d), lambda i, e: (0, 0))
    experts = pl.BlockSpec((EXPERTS_PER_GROUP, d, EXPERT_FF), lambda i, e: (e, 0, 0))
    return pl.pallas_call(
        functools.partial(_moe_kernel, final_norm=final_norm),
        grid=(n // tm, N_GROUPS),
        in_specs=[tok, pl.BlockSpec((d, d), lambda i, e: (0, 0), pipeline_mode=once), tok, vec,
                  pl.BlockSpec((d, 2 * LANES), lambda i, e: (0, 0), pipeline_mode=once),
                  pl.BlockSpec((1, LANES), lambda i, e: (0, 0)),
                  experts, experts,
                  pl.BlockSpec((1, ff, d), lambda i, e: (e, 0, 0)),
                  vec],
        out_specs=tok,
        out_shape=jax.ShapeDtypeStruct((n, d), F32),
        scratch_shapes=[pltpu.VMEM((tm, d), BF16), pltpu.VMEM((tm, LANES), BF16),
                        pltpu.VMEM((tm, LANES), F32), pltpu.VMEM((LANES, tm), F32),
                        pltpu.VMEM((tm, 1), F32), pltpu.VMEM((slots, d), BF16),
                        pltpu.SMEM((1,), jnp.int32)],
        compiler_params=_params("parallel", "arbitrary"),
        name="hier_moe",
    )(a, wo, x, g, wr, br, wg, wu, wd, gf)


def _kvq_b_kernel(x_ref, gkv_ref, gq_ref, wkv_ref, wq_ref, k_ref, v_ref, q_ref, *, lead_pad):
    def compute():
        xh = _rms_scale(x_ref[0])
        kv = _dot((xh * gkv_ref[...]).astype(BF16), wkv_ref[...])
        k_ref[0] = kv[:, :D_MODEL].astype(k_ref.dtype)
        v_ref[0] = kv[:, D_MODEL:].astype(v_ref.dtype)
        q = _dot((xh * gq_ref[...]).astype(BF16), wq_ref[...])
        q_ref[0] = (q * (B_DH ** -0.5 * LOG2E)).astype(BF16)

    if lead_pad:
        i = pl.program_id(1)

        @pl.when(i == 0)
        def _():
            k_ref[...] = jnp.zeros_like(k_ref)
            v_ref[...] = jnp.zeros_like(v_ref)

        pl.when(i > 0)(compute)
    else:
        compute()


def _kvq_b(x3, gkv, gq, wkv, wq, kv_dtype, lead_pad, row_block=None):
    b, s, d = x3.shape
    tm = min(ROW_TILE, s)
    if lead_pad:
        assert tm == B_WINDOW
        grid = (b, s // tm + 1)
        src = lambda bi, i: (bi, jnp.maximum(i - 1, 0), 0)
        dst = lambda bi, i: (bi, i, 0)
        s_kv, s_q = s + B_WINDOW, s
    elif row_block is not None:
        grid = (b, 1)
        src = lambda bi, i: (bi, row_block, 0)
        dst = lambda bi, i: (bi, 0, 0)
        s_kv = s_q = tm
    else:
        grid = (b, s // tm)
        src = dst = lambda bi, i: (bi, i, 0)
        s_kv = s_q = s
    vec = pl.BlockSpec((1, d), lambda bi, i: (0, 0))
    return pl.pallas_call(
        functools.partial(_kvq_b_kernel, lead_pad=lead_pad),
        grid=grid,
        in_specs=[pl.BlockSpec((1, tm, d), src), vec, vec,
                  pl.BlockSpec((d, 2 * d), lambda bi, i: (0, 0)),
                  pl.BlockSpec((d, d), lambda bi, i: (0, 0))],
        out_specs=[pl.BlockSpec((1, tm, d), dst), pl.BlockSpec((1, tm, d), dst),
                   pl.BlockSpec((1, tm, d), src if lead_pad else dst)],
        out_shape=[jax.ShapeDtypeStruct((b, s_kv, d), kv_dtype), jax.ShapeDtypeStruct((b, s_kv, d), kv_dtype),
                   jax.ShapeDtypeStruct((b, s_q, d), BF16)],
        compiler_params=_params("parallel", "arbitrary"),
        name="kvq_b",
    )(x3, gkv, gq, wkv, wq)


def _band_bias_kernel(u_ref, o_ref):
    x = jnp.broadcast_to(u_ref[0], (BAND_TQ, 2 * B_WINDOW))
    bias = pltpu.roll(x, 0, 1, stride=1, stride_axis=0)[:, :BAND_KW]
    qc = _chunk_of(lax.broadcasted_iota(jnp.int32, bias.shape, 0))
    kc = _chunk_of(lax.broadcasted_iota(jnp.int32, bias.shape, 1))
    ok = (kc >= qc) & (kc <= qc + B_WINDOW // CHUNK)
    o_ref[0] = jnp.where(ok, bias * LOG2E, NEG)


def _band_bias(rel_bias):
    edge = jnp.broadcast_to(rel_bias[:, 2 * REL_CLIP:], (B_HEADS, REL_CLIP))
    u = jnp.concatenate([edge, rel_bias[:, :0:-1], edge], axis=1)[:, None, :]
    assert u.shape[-1] == 2 * B_WINDOW
    return pl.pallas_call(
        _band_bias_kernel,
        grid=(B_HEADS,),
        in_specs=[pl.BlockSpec((1, 1, 2 * B_WINDOW), lambda h: (h, 0, 0))],
        out_specs=pl.BlockSpec((1, BAND_TQ, BAND_KW), lambda h: (h, 0, 0)),
        out_shape=jax.ShapeDtypeStruct((B_HEADS, BAND_TQ, BAND_KW), F32),
        compiler_params=_params("parallel"),
        name="band_bias",
    )(u)


def _pick_halves(o, t):
    lane = lax.broadcasted_iota(jnp.int32, (t, LANES), 1)
    return jnp.where(lane < HEAD_W, o[:t], o[t:])


def _band_prompt_kernel(q_ref, k_ref, v_ref, bias_ref, o_ref, p_ref, *, n_tiles):
    t = BAND_TQ

    def probabilities(off, lead):
        kw = k_ref[0, pl.ds(off, BAND_KW), :]
        qs = _stack_halves(q_ref[0, pl.ds(off, t), :])
        for g in range(2 * t // BAND_ROWS):
            r0 = g * BAND_ROWS
            s = _dot_nt(qs[r0:r0 + BAND_ROWS], kw) + bias_ref[r0 // t, r0 % t:r0 % t + BAND_ROWS, :]
            if lead:
                c = lax.broadcasted_iota(jnp.int32, s.shape, 1)
                s = jnp.where(c >= B_WINDOW - off, s, NEG)
            m = jnp.max(s, axis=-1, keepdims=True)
            p_ref[r0:r0 + BAND_ROWS, :] = jnp.exp2(s - m).astype(BF16)

    def outputs(off):
        vx = jnp.concatenate([v_ref[0, pl.ds(off, BAND_KW), :], jnp.ones((BAND_KW, LANES), BF16)], axis=1)
        o = _dot(p_ref[...], vx)
        o = o[:, :LANES] * (1.0 / o[:, LANES:])
        o_ref[0, pl.ds(off, t), :] = _pick_halves(o, t).astype(BF16)

    n_lead = B_WINDOW // t
    for ti in range(n_lead):
        if ti > 0:
            outputs((ti - 1) * t)
        probabilities(ti * t, True)

    def body(ti, carry):
        outputs(pl.multiple_of((ti - 1) * t, t))
        probabilities(pl.multiple_of(ti * t, t), False)
        return carry

    lax.fori_loop(n_lead, n_tiles, body, 0)
    outputs((n_tiles - 1) * t)


def _band_prompt(q, kpad, vpad, bias):
    b, s, d = q.shape
    t = BAND_TQ
    nhp = d // LANES
    win = pl.BlockSpec((1, s + B_WINDOW, LANES), lambda hp, bi: (bi, 0, hp))
    seq = pl.BlockSpec((1, s, LANES), lambda hp, bi: (bi, 0, hp))
    return pl.pallas_call(
        functools.partial(_band_prompt_kernel, n_tiles=s // t),
        grid=(nhp, b),
        in_specs=[seq, win, win, pl.BlockSpec((2, t, BAND_KW), lambda hp, bi: (hp, 0, 0))],
        out_specs=seq,
        out_shape=jax.ShapeDtypeStruct((b, s, d), BF16),
        scratch_shapes=[pltpu.VMEM((2 * t, BAND_KW), BF16)],
        compiler_params=_params("parallel", "parallel"),
        name="band_attn_prompt",
    )(q, kpad, vpad, bias)


def _band_sample_kernel(q_ref, kc_ref, kn_ref, vc_ref, vn_ref, bias_ref, o_ref):
    t = q_ref.shape[1]
    wb = kc_ref.shape[1]
    col = lax.broadcasted_iota(jnp.int32, (2 * t, LANES), 1)
    for hp in range(q_ref.shape[2] // LANES):
        lanes = slice(hp * LANES, (hp + 1) * LANES)
        qs = _stack_halves(q_ref[0, :, lanes])
        kn = _pad_rows(kn_ref[0, :, lanes].astype(BF16), LANES)
        vn = _pad_rows(vn_ref[0, :, lanes].astype(BF16), LANES)
        bias = bias_ref[2 * hp:2 * hp + 2].reshape(2 * t, BAND_KW)
        sc = _dot_nt(qs, kc_ref[0, :, lanes].astype(BF16)) + bias[:, :wb]
        sn = jnp.where(col < t, _dot_nt(qs, kn) + bias[:, wb:wb + LANES], NEG)
        m = jnp.maximum(jnp.max(sc, axis=-1, keepdims=True), jnp.max(sn, axis=-1, keepdims=True))
        pc = jnp.exp2(sc - m)
        pn = jnp.exp2(sn - m)
        l = jnp.sum(pc, axis=-1, keepdims=True) + jnp.sum(pn, axis=-1, keepdims=True)
        o = (_dot(pc.astype(BF16), vc_ref[0, :, lanes].astype(BF16)) + _dot(pn.astype(BF16), vn)) * (1.0 / l)
        o_ref[0, :, lanes] = _pick_halves(o, t).astype(BF16)


def _band_sample(q, kc, kn, vc, vn, bias):
    b, t, d = q.shape
    wb = kc.shape[1]
    assert wb == B_WINDOW
    new = pl.BlockSpec((1, t, d), lambda bi: (bi, 0, 0))
    old = pl.BlockSpec((1, wb, d), lambda bi: (bi, 0, 0))
    return pl.pallas_call(
        _band_sample_kernel,
        grid=(b,),
        in_specs=[new, old, new, old, new, pl.BlockSpec((B_HEADS, t, BAND_KW), lambda bi: (0, 0, 0))],
        out_specs=new,
        out_shape=jax.ShapeDtypeStruct((b, t, d), BF16),
        compiler_params=_params("parallel", fuse_inputs=[False, True, False, True, False, False]),
        name="band_attn_sample",
    )(q, kc, kn, vc, vn, bias)


def kernel(x_prompt, x_sample, cache_k_a, cache_v_a, cache_k_b, cache_v_b, g_attn, g_ffn, w_q_a, w_k_a, w_v_a, lam_q1, lam_k1, lam_q2, lam_k2, g_sub_a, w_o_a, g_kv, w_kv_b, w_q_b, rel_bias_b, w_o_b, w_router_g, b_router_g, w_router_e, b_router_e, w_gate, w_up, w_down, g_final):
    bp, sp, d = x_prompt.shape
    bs, ts, _ = x_sample.shape
    past = cache_k_a.shape[2]
    wb = cache_k_b.shape[1]
    n_p, n_s = bp * sp, bs * ts
    assert d == D_MODEL and sp % ROW_TILE == 0 and ROW_TILE % ts == 0 and ts <= LANES
    depth = g_attn.shape[0]
    n_a = w_q_a.shape[0]
    assert depth == 2 and n_a == 1 and w_q_b.shape[0] == 1

    row = lambda v: v.reshape(1, -1).astype(F32)
    bf = lambda w: w.astype(BF16)

    def moe_weights(l):
        wr = jnp.concatenate([w_router_g[l], w_router_e[l].transpose(1, 0, 2).reshape(d, N_EXPERTS)], axis=1)
        wr = jnp.pad(wr, ((0, 0), (0, LANES - wr.shape[1])))
        br = jnp.concatenate([b_router_g[l], b_router_e[l].reshape(-1)])
        br = jnp.pad(br, (0, LANES - br.shape[0]))[None, :]
        wr = wr.astype(F32)
        wr_hi = bf(wr)
        wr_split = jnp.concatenate([wr_hi, bf(wr - wr_hi.astype(F32))], axis=1)
        wd = bf(w_down[l]).reshape(N_GROUPS, EXPERTS_PER_GROUP * EXPERT_FF, d)
        return row(g_ffn[l]), wr_split, br.astype(F32), bf(w_gate[l]), bf(w_up[l]), wd

    xp = x_prompt.reshape(n_p, d)
    xs = x_sample.reshape(n_s, d)

    lam_init = 0.8 - 0.6 * math.exp(-0.3 * 0)
    lamv = jnp.stack([lam_q1[0], lam_k1[0], lam_q2[0], lam_k2[0]]).astype(F32)
    gsub = row(g_sub_a[0])
    wq, wk, wv, wo = bf(w_q_a[0]), bf(w_k_a[0]), bf(w_v_a[0]), bf(w_o_a[0])
    g0 = row(g_attn[0])

    cos_p, sin_p = _rope_table(sp, sp, 0, min(ROW_TILE, sp))
    qp, kfp, kbp, vfp, vbp = _qkv_a(x_prompt, g0, wq, wk, wv, cos_p, sin_p, True, A_DK ** -0.5 * LOG2E)
    ap = _diff_prompt(lamv, gsub, qp, kbp, vbp, lam_init)

    tm_s = min(ROW_TILE, n_s)
    cos_s, sin_s = _rope_table(tm_s, ts, past, tm_s)
    qs, kfs, _, vfs, _ = _qkv_a(xs.reshape(1, n_s, d), g0, wq, wk, wv, cos_s, sin_s, False, A_DK ** -0.5)
    kfs3, vfs3 = kfs.reshape(bs, ts, d), vfs.reshape(bs, ts, d)
    a_s = _diff_sample(lamv, gsub, qs.reshape(bs, ts, d), cache_k_a[0].reshape(bs, past, d), bf(kfs3),
                       cache_v_a[0].reshape(bs, past, d), bf(vfs3), lam_init)

    mw = moe_weights(0)
    gfin = row(g_final)
    xp = _moe(ap.reshape(n_p, d), wo, xp, *mw, gfin, final_norm=False)
    xs = _moe(a_s.reshape(n_s, d), wo, xs, *mw, gfin, final_norm=False)

    gkv, g1 = row(g_kv), row(g_attn[1])
    wkv, wqb, wob = bf(w_kv_b), bf(w_q_b[0]), bf(w_o_b[0])
    bias = _band_bias(rel_bias_b[0].astype(F32))

    xp3 = xp.reshape(bp, sp, d)
    kpad, vpad, qb = _kvq_b(xp3, gkv, g1, wkv, wqb, BF16, lead_pad=True)
    kbt, vbt, _ = _kvq_b(xp3, gkv, g1, wkv, wqb, F32, lead_pad=False, row_block=sp // ROW_TILE - 1)
    ab = _band_prompt(qb, kpad, vpad, bias)

    kns, vns, qbs = _kvq_b(xs.reshape(1, n_s, d), gkv, g1, wkv, wqb, F32, lead_pad=False)
    kns3, vns3 = kns.reshape(bs, ts, d), vns.reshape(bs, ts, d)
    abs_ = _band_sample(qbs.reshape(bs, ts, d), cache_k_b.reshape(bs, wb, d), kns3,
                        cache_v_b.reshape(bs, wb, d), vns3, bias)

    mw = moe_weights(1)
    y_prompt = _moe(ab.reshape(n_p, d), wob, xp, *mw, gfin, final_norm=True).reshape(bp, sp, d)
    y_sample = _moe(abs_.reshape(n_s, d), wob, xs, *mw, gfin, final_norm=True).reshape(bs, ts, d)

    wp = min(B_WINDOW, sp)
    assert wp == ROW_TILE
    k_a_prompt = kfp.reshape(1, bp, sp, 2 * A_HEADS, A_DK)
    v_a_prompt = vfp.reshape(1, bp, sp, A_HEADS, A_DV)
    k_a_sample = kfs.reshape(1, bs, ts, 2 * A_HEADS, A_DK)
    v_a_sample = vfs.reshape(1, bs, ts, A_HEADS, A_DV)
    k_b_prompt = kbt.reshape(bp, wp, B_HEADS, B_DH)
    v_b_prompt = vbt.reshape(bp, wp, B_HEADS, B_DH)
    k_b_sample = jnp.concatenate([cache_k_b[:, ts:], kns.reshape(bs, ts, B_HEADS, B_DH)], axis=1)
    v_b_sample = jnp.concatenate([cache_v_b[:, ts:], vns.reshape(bs, ts, B_HEADS, B_DH)], axis=1)
    return (y_prompt, y_sample, k_a_prompt, v_a_prompt, k_b_prompt, v_b_prompt,
            k_a_sample, v_a_sample, k_b_sample, v_b_sample)
```
